```python
import math
import jax, jax.numpy as jnp
from jax import lax
import numpy as np

D_MODEL = 1024
BATCH = 32
SEQ = 256
DEPTH = 4
DEC_BATCH = 2
DEC_SEQ = 2048
PAST_LEN = 256

GRID_W = 64
N_MIXERS = 3
N_GQA_LAYERS = (DEPTH + 2) // 3
N_DIFF_LAYERS = (DEPTH + 1) // 3
N_MLSTM_LAYERS = DEPTH // 3

HEAD_DIM = 64
Q_BLOCK = 128
ROPE_THETA = 10000.0
NORM_EPS = 1e-6

GQA_HEADS = 16
GQA_KV_HEADS = 4
GQA_GROUP = GQA_HEADS // GQA_KV_HEADS
GQA_Q_W = GQA_HEADS * HEAD_DIM
GQA_KV_W = GQA_KV_HEADS * HEAD_DIM
GQA_BRANCH = GQA_Q_W
GQA_IN = GQA_Q_W + 2 * GQA_KV_W + GQA_BRANCH

DIFF_HEADS = 8
DIFF_QK_W = DIFF_HEADS * 2 * HEAD_DIM
DIFF_V_DIM = 2 * HEAD_DIM
DIFF_BRANCH = DIFF_HEADS * DIFF_V_DIM
DIFF_IN = 2 * DIFF_QK_W + 2 * DIFF_BRANCH

MLSTM_INNER = 2 * D_MODEL
MLSTM_HEADS = 4
MLSTM_HEAD_DIM = MLSTM_INNER // MLSTM_HEADS
MLSTM_QKV_BLOCK = 4
MLSTM_N_BLOCKS = MLSTM_INNER // MLSTM_QKV_BLOCK
MLSTM_CONV_K = 4
MLSTM_CHUNK = 64
MLSTM_IN = 2 * MLSTM_INNER

kernel_name = "hybrid_diffusion_gqa_diff_mlstm_step"


def rms_norm(x, g, eps=NORM_EPS):
    xf = x.astype(jnp.float32)
    y = xf * lax.rsqrt(jnp.mean(xf * xf, axis=-1, keepdims=True) + eps)
    return (y * g.astype(jnp.float32)).astype(x.dtype)


def adaln(cond, w_mod, b_mod):
    mod = jax.nn.silu(cond) @ w_mod + b_mod
    mod = mod.reshape(-1, 1, 3 * D_MODEL)
    return jnp.split(mod, 3, axis=-1)


def axial_rope_tables(rows):
    t = jnp.arange(rows * GRID_W)
    row = (t // GRID_W).astype(jnp.float32)
    col = (t % GRID_W).astype(jnp.float32)
    n_freq = HEAD_DIM // 4
    inv = ROPE_THETA ** (-jnp.arange(n_freq, dtype=jnp.float32) / n_freq)
    ang = jnp.concatenate([row[:, None] * inv, col[:, None] * inv], axis=-1)
    return jnp.cos(ang), jnp.sin(ang)


def apply_rope(x, cos, sin):
    shp = (1, x.shape[1]) + (1,) * (x.ndim - 3) + (cos.shape[-1],)
    c = cos.reshape(shp)
    s = sin.reshape(shp)
    xf = x.astype(jnp.float32)
    x1, x2 = jnp.split(xf, 2, axis=-1)
    return jnp.concatenate([x1 * c - x2 * s, x2 * c + x1 * s], axis=-1).astype(x.dtype)


def map_query_blocks(fn, q):
    B, S = q.shape[:2]
    nb = S // Q_BLOCK
    qb = jnp.moveaxis(q.reshape((B, nb, Q_BLOCK) + q.shape[2:]), 1, 0)
    o = lax.map(fn, qb)
    return jnp.moveaxis(o, 0, 1).reshape((B, S) + o.shape[3:])


def gated_out(o, z, w_out):
    B, S = z.shape[:2]
    return (o.reshape(B, S, -1) * jax.nn.silu(z)) @ w_out


def gqa_attention(q, k, v):
    scale = HEAD_DIM ** -0.5

    def block(qb):
        s = jnp.einsum('bqhgd,bkhd->bhgqk', qb, k).astype(jnp.float32) * scale
        p = jax.nn.softmax(s, axis=-1).astype(v.dtype)
        return jnp.einsum('bhgqk,bkhd->bqhgd', p, v)

    return map_query_blocks(block, q)


def gqa_project(h, w_in, q_g, k_g):
    B, S, _ = h.shape
    q, k, v, z = jnp.split(h @ w_in, [GQA_Q_W, GQA_Q_W + GQA_KV_W, GQA_Q_W + 2 * GQA_KV_W], axis=-1)
    q = rms_norm(q.reshape(B, S, GQA_KV_HEADS, GQA_GROUP, HEAD_DIM), q_g)
    k = rms_norm(k.reshape(B, S, GQA_KV_HEADS, HEAD_DIM), k_g)
    return q, k, v.reshape(B, S, GQA_KV_HEADS, HEAD_DIM), z


def gqa_context(h, w_in, q_g, k_g, w_out):
    q, k, v, z = gqa_project(h, w_in, q_g, k_g)
    return gated_out(gqa_attention(q, k, v), z, w_out), k, v


def gqa_latent(h, k_ctx, v_ctx, cos, sin, w_in, q_g, k_g, w_out):
    q, k, v, z = gqa_project(h, w_in, q_g, k_g)
    q = apply_rope(q, cos, sin)
    k = apply_rope(k, cos, sin)
    k_all = jnp.concatenate([k, k_ctx.astype(k.dtype)], axis=1)
    v_all = jnp.concatenate([v, v_ctx.astype(v.dtype)], axis=1)
    return gated_out(gqa_attention(q, k_all, v_all), z, w_out)


def diff_lambda_init(layer_idx):
    return 0.8 - 0.6 * math.exp(-0.3 * layer_idx)


def diff_lambda(lq1, lk1, lq2, lk2, lam_init):
    f = lambda a: a.astype(jnp.float32)
    return jnp.exp(jnp.sum(f(lq1) * f(lk1))) - jnp.exp(jnp.sum(f(lq2) * f(lk2))) + lam_init


def diff_attention(q, k, v, lam):
    scale = HEAD_DIM ** -0.5

    def block(qb):
        s = jnp.einsum('bqhjd,bkhjd->bhjqk', qb, k).astype(jnp.float32) * scale
        p = jax.nn.softmax(s, axis=-1)
        a = (p[:, :, 0] - lam * p[:, :, 1]).astype(v.dtype)
        return jnp.einsum('bhqk,bkhe->bqhe', a, v)

    return map_query_blocks(block, q)


def diff_project(h, w_in):
    B, S, _ = h.shape
    q, k, v, z = jnp.split(h @ w_in, [DIFF_QK_W, 2 * DIFF_QK_W, 2 * DIFF_QK_W + DIFF_BRANCH], axis=-1)
    shp = (B, S, DIFF_HEADS, 2, HEAD_DIM)
    return q.reshape(shp), k.reshape(shp), v.reshape(B, S, DIFF_HEADS, DIFF_V_DIM), z


def diff_out(o, z, subln_g, lam_init, w_out):
    o = rms_norm(o, subln_g) * (1.0 - lam_init)
    return gated_out(o, z, w_out)


def diff_context(h, lam, lam_init, w_in, subln_g, w_out):
    q, k, v, z = diff_project(h, w_in)
    return diff_out(diff_attention(q, k, v, lam), z, subln_g, lam_init, w_out), k, v


def diff_latent(h, k_ctx, v_ctx, cos, sin, lam, lam_init, w_in, subln_g, w_out):
    q, k, v, z = diff_project(h, w_in)
    q = apply_rope(q, cos, sin)
    k = apply_rope(k, cos, sin)
    k_all = jnp.concatenate([k, k_ctx.astype(k.dtype)], axis=1)
    v_all = jnp.concatenate([v, v_ctx.astype(v.dtype)], axis=1)
    return diff_out(diff_attention(q, k_all, v_all, lam), z, subln_g, lam_init, w_out)


def short_conv(x, w, b):
    pad_l = MLSTM_CONV_K // 2
    y = lax.conv_general_dilated(x, w[:, None, :].astype(x.dtype), window_strides=(1,),
                                 padding=[(pad_l, MLSTM_CONV_K - 1 - pad_l)],
                                 dimension_numbers=('NWC', 'WIO', 'NWC'),
                                 feature_group_count=x.shape[-1])
    return y + b


def headwise(x, w):
    B, S, _ = x.shape
    y = jnp.einsum('bsnc,ncd->bsnd', x.reshape(B, S, MLSTM_N_BLOCKS, MLSTM_QKV_BLOCK), w)
    return y.reshape(B, S, MLSTM_INNER)


def mlstm_scan(q, k, v, i_pre, f_pre, C0, n0, m0):
    B, S, H, DH = q.shape
    L = MLSTM_CHUNK
    nc = S // L

    def chunks(a):
        a = a.astype(jnp.float32).reshape((B, nc, L, H) + a.shape[3:])
        return jnp.moveaxis(jnp.moveaxis(a, 1, 0), 3, 2)

    tril = jnp.tril(jnp.ones((L, L), dtype=bool))

    def step(carry, inp):
        C, n, m = carry
        qc, kc, vc, ic, fc = inp
        b = jnp.cumsum(jax.nn.log_sigmoid(fc), axis=-1)
        dmat = jnp.where(tril, b[..., :, None] - b[..., None, :] + ic[..., None, :], -jnp.inf)
        inter = b + m[..., None]
        m_t = jnp.maximum(jnp.max(dmat, axis=-1), inter)
        s = jnp.einsum('bhtd,bhsd->bhts', qc, kc) * jnp.exp(dmat - m_t[..., None])
        w_inter = jnp.exp(inter - m_t)
        num = jnp.einsum('bhts,bhse->bhte', s, vc) + w_inter[..., None] * jnp.einsum('bhtd,bhde->bhte', qc, C)
        den = jnp.sum(s, axis=-1) + w_inter * jnp.einsum('bhtd,bhd->bht', qc, n)
        h = num / jnp.maximum(jnp.abs(den), jnp.exp(-m_t))[..., None]
        g_s = b[..., -1:] - b + ic
        m_new = jnp.maximum(b[..., -1] + m, jnp.max(g_s, axis=-1))
        decay = jnp.exp(b[..., -1] + m - m_new)
        kw = kc * jnp.exp(g_s - m_new[..., None])[..., None]
        C_new = decay[..., None, None] * C + jnp.einsum('bhsd,bhse->bhde', kw, vc)
        n_new = decay[..., None] * n + jnp.sum(kw, axis=2)
        return (C_new, n_new, m_new), h

    init = (C0.astype(jnp.float32), n0.astype(jnp.float32), m0.astype(jnp.float32))
    (C, n, m), h = lax.scan(step, init, (chunks(q), chunks(k), chunks(v), chunks(i_pre), chunks(f_pre)))
    h = jnp.moveaxis(jnp.moveaxis(h, 2, 3), 0, 1).reshape(B, S, H, DH)
    return h.astype(q.dtype), C, n, m


def mlstm_bidir(q, k, v, g_f, g_b, C0, n0, m0):
    H = MLSTM_HEADS
    rev = lambda a: jnp.flip(a, axis=1)
    h_f, Cf, nf, mf = mlstm_scan(q, k, v, g_f[..., :H], g_f[..., H:], C0[:, 0], n0[:, 0], m0[:, 0])
    h_b, Cb, nb, mb = mlstm_scan(rev(q), rev(k), rev(v), rev(g_b[..., :H]), rev(g_b[..., H:]),
                                 C0[:, 1], n0[:, 1], m0[:, 1])
    h = h_f + rev(h_b)
    return h, jnp.stack([Cf, Cb], axis=1), jnp.stack([nf, nb], axis=1), jnp.stack([mf, mb], axis=1)


def mlstm_mixer(h, C0, n0, m0, w_in, conv_w, conv_b, w_q, w_k, w_v, w_gf, b_gf, w_gb, b_gb, mh_g, skip, w_out):
    B, S, _ = h.shape
    x_m, z = jnp.split(h @ w_in, 2, axis=-1)
    x_c = jax.nn.silu(short_conv(x_m, conv_w, conv_b))
    q = headwise(x_c, w_q)
    k = headwise(x_c, w_k)
    v = headwise(x_m, w_v)
    g_in = jnp.concatenate([q, k, v], axis=-1)
    g_f = g_in @ w_gf + b_gf
    g_b = g_in @ w_gb + b_gb
    heads = lambda a: a.reshape(B, S, MLSTM_HEADS, MLSTM_HEAD_DIM)
    hh, C, n, m = mlstm_bidir(heads(q), heads(k * (MLSTM_HEAD_DIM ** -0.5)), heads(v), g_f, g_b, C0, n0, m0)
    hf = hh.astype(jnp.float32)
    mu = jnp.mean(hf, axis=-1, keepdims=True)
    var = jnp.mean(jnp.square(hf - mu), axis=-1, keepdims=True)
    hn = ((hf - mu) * lax.rsqrt(var + NORM_EPS)).reshape(B, S, MLSTM_INNER) * mh_g.astype(jnp.float32)
    y = (hn.astype(h.dtype) + skip * x_c) * jax.nn.silu(z)
    return y @ w_out, C, n, m


def setup_inputs(seed: int = 0) -> dict:
    key = jax.random.key(seed)
    ks = iter(jax.random.split(key, 64))
    nrm = lambda shape, scale: scale * jax.random.normal(next(ks), shape, jnp.float32)
    D = D_MODEL
    H, DH = MLSTM_HEADS, MLSTM_HEAD_DIM
    NG, ND, NM = N_GQA_LAYERS, N_DIFF_LAYERS, N_MLSTM_LAYERS
    gate_in = 3 * MLSTM_INNER

    def gate_bias():
        return jnp.concatenate([nrm((NM, H), 0.1), jnp.linspace(3.0, 6.0, H)[None, :] + nrm((NM, H), 0.1)], axis=-1)

    inputs = {}
    inputs["x_prompt"] = nrm((BATCH, SEQ, D), 1.0)
    inputs["x_sample"] = nrm((DEC_BATCH, DEC_SEQ, D), 1.0)
    inputs["cache_gqa_k"] = nrm((DEC_BATCH, NG, PAST_LEN, GQA_KV_HEADS, HEAD_DIM), 1.0)
    inputs["cache_gqa_v"] = nrm((DEC_BATCH, NG, PAST_LEN, GQA_KV_HEADS, HEAD_DIM), 1.0)
    inputs["cache_diff_k"] = nrm((DEC_BATCH, ND, PAST_LEN, DIFF_HEADS, 2, HEAD_DIM), 1.0)
    inputs["cache_diff_v"] = nrm((DEC_BATCH, ND, PAST_LEN, DIFF_HEADS, DIFF_V_DIM), 1.0)
    inputs["state_mlstm_C"] = nrm((DEC_BATCH, NM, 2, H, DH, DH), 0.05)
    inputs["state_mlstm_n"] = nrm((DEC_BATCH, NM, 2, H, DH), 0.05)
    inputs["state_mlstm_m"] = 1.0 + nrm((DEC_BATCH, NM, 2, H), 0.5)
    inputs["c"] = nrm((DEC_BATCH, D), 1.0)
    inputs["c_ctx"] = nrm((D,), 1.0)
    inputs["norm_g"] = 1.0 + nrm((DEPTH, D), 0.02)
    inputs["w_mod"] = nrm((DEPTH, D, 3 * D), 0.5 * D ** -0.5)
    inputs["b_mod"] = nrm((DEPTH, 3 * D), 0.02)
    inputs["gqa_w_in"] = nrm((NG, D, GQA_IN), D ** -0.5)
    inputs["gqa_q_norm_g"] = 1.0 + nrm((NG, HEAD_DIM), 0.02)
    inputs["gqa_k_norm_g"] = 1.0 + nrm((NG, HEAD_DIM), 0.02)
    inputs["gqa_w_out"] = nrm((NG, GQA_BRANCH, D), GQA_BRANCH ** -0.5)
    inputs["diff_w_in"] = nrm((ND, D, DIFF_IN), D ** -0.5)
    inputs["diff_lambda_q1"] = nrm((ND, HEAD_DIM), 0.1)
    inputs["diff_lambda_k1"] = nrm((ND, HEAD_DIM), 0.1)
    inputs["diff_lambda_q2"] = nrm((ND, HEAD_DIM), 0.1)
    inputs["diff_lambda_k2"] = nrm((ND, HEAD_DIM), 0.1)
    inputs["diff_subln_g"] = 1.0 + nrm((ND, DIFF_V_DIM), 0.02)
    inputs["diff_w_out"] = nrm((ND, DIFF_BRANCH, D), DIFF_BRANCH ** -0.5)
    inputs["mlstm_w_in"] = nrm((NM, D, MLSTM_IN), D ** -0.5)
    inputs["mlstm_conv_w"] = nrm((NM, MLSTM_CONV_K, MLSTM_INNER), MLSTM_CONV_K ** -0.5)
    inputs["mlstm_conv_b"] = nrm((NM, MLSTM_INNER), 0.02)
    inputs["mlstm_w_q"] = nrm((NM, MLSTM_N_BLOCKS, MLSTM_QKV_BLOCK, MLSTM_QKV_BLOCK), MLSTM_QKV_BLOCK ** -0.5)
    inputs["mlstm_w_k"] = nrm((NM, MLSTM_N_BLOCKS, MLSTM_QKV_BLOCK, MLSTM_QKV_BLOCK), MLSTM_QKV_BLOCK ** -0.5)
    inputs["mlstm_w_v"] = nrm((NM, MLSTM_N_BLOCKS, MLSTM_QKV_BLOCK, MLSTM_QKV_BLOCK), MLSTM_QKV_BLOCK ** -0.5)
    inputs["mlstm_w_gate_f"] = nrm((NM, gate_in, 2 * H), 0.1 * gate_in ** -0.5)
    inputs["mlstm_b_gate_f"] = gate_bias()
    inputs["mlstm_w_gate_b"] = nrm((NM, gate_in, 2 * H), 0.1 * gate_in ** -0.5)
    inputs["mlstm_b_gate_b"] = gate_bias()
    inputs["mlstm_mh_norm_g"] = 1.0 + nrm((NM, MLSTM_INNER), 0.02)
    inputs["mlstm_skip"] = 1.0 + nrm((NM, MLSTM_INNER), 0.02)
    inputs["mlstm_w_out"] = nrm((NM, MLSTM_INNER, D), MLSTM_INNER ** -0.5)
    inputs["final_norm_g"] = 1.0 + nrm((D,), 0.02)
    return inputs


def reference(x_prompt, x_sample, cache_gqa_k, cache_gqa_v, cache_diff_k, cache_diff_v,
              state_mlstm_C, state_mlstm_n, state_mlstm_m, c, c_ctx,
              norm_g, w_mod, b_mod,
              gqa_w_in, gqa_q_norm_g, gqa_k_norm_g, gqa_w_out,
              diff_w_in, diff_lambda_q1, diff_lambda_k1, diff_lambda_q2, diff_lambda_k2, diff_subln_g, diff_w_out,
              mlstm_w_in, mlstm_conv_w, mlstm_conv_b, mlstm_w_q, mlstm_w_k, mlstm_w_v,
              mlstm_w_gate_f, mlstm_b_gate_f, mlstm_w_gate_b, mlstm_b_gate_b,
              mlstm_mh_norm_g, mlstm_skip, mlstm_w_out,
              final_norm_g):
    rows = x_sample.shape[1] // GRID_W
    cos, sin = axial_rope_tables(rows)
    xp, xs = x_prompt, x_sample
    bp = xp.shape[0]
    gqa_k_list, gqa_v_list, diff_k_list, diff_v_list = [], [], [], []
    mC_list, mn_list, mm_list = [], [], []
    for i in range(DEPTH):
        kind, j = i % N_MIXERS, i // N_MIXERS
        sh_p, sc_p, gt_p = adaln(c_ctx, w_mod[i], b_mod[i])
        sh_s, sc_s, gt_s = adaln(c, w_mod[i], b_mod[i])
        hp = rms_norm(xp, norm_g[i]) * (1.0 + sc_p) + sh_p
        hs = rms_norm(xs, norm_g[i]) * (1.0 + sc_s) + sh_s
        if kind == 0:
            op, kp, vp = gqa_context(hp, gqa_w_in[j], gqa_q_norm_g[j], gqa_k_norm_g[j], gqa_w_out[j])
            os_ = gqa_latent(hs, cache_gqa_k[:, j], cache_gqa_v[:, j], cos, sin,
                             gqa_w_in[j], gqa_q_norm_g[j], gqa_k_norm_g[j], gqa_w_out[j])
            gqa_k_list.append(kp)
            gqa_v_list.append(vp)
        elif kind == 1:
            lam_init = diff_lambda_init(i)
            lam = diff_lambda(diff_lambda_q1[j], diff_lambda_k1[j], diff_lambda_q2[j], diff_lambda_k2[j], lam_init)
            op, kp, vp = diff_context(hp, lam, lam_init, diff_w_in[j], diff_subln_g[j], diff_w_out[j])
            os_ = diff_latent(hs, cache_diff_k[:, j], cache_diff_v[:, j], cos, sin, lam, lam_init,
                              diff_w_in[j], diff_subln_g[j], diff_w_out[j])
            diff_k_list.append(kp)
            diff_v_list.append(vp)
        else:
            prm = (mlstm_w_in[j], mlstm_conv_w[j], mlstm_conv_b[j], mlstm_w_q[j], mlstm_w_k[j], mlstm_w_v[j],
                   mlstm_w_gate_f[j], mlstm_b_gate_f[j], mlstm_w_gate_b[j], mlstm_b_gate_b[j],
                   mlstm_mh_norm_g[j], mlstm_skip[j], mlstm_w_out[j])
            C0 = jnp.zeros((bp, 2, MLSTM_HEADS, MLSTM_HEAD_DIM, MLSTM_HEAD_DIM), jnp.float32)
            n0 = jnp.zeros((bp, 2, MLSTM_HEADS, MLSTM_HEAD_DIM), jnp.float32)
            m0 = jnp.zeros((bp, 2, MLSTM_HEADS), jnp.float32)
            op, Cp, np_, mp_ = mlstm_mixer(hp, C0, n0, m0, *prm)
            os_, _, _, _ = mlstm_mixer(hs, state_mlstm_C[:, j], state_mlstm_n[:, j], state_mlstm_m[:, j], *prm)
            mC_list.append(Cp)
            mn_list.append(np_)
            mm_list.append(mp_)
        xp = xp + gt_p * op
        xs = xs + gt_s * os_
    y_prompt = rms_norm(xp, final_norm_g)
    y_sample = rms_norm(xs, final_norm_g)
    new_cache_gqa_k = jnp.stack(gqa_k_list, axis=1)
    new_cache_gqa_v = jnp.stack(gqa_v_list, axis=1)
    new_cache_diff_k = jnp.stack(diff_k_list, axis=1)
    new_cache_diff_v = jnp.stack(diff_v_list, axis=1)
    new_state_mlstm_C = jnp.stack(mC_list, axis=1)
    new_state_mlstm_n = jnp.stack(mn_list, axis=1)
    new_state_mlstm_m = jnp.stack(mm_list, axis=1)
    return (y_prompt, y_sample, new_cache_gqa_k, new_cache_gqa_v, new_cache_diff_k, new_cache_diff_v,
            new_state_mlstm_C, new_state_mlstm_n, new_state_mlstm_m)
```

```python
import functools
import math

import jax
import jax.numpy as jnp
from jax import lax
from jax.experimental import pallas as pl
from jax.experimental.pallas import tpu as pltpu

D_MODEL = 1024
DEPTH = 4
GRID_W = 64
N_MIXERS = 3
HEAD_DIM = 64
ROPE_THETA = 10000.0
NORM_EPS = 1e-6

GQA_HEADS = 16
GQA_KV_HEADS = 4
GQA_GROUP = GQA_HEADS // GQA_KV_HEADS
GQA_Q_W = GQA_HEADS * HEAD_DIM
GQA_KV_W = GQA_KV_HEADS * HEAD_DIM

DIFF_HEADS = 8
DIFF_QK_W = DIFF_HEADS * 2 * HEAD_DIM
DIFF_V_DIM = 2 * HEAD_DIM
DIFF_BRANCH = DIFF_HEADS * DIFF_V_DIM

MLSTM_INNER = 2 * D_MODEL
MLSTM_HEADS = 4
MLSTM_HEAD_DIM = MLSTM_INNER // MLSTM_HEADS
MLSTM_QKV_BLOCK = 4
MLSTM_CONV_K = 4

LANES = 128
SUBLANES = 8
VMEM_LIMIT_BYTES = 56 * 1024 * 1024

TOKEN_TILE = 512
PROJ_N_TILE = 256
ATTN_Q_TILE = 256
MLSTM_CHUNK = 256
GATE_W = 4 * MLSTM_HEADS

F32 = jnp.float32
BF16 = jnp.bfloat16


def _params(*semantics):
    return pltpu.CompilerParams(dimension_semantics=semantics, vmem_limit_bytes=VMEM_LIMIT_BYTES)


def _silu(x):
    return x * (1.0 / (1.0 + jnp.exp(-x)))


def _dot(a, b):
    return jnp.dot(a, b, preferred_element_type=F32)


def _dot_nt(a, b):
    return lax.dot_general(a, b, (((1,), (1,)), ((), ())), preferred_element_type=F32)


def _mod_kernel(cond_ref, w_ref, b_ref, o_ref):
    a = _silu(cond_ref[...]).astype(BF16)
    o_ref[...] = _dot(a, w_ref[...].astype(BF16)) + b_ref[...]


def adaln_all(cond, w_mod, b_mod):
    tn = 1024
    return pl.pallas_call(
        _mod_kernel,
        grid=(DEPTH, 3 * D_MODEL // tn),
        in_specs=[
            pl.BlockSpec((SUBLANES, D_MODEL), lambda i, n: (0, 0)),
            pl.BlockSpec((None, D_MODEL, tn), lambda i, n: (i, 0, n)),
            pl.BlockSpec((None, 1, tn), lambda i, n: (i, 0, n)),
        ],
        out_specs=pl.BlockSpec((None, SUBLANES, tn), lambda i, n: (i, 0, n)),
        out_shape=jax.ShapeDtypeStruct((DEPTH, SUBLANES, 3 * D_MODEL), F32),
        compiler_params=_params("parallel", "parallel"),
        name="adaln_mod",
    )(cond, w_mod, b_mod.reshape(DEPTH, 1, 3 * D_MODEL))


def _swap_half_heads(y):
    w = y.shape[-1]
    lane = lax.broadcasted_iota(jnp.int32, y.shape, 1)
    upper = (lane & (HEAD_DIM // 2)) != 0
    return jnp.where(upper, pltpu.roll(y, HEAD_DIM // 2, 1), pltpu.roll(y, w - HEAD_DIM // 2, 1))


def _proj_in_kernel(*refs, segs, n_prompt_tiles, has_norm, has_rope):
    it = iter(refs)
    x_ref, g_ref, mod_ref, w_ref = next(it), next(it), next(it), next(it)
    if has_rope:
        cos_ref, sin_ref = next(it), next(it)
    if has_norm:
        gmat_ref, qg_ref, kg_ref = next(it), next(it), next(it)
    out_refs = [next(it) for _ in segs]
    h_ref = next(it)

    m = pl.program_id(0)
    n = pl.program_id(1)

    @pl.when(n == 0)
    def _():
        x = x_ref[...]
        y = x * lax.rsqrt(jnp.mean(x * x, axis=-1, keepdims=True) + NORM_EPS) * g_ref[...]
        shift = mod_ref[:, 0:D_MODEL]
        scale = mod_ref[:, D_MODEL:2 * D_MODEL]
        h_ref[...] = (y * (1.0 + scale) + shift).astype(BF16)

    def finish(acc, mode):
        if mode in ("q_norm", "k_norm"):
            ss = _dot((acc * acc).astype(BF16), gmat_ref[...])
            gvec = qg_ref[...] if mode == "q_norm" else kg_ref[...]
            acc = acc * lax.rsqrt(ss * (1.0 / HEAD_DIM) + NORM_EPS) * gvec
        return acc

    def rope(acc):
        return acc * cos_ref[...] + _swap_half_heads(acc) * sin_ref[...]

    start = 0
    for (ntiles, dtype, mode), o_ref in zip(segs, out_refs):
        def body(o_ref=o_ref, mode=mode, dtype=dtype):
            acc = _dot(h_ref[...], w_ref[n])
            acc = finish(acc, mode)
            post = (lambda a: a * (HEAD_DIM ** -0.5)) if mode in ("q_norm", "q") else (lambda a: a)
            if has_rope and mode in ("q_norm", "k_norm", "q", "k"):
                @pl.when(m < n_prompt_tiles)
                def _():
                    o_ref[...] = post(acc).astype(dtype)

                @pl.when(m >= n_prompt_tiles)
                def _():
                    o_ref[...] = post(rope(acc)).astype(dtype)
            else:
                o_ref[...] = post(acc).astype(dtype)

        pl.when((n >= start) & (n < start + ntiles))(body)
        start += ntiles


def proj_in(x, norm_g, mod4, layer, w_tiles, segs, n_prompt, sample_seq, rope_tabs=None, head_norm=None):
    T = x.shape[0]
    tm, tn = TOKEN_TILE, PROJ_N_TILE
    n_prompt_tiles = n_prompt // tm
    tiles_per_seq = sample_seq // tm
    seg_tiles = tuple((w // tn, dt, mode) for (w, dt, mode) in segs)
    n_tiles = sum(s[0] for s in seg_tiles)
    assert w_tiles.shape == (n_tiles, D_MODEL, tn)

    def mod_row(m):
        return jnp.where(m < n_prompt_tiles, 0, 1 + (m - n_prompt_tiles) // tiles_per_seq)

    in_specs = [
        pl.BlockSpec((tm, D_MODEL), lambda m, n: (m, 0)),
        pl.BlockSpec((1, D_MODEL), lambda m, n: (0, 0)),
        pl.BlockSpec((None, None, 1, 3 * D_MODEL), lambda m, n: (layer, mod_row(m), 0, 0)),
        pl.BlockSpec((n_tiles, D_MODEL, tn), lambda m, n: (0, 0, 0)),
    ]
    args = [x, norm_g.reshape(1, D_MODEL), mod4, w_tiles]
    if rope_tabs is not None:
        def pos_block(m, n):
            return (jnp.maximum(m - n_prompt_tiles, 0) % tiles_per_seq, 0)
        in_specs += [pl.BlockSpec((tm, tn), pos_block), pl.BlockSpec((tm, tn), pos_block)]
        args += list(rope_tabs)
    if head_norm is not None:
        in_specs += [pl.BlockSpec((tn, tn), lambda m, n: (0, 0)),
                     pl.BlockSpec((1, tn), lambda m, n: (0, 0)),
                     pl.BlockSpec((1, tn), lambda m, n: (0, 0))]
        args += list(head_norm)

    out_specs, out_shape = [], []
    start = 0
    for ntl, dt, _ in seg_tiles:
        out_specs.append(pl.BlockSpec(
            (tm, tn), lambda m, n, s=start, c=ntl: (m, jnp.clip(n - s, 0, c - 1))))
        out_shape.append(jax.ShapeDtypeStruct((T, ntl * tn), dt))
        start += ntl

    kern = functools.partial(_proj_in_kernel, segs=seg_tiles, n_prompt_tiles=n_prompt_tiles,
                             has_norm=head_norm is not None, has_rope=rope_tabs is not None)
    return pl.pallas_call(
        kern,
        grid=(T // tm, n_tiles),
        in_specs=in_specs,
        out_specs=out_specs,
        out_shape=out_shape,
        scratch_shapes=[pltpu.VMEM((tm, D_MODEL), BF16)],
        compiler_params=_params("parallel", "arbitrary"),
        name="proj_in",
    )(*args)


def _proj_out_kernel(*refs, with_skip):
    if with_skip:
        o_ref, xc_ref, skip_ref, z_ref, x_ref, mod_ref, w_ref, out_ref = refs
        a = o_ref[...] + skip_ref[...] * xc_ref[...]
    else:
        o_ref, z_ref, x_ref, mod_ref, w_ref, out_ref = refs
        a = o_ref[...]
    a = (a * _silu(z_ref[...])).astype(BF16)
    gate = mod_ref[:, 2 * D_MODEL:3 * D_MODEL]
    out_ref[...] = x_ref[...] + gate * _dot(a, w_ref[...])


def proj_out(o, z, x, mod4, layer, w_out, n_prompt, sample_seq, xc=None, skip=None):
    T, K = o.shape
    tm = TOKEN_TILE
    n_prompt_tiles = n_prompt // tm
    tiles_per_seq = sample_seq // tm

    def mod_row(m):
        return jnp.where(m < n_prompt_tiles, 0, 1 + (m - n_prompt_tiles) // tiles_per_seq)

    row = lambda m: (m, 0)
    in_specs = [pl.BlockSpec((tm, K), row)]
    args = [o]
    if xc is not None:
        in_specs += [pl.BlockSpec((tm, K), row), pl.BlockSpec((1, K), lambda m: (0, 0))]
        args += [xc, skip.reshape(1, K)]
    in_specs += [
        pl.BlockSpec((tm, K), row),
        pl.BlockSpec((tm, D_MODEL), row),
        pl.BlockSpec((None, None, 1, 3 * D_MODEL), lambda m: (layer, mod_row(m), 0, 0)),
        pl.BlockSpec((K, D_MODEL), lambda m: (0, 0)),
    ]
    args += [z, x, mod4, w_out]
    return pl.pallas_call(
        functools.partial(_proj_out_kernel, with_skip=xc is not None),
        grid=(T // tm,),
        in_specs=in_specs,
        out_specs=pl.BlockSpec((tm, D_MODEL), row),
        out_shape=jax.ShapeDtypeStruct((T, D_MODEL), F32),
        compiler_params=_params("parallel"),
        name="proj_out",
    )(*args)


def _final_norm_kernel(x_ref, g_ref, o_ref):
    x = x_ref[...]
    o_ref[...] = x * lax.rsqrt(jnp.mean(x * x, axis=-1, keepdims=True) + NORM_EPS) * g_ref[...]


def final_norm(x, g, row_start, rows):
    tm = TOKEN_TILE
    off = row_start // tm
    return pl.pallas_call(
        _final_norm_kernel,
        grid=(rows // tm,),
        in_specs=[pl.BlockSpec((tm, D_MODEL), lambda m: (m + off, 0)),
                  pl.BlockSpec((1, D_MODEL), lambda m: (0, 0))],
        out_specs=pl.BlockSpec((tm, D_MODEL), lambda m: (m, 0)),
        out_shape=jax.ShapeDtypeStruct((rows, D_MODEL), F32),
        compiler_params=_params("parallel"),
        name="final_norm",
    )(x, g.reshape(1, D_MODEL))


def _softmax_parts(q, k):
    s = _dot_nt(q, k)
    e = jnp.exp(s - jnp.max(s, axis=-1, keepdims=True))
    return e, jnp.sum(e, axis=-1, keepdims=True)


def _gqa_attn_kernel(q_ref, k_ref, v_ref, o_ref):
    k = k_ref[...].astype(BF16)
    v = v_ref[...].astype(BF16)
    for kv in range(GQA_KV_HEADS):
        kh = k[:, kv * HEAD_DIM:(kv + 1) * HEAD_DIM]
        vh = v[:, kv * HEAD_DIM:(kv + 1) * HEAD_DIM]
        for g in range(GQA_GROUP):
            lo = (kv * GQA_GROUP + g) * HEAD_DIM
            e, l = _softmax_parts(q_ref[:, lo:lo + HEAD_DIM], kh)
            o_ref[:, lo:lo + HEAD_DIM] = _dot(e.astype(BF16), vh) / l


def _diff_attn_kernel(q_ref, k_ref, v_ref, lam_ref, g_ref, o_ref, *, lam_init):
    lv = lam_ref[...]
    s1 = jnp.sum(lv[0:1] * lv[1:2], axis=-1, keepdims=True)
    s2 = jnp.sum(lv[2:3] * lv[3:4], axis=-1, keepdims=True)
    lam = jnp.exp(s1) - jnp.exp(s2) + lam_init
    k = k_ref[...].astype(BF16)
    v = v_ref[...].astype(BF16)
    for h in range(DIFF_HEADS):
        vh = v[:, h * DIFF_V_DIM:(h + 1) * DIFF_V_DIM]
        lo = h * 2 * HEAD_DIM
        e0, l0 = _softmax_parts(q_ref[:, lo:lo + HEAD_DIM], k[:, lo:lo + HEAD_DIM])
        e1, l1 = _softmax_parts(q_ref[:, lo + HEAD_DIM:lo + 2 * HEAD_DIM], k[:, lo + HEAD_DIM:lo + 2 * HEAD_DIM])
        o = _dot(e0.astype(BF16), vh) / l0 - lam * (_dot(e1.astype(BF16), vh) / l1)
        o = o * lax.rsqrt(jnp.mean(o * o, axis=-1, keepdims=True) + NORM_EPS) * g_ref[...]
        o_ref[:, h * DIFF_V_DIM:(h + 1) * DIFF_V_DIM] = o * (1.0 - lam_init)


def attention(kernel, q, k, v, batch, sq, sk, q_row0, extra=(), out=None):
    T, Wq = q.shape
    tq = ATTN_Q_TILE
    nq = sq // tq
    off = q_row0 // tq
    q_map = lambda b, i: (b * nq + i + off, 0)
    in_specs = [
        pl.BlockSpec((tq, Wq), q_map),
        pl.BlockSpec((sk, k.shape[1]), lambda b, i: (b, 0)),
        pl.BlockSpec((sk, v.shape[1]), lambda b, i: (b, 0)),
    ]
    args = [q, k, v, *extra]
    for a in extra:
        in_specs.append(pl.BlockSpec(a.shape, lambda b, i: (0, 0)))
    aliases = {}
    if out is not None:
        in_specs.append(pl.BlockSpec(memory_space=pl.ANY))
        aliases = {len(args): 0}
        args.append(out)
        kernel = functools.partial(_drop_alias_ref, kernel, len(args) - 1)
    return pl.pallas_call(
        kernel,
        grid=(batch, nq),
        in_specs=in_specs,
        out_specs=pl.BlockSpec((tq, D_MODEL), q_map),
        out_shape=jax.ShapeDtypeStruct((T, D_MODEL), F32),
        input_output_aliases=aliases,
        compiler_params=_params("parallel", "parallel"),
        name="attention",
    )(*args)


def _drop_alias_ref(kernel, pos, *refs):
    kernel(*refs[:pos], *refs[pos + 1:])


def _log_sigmoid(x):
    return jnp.minimum(x, 0.0) - jnp.log1p(jnp.exp(-jnp.abs(x)))


def _mlstm_pre_kernel(cur_ref, prev_ref, next_ref, cw_ref, cb_ref, bq_ref, bk_ref, bkt_ref, bv_ref,
                      wg_ref, bg_ref, xc_ref, q_ref, kt_ref, v_ref, gcol_ref, grow_ref,
                      *, n_prompt_tiles, tiles_per_seq):
    L = MLSTM_CHUNK
    t = pl.program_id(0)
    st = jnp.maximum(t - n_prompt_tiles, 0) % tiles_per_seq
    first = (t < n_prompt_tiles) | (st == 0)
    last = (t < n_prompt_tiles) | (st == tiles_per_seq - 1)
    cur = cur_ref[...]
    prev = jnp.where(first, 0.0, prev_ref[...])
    nxt = jnp.where(last, 0.0, next_ref[...])
    ext = jnp.concatenate([prev, cur, nxt], axis=0)
    n_ext = L + 2 * SUBLANES
    pad_l = MLSTM_CONV_K // 2
    acc = cb_ref[...] + cw_ref[pad_l:pad_l + 1, :] * cur
    for j in range(MLSTM_CONV_K):
        if j == pad_l:
            continue
        sh = (pad_l - j) % n_ext
        acc = acc + cw_ref[j:j + 1, :] * pltpu.roll(ext, sh, 0)[SUBLANES:SUBLANES + L]
    xc = _silu(acc)
    xc_ref[...] = xc

    g = jnp.zeros((L, LANES), F32)
    for c in range(MLSTM_INNER // LANES):
        sl = slice(c * LANES, (c + 1) * LANES)
        xcb = xc[:, sl].astype(BF16)
        xmb = cur[:, sl].astype(BF16)
        qc = _dot(xcb, bq_ref[c]).astype(BF16)
        kc = _dot(xcb, bk_ref[c]).astype(BF16)
        vc = _dot(xmb, bv_ref[c]).astype(BF16)
        q_ref[:, sl] = qc
        v_ref[:, sl] = vc
        kt_ref[sl, :] = _dot_nt(bkt_ref[c], xcb).astype(BF16)
        g = g + _dot(qc, wg_ref[0, sl, :]) + _dot(kc, wg_ref[1, sl, :]) + _dot(vc, wg_ref[2, sl, :])
    g = g + bg_ref[...]

    ls = _log_sigmoid(g)
    row = lax.broadcasted_iota(jnp.int32, (L, LANES), 0)
    pre, suf = ls, ls
    s = 1
    while s < L:
        pre = pre + jnp.where(row >= s, pltpu.roll(pre, s, 0), 0.0)
        suf = suf + jnp.where(row < L - s, pltpu.roll(suf, L - s, 0), 0.0)
        s *= 2
    lane = lax.broadcasted_iota(jnp.int32, (L, LANES), 1)
    H = MLSTM_HEADS
    out = jnp.where((lane >= H) & (lane < 2 * H), pre, jnp.where((lane >= 3 * H) & (lane < 4 * H), suf, g))
    gcol_ref[...] = out[:, 0:GATE_W]
    grow_ref[...] = out.T[0:GATE_W, :]


def mlstm_pre(xm, conv_w, conv_b, bq, bk, bkt, bv, wg, bg, n_prompt, sample_seq):
    T = xm.shape[0]
    L = MLSTM_CHUNK
    n_tiles = T // L
    r8 = L // SUBLANES
    n_blk8 = T // SUBLANES
    C = MLSTM_INNER
    kern = functools.partial(_mlstm_pre_kernel, n_prompt_tiles=n_prompt // L, tiles_per_seq=sample_seq // L)
    const3 = lambda t: (0, 0, 0)
    return pl.pallas_call(
        kern,
        grid=(n_tiles,),
        in_specs=[
            pl.BlockSpec((L, C), lambda t: (t, 0)),
            pl.BlockSpec((SUBLANES, C), lambda t: (jnp.maximum(t * r8 - 1, 0), 0)),
            pl.BlockSpec((SUBLANES, C), lambda t: (jnp.minimum((t + 1) * r8, n_blk8 - 1), 0)),
            pl.BlockSpec((MLSTM_CONV_K, C), lambda t: (0, 0)),
            pl.BlockSpec((1, C), lambda t: (0, 0)),
            pl.BlockSpec(bq.shape, const3),
            pl.BlockSpec(bk.shape, const3),
            pl.BlockSpec(bkt.shape, const3),
            pl.BlockSpec(bv.shape, const3),
            pl.BlockSpec(wg.shape, const3),
            pl.BlockSpec((1, LANES), lambda t: (0, 0)),
        ],
        out_specs=[
            pl.BlockSpec((L, C), lambda t: (t, 0)),
            pl.BlockSpec((L, C), lambda t: (t, 0)),
            pl.BlockSpec((None, C, L), lambda t: (t, 0, 0)),
            pl.BlockSpec((L, C), lambda t: (t, 0)),
            pl.BlockSpec((L, GATE_W), lambda t: (t, 0)),
            pl.BlockSpec((None, GATE_W, L), lambda t: (t, 0, 0)),
        ],
        out_shape=[
            jax.ShapeDtypeStruct((T, C), F32),
            jax.ShapeDtypeStruct((T, C), BF16),
            jax.ShapeDtypeStruct((n_tiles, C, L), BF16),
            jax.ShapeDtypeStruct((T, C), BF16),
            jax.ShapeDtypeStruct((T, GATE_W), F32),
            jax.ShapeDtypeStruct((n_tiles, GATE_W, L), F32),
        ],
        compiler_params=_params("parallel"),
        name="mlstm_pre",
    )(xm, xm, xm, conv_w, conv_b.reshape(1, C), bq, bk, bkt, bv, wg, bg)


def _gate_cols(gcol, h):
    lane = lax.broadcasted_iota(jnp.int32, gcol.shape, 1)
    pick = lambda idx: jnp.sum(jnp.where(lane == idx, gcol, 0.0), axis=-1, keepdims=True)
    return pick(MLSTM_HEADS + h), pick(3 * MLSTM_HEADS + h)


def _chunk_dir(qk, q, kt, v, b_col, b_row, i_row, tot, causal_mask, m_prev, C_prev, n_prev):
    ar = i_row - b_row
    d = jnp.where(causal_mask, b_col + ar, -jnp.inf)
    inter = b_col + m_prev
    mt = jnp.maximum(jnp.max(d, axis=-1, keepdims=True), inter)
    s = qk * jnp.exp(d - mt)
    num = _dot(s.astype(BF16), v)
    den = jnp.sum(s, axis=-1, keepdims=True)
    if C_prev is not None:
        w_inter = jnp.exp(inter - mt)
        num = num + w_inter * _dot(q, C_prev.astype(BF16))
        den = den + w_inter * jnp.sum(q.astype(F32) * n_prev, axis=-1, keepdims=True)
    h = num / jnp.maximum(jnp.abs(den), jnp.exp(-mt))

    g = tot + ar
    m_new = jnp.maximum(tot + m_prev, jnp.max(g, axis=-1, keepdims=True))
    w = jnp.exp(g - m_new)
    C_new = _dot((kt.astype(F32) * w).astype(BF16), v)
    w_hi = w.astype(BF16).astype(F32)
    rid = lax.broadcasted_iota(jnp.int32, (SUBLANES, w.shape[-1]), 0)
    w8 = jnp.where(rid == 0, w_hi, jnp.where(rid == 1, w - w_hi, 0.0)).astype(BF16)
    n8 = _dot_nt(w8, kt)
    n_new = n8[0:1] + n8[1:2]
    if C_prev is not None:
        decay = jnp.exp(tot + m_prev - m_new)
        C_new = decay * C_prev + C_new
        n_new = decay * n_prev + n_new
    return h, C_new, n_new, m_new


def _head_layernorm(h, g):
    mu = jnp.mean(h, axis=-1, keepdims=True)
    hc = h - mu
    var = jnp.mean(hc * hc, axis=-1, keepdims=True)
    return hc * lax.rsqrt(var + NORM_EPS) * g


def _chunk_masks(L):
    t_idx = lax.broadcasted_iota(jnp.int32, (L, L), 0)
    s_idx = lax.broadcasted_iota(jnp.int32, (L, L), 1)
    return s_idx <= t_idx, s_idx >= t_idx


def _mlstm_prompt_kernel(q_ref, kt_ref, v_ref, gcol_ref, grow_ref, mhg_ref,
                         hn_ref, C_ref, n_ref, m_ref):
    L = MLSTM_CHUNK
    H = MLSTM_HEADS
    h = pl.program_id(1)
    q, kt, v = q_ref[...], kt_ref[...], v_ref[...]
    qk = _dot(q, kt)
    fmask, bmask = _chunk_masks(L)
    bf_col, bb_col = _gate_cols(gcol_ref[...], h)
    zero = jnp.zeros((1, 1), F32)
    row = lambda idx: grow_ref[pl.ds(idx, 1), :]
    bf_row, bb_row = row(H + h), row(3 * H + h)
    hf, Cf, nf, mf = _chunk_dir(qk, q, kt, v, bf_col, bf_row, row(h), bf_row[:, L - 1:L], fmask, zero, None, None)
    hb, Cb, nb, mb = _chunk_dir(qk, q, kt, v, bb_col, bb_row, row(2 * H + h), bb_row[:, 0:1], bmask, zero, None, None)
    hn_ref[...] = _head_layernorm(hf + hb, mhg_ref[...])
    C_ref[0] = Cf
    C_ref[1] = Cb
    n_ref[0] = nf
    n_ref[1] = nb
    lane = lax.broadcasted_iota(jnp.int32, (1, LANES), 1)
    m_ref[...] = jnp.where(lane == 0, mf, jnp.where(lane == 1, mb, 0.0))


def mlstm_prompt(q, kt, v, gcol, grow, mh_g, batch):
    L, H, DH = MLSTM_CHUNK, MLSTM_HEADS, MLSTM_HEAD_DIM
    return pl.pallas_call(
        _mlstm_prompt_kernel,
        grid=(batch, H),
        in_specs=[
            pl.BlockSpec((L, DH), lambda b, h: (b, h)),
            pl.BlockSpec((None, DH, L), lambda b, h: (b, h, 0)),
            pl.BlockSpec((L, DH), lambda b, h: (b, h)),
            pl.BlockSpec((L, GATE_W), lambda b, h: (b, 0)),
            pl.BlockSpec((None, GATE_W, L), lambda b, h: (b, 0, 0)),
            pl.BlockSpec((1, DH), lambda b, h: (0, h)),
        ],
        out_specs=[
            pl.BlockSpec((L, DH), lambda b, h: (b, h)),
            pl.BlockSpec((None, None, 2, None, DH, DH), lambda b, h: (b, 0, 0, h, 0, 0)),
            pl.BlockSpec((None, 2, None, 1, DH), lambda b, h: (b, 0, h, 0, 0)),
            pl.BlockSpec((None, None, 1, LANES), lambda b, h: (b, h, 0, 0)),
        ],
        out_shape=[
            jax.ShapeDtypeStruct((q.shape[0], MLSTM_INNER), F32),
            jax.ShapeDtypeStruct((batch, 1, 2, H, DH, DH), F32),
            jax.ShapeDtypeStruct((batch, 2, H, 1, DH), F32),
            jax.ShapeDtypeStruct((batch, H, 1, LANES), F32),
        ],
        compiler_params=_params("parallel", "parallel"),
        name="mlstm_prompt",
    )(q, kt, v, gcol, grow, mh_g.reshape(1, MLSTM_INNER))


def _mlstm_sample_kernel(q_ref, kt_ref, v_ref, gcol_ref, grow_ref, mhg_ref, C0_ref, n0_ref, m0_ref, hn_in_ref,
                         hn_ref, hf_ref, hb_ref, C_ref, n_ref, m_ref, *, n_chunks):
    L = MLSTM_CHUNK
    H = MLSTM_HEADS
    h = pl.program_id(1)
    fmask, bmask = _chunk_masks(L)
    C_ref[...] = C0_ref[...]
    n_ref[...] = n0_ref[...]
    m_ref[...] = m0_ref[...]

    def one(c, direction):
        rows = pl.ds(pl.multiple_of(c * L, L), L)
        q, kt, v = q_ref[rows, :], kt_ref[c], v_ref[rows, :]
        qk = _dot(q, kt)
        gcol = gcol_ref[rows, :]
        b_col = _gate_cols(gcol, h)[direction]
        grow = lambda idx: grow_ref[c, pl.ds(idx, 1), :]
        b_row = grow((2 * direction + 1) * H + h)
        i_row = grow(2 * direction * H + h)
        tot = b_row[:, L - 1:L] if direction == 0 else b_row[:, 0:1]
        mask = fmask if direction == 0 else bmask
        m_prev = m_ref[direction][:, 0:1]
        hh, C_new, n_new, m_new = _chunk_dir(qk, q, kt, v, b_col, b_row, i_row, tot, mask, m_prev,
                                             C_ref[direction], n_ref[direction])
        C_ref[direction] = C_new
        n_ref[direction] = n_new
        m_ref[direction] = jnp.broadcast_to(m_new, (1, LANES))
        (hf_ref if direction == 0 else hb_ref)[rows, :] = hh

    def body(i, carry):
        one(i, 0)
        one(n_chunks - 1 - i, 1)
        return carry

    lax.fori_loop(0, n_chunks, body, 0)

    def norm(c, carry):
        rows = pl.ds(pl.multiple_of(c * L, L), L)
        hn_ref[rows, :] = _head_layernorm(hf_ref[rows, :] + hb_ref[rows, :], mhg_ref[...])
        return carry

    lax.fori_loop(0, n_chunks, norm, 0)


def mlstm_sample(q, kt, v, gcol, grow, mh_g, C0, n0, m0, hn, n_prompt, batch, seq):
    L, H, DH = MLSTM_CHUNK, MLSTM_HEADS, MLSTM_HEAD_DIM
    nc = seq // L
    row_off = n_prompt // seq
    tile_off = n_prompt // L // nc
    return pl.pallas_call(
        functools.partial(_mlstm_sample_kernel, n_chunks=nc),
        grid=(batch, H),
        in_specs=[
            pl.BlockSpec((seq, DH), lambda b, h: (b + row_off, h)),
            pl.BlockSpec((nc, DH, L), lambda b, h: (b + tile_off, h, 0)),
            pl.BlockSpec((seq, DH), lambda b, h: (b + row_off, h)),
            pl.BlockSpec((seq, GATE_W), lambda b, h: (b + row_off, 0)),
            pl.BlockSpec((nc, GATE_W, L), lambda b, h: (b + tile_off, 0, 0)),
            pl.BlockSpec((1, DH), lambda b, h: (0, h)),
            pl.BlockSpec((None, 2, None, DH, DH), lambda b, h: (b, 0, h, 0, 0)),
            pl.BlockSpec((None, 2, None, 1, DH), lambda b, h: (b, 0, h, 0, 0)),
            pl.BlockSpec((None, None, 2, 1, LANES), lambda b, h: (b, h, 0, 0, 0)),
            pl.BlockSpec(memory_space=pl.ANY),
        ],
        out_specs=pl.BlockSpec((seq, DH), lambda b, h: (b + row_off, h)),
        out_shape=jax.ShapeDtypeStruct(hn.shape, F32),
        input_output_aliases={9: 0},
        scratch_shapes=[
            pltpu.VMEM((seq, DH), F32),
            pltpu.VMEM((seq, DH), F32),
            pltpu.VMEM((2, DH, DH), F32),
            pltpu.VMEM((2, 1, DH), F32),
            pltpu.VMEM((2, 1, LANES), F32),
        ],
        compiler_params=_params("parallel", "parallel"),
        name="mlstm_sample",
    )(q, kt, v, gcol, grow, mh_g.reshape(1, MLSTM_INNER), C0, n0, m0, hn)


def _w_tiles(w):
    D, N = w.shape
    return w.astype(BF16).reshape(D, N // PROJ_N_TILE, PROJ_N_TILE).transpose(1, 0, 2)


def _rope_tables(seq):
    t = jnp.arange(seq)
    row = (t // GRID_W).astype(F32)
    col = (t % GRID_W).astype(F32)
    n_freq = HEAD_DIM // 4
    inv = ROPE_THETA ** (-jnp.arange(n_freq, dtype=F32) / n_freq)
    ang = jnp.concatenate([row[:, None] * inv, col[:, None] * inv], axis=-1)
    cos, sin = jnp.cos(ang), jnp.sin(ang)
    reps = PROJ_N_TILE // HEAD_DIM
    return (jnp.tile(jnp.concatenate([cos, cos], axis=-1), (1, reps)),
            jnp.tile(jnp.concatenate([-sin, sin], axis=-1), (1, reps)))


def _block_diag(w):
    per = LANES // MLSTM_QKV_BLOCK
    w4 = w.reshape(-1, per, MLSTM_QKV_BLOCK, MLSTM_QKV_BLOCK)
    eye = jnp.eye(per, dtype=w.dtype)
    return jnp.einsum('kacd,ab->kacbd', w4, eye).reshape(-1, LANES, LANES)


def _diff_lambda_init(layer_idx):
    return 0.8 - 0.6 * math.exp(-0.3 * layer_idx)


def kernel(x_prompt, x_sample, cache_gqa_k, cache_gqa_v, cache_diff_k, cache_diff_v, state_mlstm_C, state_mlstm_n, state_mlstm_m, c, c_ctx, norm_g, w_mod, b_mod, gqa_w_in, gqa_q_norm_g, gqa_k_norm_g, gqa_w_out, diff_w_in, diff_lambda_q1, diff_lambda_k1, diff_lambda_q2, diff_lambda_k2, diff_subln_g, diff_w_out, mlstm_w_in, mlstm_conv_w, mlstm_conv_b, mlstm_w_q, mlstm_w_k, mlstm_w_v, mlstm_w_gate_f, mlstm_b_gate_f, mlstm_w_gate_b, mlstm_b_gate_b, mlstm_mh_norm_g, mlstm_skip, mlstm_w_out, final_norm_g):
    Bp, Sp, D = x_prompt.shape
    Bs, Ss, _ = x_sample.shape
    Tp, Ts = Bp * Sp, Bs * Ss
    assert Sp == MLSTM_CHUNK and Ss % TOKEN_TILE == 0 and Tp % TOKEN_TILE == 0
    H, DH = MLSTM_HEADS, MLSTM_HEAD_DIM

    x = jnp.concatenate([x_prompt.reshape(Tp, D), x_sample.reshape(Ts, D)], axis=0)
    cond = jnp.zeros((SUBLANES, D), F32).at[0].set(c_ctx).at[1:1 + Bs].set(c)
    mod4 = adaln_all(cond, w_mod, b_mod).reshape(DEPTH, SUBLANES, 1, 3 * D)
    rope_tabs = _rope_tables(Ss)
    gi = jnp.arange(PROJ_N_TILE) // HEAD_DIM
    gmat = (gi[:, None] == gi[None, :]).astype(BF16)
    tile_g = lambda g: jnp.tile(g, PROJ_N_TILE // HEAD_DIM).reshape(1, PROJ_N_TILE)

    def with_cache(new, cache):
        w = new.shape[1]
        both = jnp.concatenate([new[Tp:].reshape(Bs, Ss, w), cache.reshape(Bs, -1, w)], axis=1)
        return both.reshape(-1, w).astype(BF16)

    gqa_k_list, gqa_v_list, diff_k_list, diff_v_list = [], [], [], []
    mC_list, mn_list, mm_list = [], [], []
    for i in range(DEPTH):
        kind, j = i % N_MIXERS, i // N_MIXERS
        common = dict(n_prompt=Tp, sample_seq=Ss)
        if kind == 0:
            segs = ((GQA_Q_W, BF16, "q_norm"), (GQA_KV_W, F32, "k_norm"), (GQA_KV_W, F32, "plain"), (GQA_Q_W, F32, "plain"))
            q, k, v, z = proj_in(x, norm_g[i], mod4, i, _w_tiles(gqa_w_in[j]), segs, rope_tabs=rope_tabs,
                                 head_norm=(gmat, tile_g(gqa_q_norm_g[j]), tile_g(gqa_k_norm_g[j])), **common)
            o = attention(_gqa_attn_kernel, q, k, v, Bp, Sp, Sp, 0)
            k_all = with_cache(k, cache_gqa_k[:, j])
            v_all = with_cache(v, cache_gqa_v[:, j])
            o = attention(_gqa_attn_kernel, q, k_all, v_all, Bs, Ss, k_all.shape[0] // Bs, Tp, out=o)
            x = proj_out(o, z, x, mod4, i, gqa_w_out[j].astype(BF16), **common)
            gqa_k_list.append(k[:Tp].reshape(Bp, Sp, GQA_KV_HEADS, HEAD_DIM))
            gqa_v_list.append(v[:Tp].reshape(Bp, Sp, GQA_KV_HEADS, HEAD_DIM))
        elif kind == 1:
            lam_init = _diff_lambda_init(i)
            segs = ((DIFF_QK_W, BF16, "q"), (DIFF_QK_W, F32, "k"), (DIFF_BRANCH, F32, "plain"), (DIFF_BRANCH, F32, "plain"))
            q, k, v, z = proj_in(x, norm_g[i], mod4, i, _w_tiles(diff_w_in[j]), segs, rope_tabs=rope_tabs, **common)
            lam_vecs = jnp.stack([diff_lambda_q1[j], diff_lambda_k1[j], diff_lambda_q2[j], diff_lambda_k2[j]])
            extra = (lam_vecs, diff_subln_g[j].reshape(1, DIFF_V_DIM))
            kern = functools.partial(_diff_attn_kernel, lam_init=lam_init)
            o = attention(kern, q, k, v, Bp, Sp, Sp, 0, extra)
            k_all = with_cache(k, cache_diff_k[:, j])
            v_all = with_cache(v, cache_diff_v[:, j])
            o = attention(kern, q, k_all, v_all, Bs, Ss, k_all.shape[0] // Bs, Tp, extra, out=o)
            x = proj_out(o, z, x, mod4, i, diff_w_out[j].astype(BF16), **common)
            diff_k_list.append(k[:Tp].reshape(Bp, Sp, DIFF_HEADS, 2, HEAD_DIM))
            diff_v_list.append(v[:Tp].reshape(Bp, Sp, DIFF_HEADS, DIFF_V_DIM))
        else:
            segs = ((MLSTM_INNER, F32, "plain"), (MLSTM_INNER, F32, "plain"))
            xm, z = proj_in(x, norm_g[i], mod4, i, _w_tiles(mlstm_w_in[j]), segs, **common)
            bq = _block_diag(mlstm_w_q[j]).astype(BF16)
            bk = _block_diag(mlstm_w_k[j])
            bkt = (bk * (DH ** -0.5)).transpose(0, 2, 1).astype(BF16)
            bv = _block_diag(mlstm_w_v[j]).astype(BF16)
            wg = jnp.concatenate([mlstm_w_gate_f[j][:, :H], mlstm_w_gate_f[j][:, H:],
                                  mlstm_w_gate_b[j][:, :H], mlstm_w_gate_b[j][:, H:]], axis=-1)
            wg = jnp.pad(wg, ((0, 0), (0, LANES - GATE_W))).astype(BF16).reshape(3, MLSTM_INNER, LANES)
            bg = jnp.pad(jnp.concatenate([mlstm_b_gate_f[j], mlstm_b_gate_b[j]]), (0, LANES - GATE_W)).reshape(1, LANES)
            xc, q, kt, v, gcol, grow = mlstm_pre(xm, mlstm_conv_w[j], mlstm_conv_b[j], bq, bk.astype(BF16), bkt, bv,
                                                 wg, bg, **common)
            hn, Cp, np_, mp_ = mlstm_prompt(q, kt, v, gcol, grow, mlstm_mh_norm_g[j], Bp)
            C0 = state_mlstm_C[:, j]
            n0 = state_mlstm_n[:, j].reshape(Bs, 2, H, 1, DH)
            m0 = jnp.broadcast_to(state_mlstm_m[:, j].transpose(0, 2, 1)[..., None, None], (Bs, H, 2, 1, LANES))
            hn = mlstm_sample(q, kt, v, gcol, grow, mlstm_mh_norm_g[j], C0, n0, m0, hn, Tp, Bs, Ss)
            x = proj_out(hn, z, x, mod4, i, mlstm_w_out[j].astype(BF16), xc=xc, skip=mlstm_skip[j], **common)
            mC_list.append(Cp)
            mn_list.append(np_.reshape(Bp, 1, 2, H, DH))
            mm_list.append(mp_[:, :, 0, 0:2].transpose(0, 2, 1)[:, None])

    cat1 = lambda parts: parts[0] if len(parts) == 1 else jnp.concatenate(parts, axis=1)
    y_prompt = final_norm(x, final_norm_g, 0, Tp).reshape(Bp, Sp, D)
    y_sample = final_norm(x, final_norm_g, Tp, Ts).reshape(Bs, Ss, D)
    return (y_prompt, y_sample,
            jnp.stack(gqa_k_list, axis=1), jnp.stack(gqa_v_list, axis=1),
            jnp.stack(diff_k_list, axis=1), jnp.stack(diff_v_list, axis=1),
            cat1(mC_list), cat1(mn_list), cat1(mm_list))
```

```python
import functools
import math

import jax
import jax.numpy as jnp
from jax import lax
from jax.experimental import pallas as pl
from jax.experimental.pallas import tpu as pltpu

D_MODEL = 1024
DEPTH = 4
GRID_W = 64
N_MIXERS = 3
HEAD_DIM = 64
ROPE_THETA = 10000.0
NORM_EPS = 1e-6

GQA_HEADS = 16
GQA_KV_HEADS = 4
GQA_GROUP = GQA_HEADS // GQA_KV_HEADS
GQA_Q_W = GQA_HEADS * HEAD_DIM
GQA_KV_W = GQA_KV_HEADS * HEAD_DIM

DIFF_HEADS = 8
DIFF_QK_W = DIFF_HEADS * 2 * HEAD_DIM
DIFF_V_DIM = 2 * HEAD_DIM
DIFF_BRANCH = DIFF_HEADS * DIFF_V_DIM

MLSTM_INNER = 2 * D_MODEL
MLSTM_HEADS = 4
MLSTM_HEAD_DIM = MLSTM_INNER // MLSTM_HEADS
MLSTM_QKV_BLOCK = 4
MLSTM_CONV_K = 4

LANES = 128
SUBLANES = 8
VMEM_LIMIT_BYTES = 56 * 1024 * 1024

TOKEN_TILE = 512
PROJ_N_TILE = 256
ATTN_Q_TILE = 256
MLSTM_CHUNK = 256
GATE_W = 4 * MLSTM_HEADS

F32 = jnp.float32
BF16 = jnp.bfloat16


def _params(*semantics):
    return pltpu.CompilerParams(dimension_semantics=semantics, vmem_limit_bytes=VMEM_LIMIT_BYTES)


def _silu(x):
    return x * (1.0 / (1.0 + jnp.exp(-x)))


def _dot(a, b):
    return jnp.dot(a, b, preferred_element_type=F32)


def _dot_nt(a, b):
    return lax.dot_general(a, b, (((1,), (1,)), ((), ())), preferred_element_type=F32)


def _mod_kernel(cond_ref, w_ref, b_ref, o_ref):
    a = _silu(cond_ref[...]).astype(BF16)
    o_ref[...] = _dot(a, w_ref[...].astype(BF16)) + b_ref[...]


def adaln_all(cond, w_mod, b_mod):
    tn = 1024
    return pl.pallas_call(
        _mod_kernel,
        grid=(DEPTH, 3 * D_MODEL // tn),
        in_specs=[
            pl.BlockSpec((SUBLANES, D_MODEL), lambda i, n: (0, 0)),
            pl.BlockSpec((None, D_MODEL, tn), lambda i, n: (i, 0, n)),
            pl.BlockSpec((None, 1, tn), lambda i, n: (i, 0, n)),
        ],
        out_specs=pl.BlockSpec((None, SUBLANES, tn), lambda i, n: (i, 0, n)),
        out_shape=jax.ShapeDtypeStruct((DEPTH, SUBLANES, 3 * D_MODEL), F32),
        compiler_params=_params("parallel", "parallel"),
        name="adaln_mod",
    )(cond, w_mod, b_mod.reshape(DEPTH, 1, 3 * D_MODEL))


def _swap_half_heads(y):
    w = y.shape[-1]
    lane = lax.broadcasted_iota(jnp.int32, y.shape, 1)
    upper = (lane & (HEAD_DIM // 2)) != 0
    return jnp.where(upper, pltpu.roll(y, HEAD_DIM // 2, 1), pltpu.roll(y, w - HEAD_DIM // 2, 1))


def _proj_in_kernel(*refs, segs, has_norm, has_rope):
    it = iter(refs)
    x_ref, g_ref, mod_ref, w_ref = next(it), next(it), next(it), next(it)
    if has_rope:
        cos_ref, sin_ref = next(it), next(it)
    if has_norm:
        gmat_ref, qg_ref, kg_ref = next(it), next(it), next(it)
    out_refs = [next(it) for _ in segs]
    cw = PROJ_N_TILE

    x = x_ref[...]
    y = x * lax.rsqrt(jnp.mean(x * x, axis=-1, keepdims=True) + NORM_EPS) * g_ref[...]
    shift = mod_ref[:, 0:D_MODEL]
    scale = mod_ref[:, D_MODEL:2 * D_MODEL]
    h = (y * (1.0 + scale) + shift).astype(BF16)

    col = 0
    for (width, dtype, mode), o_ref in zip(segs, out_refs):
        for c in range(width // cw):
            acc = _dot(h, w_ref[:, col:col + cw])
            col += cw
            if mode in ("q_norm", "k_norm"):
                ss = _dot((acc * acc).astype(BF16), gmat_ref[...])
                gvec = qg_ref[...] if mode == "q_norm" else kg_ref[...]
                acc = acc * lax.rsqrt(ss * (1.0 / HEAD_DIM) + NORM_EPS) * gvec
            if has_rope and mode != "plain":
                acc = acc * cos_ref[...] + _swap_half_heads(acc) * sin_ref[...]
            if mode in ("q_norm", "q"):
                acc = acc * (HEAD_DIM ** -0.5)
            o_ref[:, c * cw:(c + 1) * cw] = acc.astype(dtype)


def proj_in(x, norm_g, mod4, layer, w, segs, n_prompt, sample_seq, rope_tabs=None, head_norm=None):
    T = x.shape[0]
    tm, cw = TOKEN_TILE, PROJ_N_TILE
    n_prompt_tiles = n_prompt // tm
    tiles_per_seq = sample_seq // tm
    N = w.shape[1]
    assert N == sum(s[0] for s in segs) and all(s[0] % cw == 0 for s in segs)

    def mod_row(m):
        return jnp.where(m < n_prompt_tiles, 0, 1 + (m - n_prompt_tiles) // tiles_per_seq)

    in_specs = [
        pl.BlockSpec((tm, D_MODEL), lambda m: (m, 0)),
        pl.BlockSpec((1, D_MODEL), lambda m: (0, 0)),
        pl.BlockSpec((None, None, 1, 3 * D_MODEL), lambda m: (layer, mod_row(m), 0, 0)),
        pl.BlockSpec((D_MODEL, N), lambda m: (0, 0)),
    ]
    args = [x, norm_g.reshape(1, D_MODEL), mod4, w]
    if rope_tabs is not None:
        def pos_block(m):
            return (jnp.where(m < n_prompt_tiles, tiles_per_seq, (m - n_prompt_tiles) % tiles_per_seq), 0)
        in_specs += [pl.BlockSpec((tm, cw), pos_block), pl.BlockSpec((tm, cw), pos_block)]
        args += list(rope_tabs)
    if head_norm is not None:
        in_specs += [pl.BlockSpec((cw, cw), lambda m: (0, 0)),
                     pl.BlockSpec((1, cw), lambda m: (0, 0)),
                     pl.BlockSpec((1, cw), lambda m: (0, 0))]
        args += list(head_norm)

    kern = functools.partial(_proj_in_kernel, segs=segs,
                             has_norm=head_norm is not None, has_rope=rope_tabs is not None)
    return pl.pallas_call(
        kern,
        grid=(T // tm,),
        in_specs=in_specs,
        out_specs=[pl.BlockSpec((tm, width), lambda m: (m, 0)) for width, _, _ in segs],
        out_shape=[jax.ShapeDtypeStruct((T, width), dt) for width, dt, _ in segs],
        compiler_params=_params("parallel"),
        name="proj_in",
    )(*args)


def _proj_out_kernel(*refs, with_skip):
    if with_skip:
        o_ref, xc_ref, skip_ref, z_ref, x_ref, mod_ref, w_ref, out_ref = refs
        a = o_ref[...] + skip_ref[...] * xc_ref[...]
    else:
        o_ref, z_ref, x_ref, mod_ref, w_ref, out_ref = refs
        a = o_ref[...]
    a = (a * _silu(z_ref[...])).astype(BF16)
    gate = mod_ref[:, 2 * D_MODEL:3 * D_MODEL]
    out_ref[...] = x_ref[...] + gate * _dot(a, w_ref[...])


def proj_out(o, z, x, mod4, layer, w_out, n_prompt, sample_seq, xc=None, skip=None):
    T, K = o.shape
    tm = TOKEN_TILE
    n_prompt_tiles = n_prompt // tm
    tiles_per_seq = sample_seq // tm

    def mod_row(m):
        return jnp.where(m < n_prompt_tiles, 0, 1 + (m - n_prompt_tiles) // tiles_per_seq)

    row = lambda m: (m, 0)
    in_specs = [pl.BlockSpec((tm, K), row)]
    args = [o]
    if xc is not None:
        in_specs += [pl.BlockSpec((tm, K), row), pl.BlockSpec((1, K), lambda m: (0, 0))]
        args += [xc, skip.reshape(1, K)]
    in_specs += [
        pl.BlockSpec((tm, K), row),
        pl.BlockSpec((tm, D_MODEL), row),
        pl.BlockSpec((None, None, 1, 3 * D_MODEL), lambda m: (layer, mod_row(m), 0, 0)),
        pl.BlockSpec((K, D_MODEL), lambda m: (0, 0)),
    ]
    args += [z, x, mod4, w_out]
    return pl.pallas_call(
        functools.partial(_proj_out_kernel, with_skip=xc is not None),
        grid=(T // tm,),
        in_specs=in_specs,
        out_specs=pl.BlockSpec((tm, D_MODEL), row),
        out_shape=jax.ShapeDtypeStruct((T, D_MODEL), F32),
        compiler_params=_params("parallel"),
        name="proj_out",
    )(*args)


def _final_norm_kernel(x_ref, g_ref, o_ref):
    x = x_ref[...]
    o_ref[...] = x * lax.rsqrt(jnp.mean(x * x, axis=-1, keepdims=True) + NORM_EPS) * g_ref[...]


def final_norm(x, g, row_start, rows):
    tm = TOKEN_TILE
    off = row_start // tm
    return pl.pallas_call(
        _final_norm_kernel,
        grid=(rows // tm,),
        in_specs=[pl.BlockSpec((tm, D_MODEL), lambda m: (m + off, 0)),
                  pl.BlockSpec((1, D_MODEL), lambda m: (0, 0))],
        out_specs=pl.BlockSpec((tm, D_MODEL), lambda m: (m, 0)),
        out_shape=jax.ShapeDtypeStruct((rows, D_MODEL), F32),
        compiler_params=_params("parallel"),
        name="final_norm",
    )(x, g.reshape(1, D_MODEL))


def _softmax_parts(q, k):
    s = _dot_nt(q, k)
    e = jnp.exp(s - jnp.max(s, axis=-1, keepdims=True))
    return e, jnp.sum(e, axis=-1, keepdims=True)


def _gqa_attn_kernel(q_ref, k_ref, v_ref, o_ref):
    k = k_ref[...].astype(BF16)
    v = v_ref[...].astype(BF16)
    for kv in range(GQA_KV_HEADS):
        kh = k[:, kv * HEAD_DIM:(kv + 1) * HEAD_DIM]
        vh = v[:, kv * HEAD_DIM:(kv + 1) * HEAD_DIM]
        for g in range(GQA_GROUP):
            lo = (kv * GQA_GROUP + g) * HEAD_DIM
            e, l = _softmax_parts(q_ref[:, lo:lo + HEAD_DIM], kh)
            o_ref[:, lo:lo + HEAD_DIM] = _dot(e.astype(BF16), vh) / l


def _diff_attn_kernel(q_ref, k_ref, v_ref, lam_ref, g_ref, o_ref, *, lam_init):
    lv = lam_ref[...]
    s1 = jnp.sum(lv[0:1] * lv[1:2], axis=-1, keepdims=True)
    s2 = jnp.sum(lv[2:3] * lv[3:4], axis=-1, keepdims=True)
    lam = jnp.exp(s1) - jnp.exp(s2) + lam_init
    k = k_ref[...].astype(BF16)
    v = v_ref[...].astype(BF16)
    for h in range(DIFF_HEADS):
        vh = v[:, h * DIFF_V_DIM:(h + 1) * DIFF_V_DIM]
        lo = h * 2 * HEAD_DIM
        e0, l0 = _softmax_parts(q_ref[:, lo:lo + HEAD_DIM], k[:, lo:lo + HEAD_DIM])
        e1, l1 = _softmax_parts(q_ref[:, lo + HEAD_DIM:lo + 2 * HEAD_DIM], k[:, lo + HEAD_DIM:lo + 2 * HEAD_DIM])
        o = _dot(e0.astype(BF16), vh) / l0 - lam * (_dot(e1.astype(BF16), vh) / l1)
        o = o * lax.rsqrt(jnp.mean(o * o, axis=-1, keepdims=True) + NORM_EPS) * g_ref[...]
        o_ref[:, h * DIFF_V_DIM:(h + 1) * DIFF_V_DIM] = o * (1.0 - lam_init)


def attention(kernel, q, k, v, batch, sq, sk, q_row0, extra=(), out=None):
    T, Wq = q.shape
    tq = ATTN_Q_TILE
    nq = sq // tq
    off = q_row0 // tq
    q_map = lambda b, i: (b * nq + i + off, 0)
    in_specs = [
        pl.BlockSpec((tq, Wq), q_map),
        pl.BlockSpec((sk, k.shape[1]), lambda b, i: (b, 0)),
        pl.BlockSpec((sk, v.shape[1]), lambda b, i: (b, 0)),
    ]
    args = [q, k, v, *extra]
    for a in extra:
        in_specs.append(pl.BlockSpec(a.shape, lambda b, i: (0, 0)))
    aliases = {}
    if out is not None:
        in_specs.append(pl.BlockSpec(memory_space=pl.ANY))
        aliases = {len(args): 0}
        args.append(out)
        kernel = functools.partial(_drop_alias_ref, kernel, len(args) - 1)
    return pl.pallas_call(
        kernel,
        grid=(batch, nq),
        in_specs=in_specs,
        out_specs=pl.BlockSpec((tq, D_MODEL), q_map),
        out_shape=jax.ShapeDtypeStruct((T, D_MODEL), F32),
        input_output_aliases=aliases,
        compiler_params=_params("parallel", "parallel"),
        name="attention",
    )(*args)


def _drop_alias_ref(kernel, pos, *refs):
    kernel(*refs[:pos], *refs[pos + 1:])


def _log_sigmoid(x):
    return jnp.minimum(x, 0.0) - jnp.log1p(jnp.exp(-jnp.abs(x)))


def _mlstm_pre_kernel(cur_ref, prev_ref, next_ref, cw_ref, cb_ref, bq_ref, bk_ref, bkt_ref, bv_ref,
                      wg_ref, bg_ref, xc_ref, q_ref, kt_ref, v_ref, gcol_ref, grow_ref,
                      *, n_prompt_tiles, tiles_per_seq):
    L = MLSTM_CHUNK
    t = pl.program_id(0)
    st = jnp.maximum(t - n_prompt_tiles, 0) % tiles_per_seq
    first = (t < n_prompt_tiles) | (st == 0)
    last = (t < n_prompt_tiles) | (st == tiles_per_seq - 1)
    cur = cur_ref[...]
    prev = jnp.where(first, 0.0, prev_ref[...])
    nxt = jnp.where(last, 0.0, next_ref[...])
    ext = jnp.concatenate([prev, cur, nxt], axis=0)
    n_ext = L + 2 * SUBLANES
    pad_l = MLSTM_CONV_K // 2
    acc = cb_ref[...] + cw_ref[pad_l:pad_l + 1, :] * cur
    for j in range(MLSTM_CONV_K):
        if j == pad_l:
            continue
        sh = (pad_l - j) % n_ext
        acc = acc + cw_ref[j:j + 1, :] * pltpu.roll(ext, sh, 0)[SUBLANES:SUBLANES + L]
    xc = _silu(acc)
    xc_ref[...] = xc

    g = jnp.zeros((L, LANES), F32)
    for c in range(MLSTM_INNER // LANES):
        sl = slice(c * LANES, (c + 1) * LANES)
        xcb = xc[:, sl].astype(BF16)
        xmb = cur[:, sl].astype(BF16)
        qc = _dot(xcb, bq_ref[c]).astype(BF16)
        kc = _dot(xcb, bk_ref[c]).astype(BF16)
        vc = _dot(xmb, bv_ref[c]).astype(BF16)
        q_ref[:, sl] = qc
        v_ref[:, sl] = vc
        kt_ref[sl, :] = _dot_nt(bkt_ref[c], xcb).astype(BF16)
        g = g + _dot(qc, wg_ref[0, sl, :]) + _dot(kc, wg_ref[1, sl, :]) + _dot(vc, wg_ref[2, sl, :])
    g = g + bg_ref[...]

    ls = _log_sigmoid(g)
    row = lax.broadcasted_iota(jnp.int32, (L, LANES), 0)
    pre, suf = ls, ls
    s = 1
    while s < L:
        pre = pre + jnp.where(row >= s, pltpu.roll(pre, s, 0), 0.0)
        suf = suf + jnp.where(row < L - s, pltpu.roll(suf, L - s, 0), 0.0)
        s *= 2
    lane = lax.broadcasted_iota(jnp.int32, (L, LANES), 1)
    H = MLSTM_HEADS
    out = jnp.where((lane >= H) & (lane < 2 * H), pre, jnp.where((lane >= 3 * H) & (lane < 4 * H), suf, g))
    gcol_ref[...] = out[:, 0:GATE_W]
    grow_ref[...] = out.T[0:GATE_W, :]


def mlstm_pre(xm, conv_w, conv_b, bq, bk, bkt, bv, wg, bg, n_prompt, sample_seq):
    T = xm.shape[0]
    L = MLSTM_CHUNK
    n_tiles = T // L
    r8 = L // SUBLANES
    n_blk8 = T // SUBLANES
    C = MLSTM_INNER
    kern = functools.partial(_mlstm_pre_kernel, n_prompt_tiles=n_prompt // L, tiles_per_seq=sample_seq // L)
    const3 = lambda t: (0, 0, 0)
    return pl.pallas_call(
        kern,
        grid=(n_tiles,),
        in_specs=[
            pl.BlockSpec((L, C), lambda t: (t, 0)),
            pl.BlockSpec((SUBLANES, C), lambda t: (jnp.maximum(t * r8 - 1, 0), 0)),
            pl.BlockSpec((SUBLANES, C), lambda t: (jnp.minimum((t + 1) * r8, n_blk8 - 1), 0)),
            pl.BlockSpec((MLSTM_CONV_K, C), lambda t: (0, 0)),
            pl.BlockSpec((1, C), lambda t: (0, 0)),
            pl.BlockSpec(bq.shape, const3),
            pl.BlockSpec(bk.shape, const3),
            pl.BlockSpec(bkt.shape, const3),
            pl.BlockSpec(bv.shape, const3),
            pl.BlockSpec(wg.shape, const3),
            pl.BlockSpec((1, LANES), lambda t: (0, 0)),
        ],
        out_specs=[
            pl.BlockSpec((L, C), lambda t: (t, 0)),
            pl.BlockSpec((L, C), lambda t: (t, 0)),
            pl.BlockSpec((None, C, L), lambda t: (t, 0, 0)),
            pl.BlockSpec((L, C), lambda t: (t, 0)),
            pl.BlockSpec((L, GATE_W), lambda t: (t, 0)),
            pl.BlockSpec((None, GATE_W, L), lambda t: (t, 0, 0)),
        ],
        out_shape=[
            jax.ShapeDtypeStruct((T, C), F32),
            jax.ShapeDtypeStruct((T, C), BF16),
            jax.ShapeDtypeStruct((n_tiles, C, L), BF16),
            jax.ShapeDtypeStruct((T, C), BF16),
            jax.ShapeDtypeStruct((T, GATE_W), F32),
            jax.ShapeDtypeStruct((n_tiles, GATE_W, L), F32),
        ],
        compiler_params=_params("parallel"),
        name="mlstm_pre",
    )(xm, xm, xm, conv_w, conv_b.reshape(1, C), bq, bk, bkt, bv, wg, bg)


def _gate_cols(gcol, h):
    lane = lax.broadcasted_iota(jnp.int32, gcol.shape, 1)
    pick = lambda idx: jnp.sum(jnp.where(lane == idx, gcol, 0.0), axis=-1, keepdims=True)
    return pick(MLSTM_HEADS + h), pick(3 * MLSTM_HEADS + h)


def _chunk_dir(qk, q, kt, v, b_col, b_row, i_row, tot, causal_mask, m_prev, C_prev, n_prev):
    ar = i_row - b_row
    d = jnp.where(causal_mask, b_col + ar, -jnp.inf)
    inter = b_col + m_prev
    mt = jnp.maximum(jnp.max(d, axis=-1, keepdims=True), inter)
    s = qk * jnp.exp(d - mt)
    num = _dot(s.astype(BF16), v)
    den = jnp.sum(s, axis=-1, keepdims=True)
    if C_prev is not None:
        w_inter = jnp.exp(inter - mt)
        num = num + w_inter * _dot(q, C_prev.astype(BF16))
        den = den + w_inter * jnp.sum(q.astype(F32) * n_prev, axis=-1, keepdims=True)
    h = num / jnp.maximum(jnp.abs(den), jnp.exp(-mt))

    g = tot + ar
    m_new = jnp.maximum(tot + m_prev, jnp.max(g, axis=-1, keepdims=True))
    w = jnp.exp(g - m_new)
    C_new = _dot((kt.astype(F32) * w).astype(BF16), v)
    w_hi = w.astype(BF16).astype(F32)
    rid = lax.broadcasted_iota(jnp.int32, (SUBLANES, w.shape[-1]), 0)
    w8 = jnp.where(rid == 0, w_hi, jnp.where(rid == 1, w - w_hi, 0.0)).astype(BF16)
    n8 = _dot_nt(w8, kt)
    n_new = n8[0:1] + n8[1:2]
    if C_prev is not None:
        decay = jnp.exp(tot + m_prev - m_new)
        C_new = decay * C_prev + C_new
        n_new = decay * n_prev + n_new
    return h, C_new, n_new, m_new


def _head_layernorm(h, g):
    mu = jnp.mean(h, axis=-1, keepdims=True)
    hc = h - mu
    var = jnp.mean(hc * hc, axis=-1, keepdims=True)
    return hc * lax.rsqrt(var + NORM_EPS) * g


def _chunk_masks(L):
    t_idx = lax.broadcasted_iota(jnp.int32, (L, L), 0)
    s_idx = lax.broadcasted_iota(jnp.int32, (L, L), 1)
    return s_idx <= t_idx, s_idx >= t_idx


def _mlstm_prompt_kernel(q_ref, kt_ref, v_ref, gcol_ref, grow_ref, mhg_ref,
                         hn_ref, C_ref, n_ref, m_ref):
    L = MLSTM_CHUNK
    H = MLSTM_HEADS
    h = pl.program_id(1)
    q, kt, v = q_ref[...], kt_ref[...], v_ref[...]
    qk = _dot(q, kt)
    fmask, bmask = _chunk_masks(L)
    bf_col, bb_col = _gate_cols(gcol_ref[...], h)
    zero = jnp.zeros((1, 1), F32)
    row = lambda idx: grow_ref[pl.ds(idx, 1), :]
    bf_row, bb_row = row(H + h), row(3 * H + h)
    hf, Cf, nf, mf = _chunk_dir(qk, q, kt, v, bf_col, bf_row, row(h), bf_row[:, L - 1:L], fmask, zero, None, None)
    hb, Cb, nb, mb = _chunk_dir(qk, q, kt, v, bb_col, bb_row, row(2 * H + h), bb_row[:, 0:1], bmask, zero, None, None)
    hn_ref[...] = _head_layernorm(hf + hb, mhg_ref[...])
    C_ref[0] = Cf
    C_ref[1] = Cb
    n_ref[0] = nf
    n_ref[1] = nb
    lane = lax.broadcasted_iota(jnp.int32, (1, LANES), 1)
    m_ref[...] = jnp.where(lane == 0, mf, jnp.where(lane == 1, mb, 0.0))


def mlstm_prompt(q, kt, v, gcol, grow, mh_g, batch):
    L, H, DH = MLSTM_CHUNK, MLSTM_HEADS, MLSTM_HEAD_DIM
    return pl.pallas_call(
        _mlstm_prompt_kernel,
        grid=(batch, H),
        in_specs=[
            pl.BlockSpec((L, DH), lambda b, h: (b, h)),
            pl.BlockSpec((None, DH, L), lambda b, h: (b, h, 0)),
            pl.BlockSpec((L, DH), lambda b, h: (b, h)),
            pl.BlockSpec((L, GATE_W), lambda b, h: (b, 0)),
            pl.BlockSpec((None, GATE_W, L), lambda b, h: (b, 0, 0)),
            pl.BlockSpec((1, DH), lambda b, h: (0, h)),
        ],
        out_specs=[
            pl.BlockSpec((L, DH), lambda b, h: (b, h)),
            pl.BlockSpec((None, None, 2, None, DH, DH), lambda b, h: (b, 0, 0, h, 0, 0)),
            pl.BlockSpec((None, 2, None, 1, DH), lambda b, h: (b, 0, h, 0, 0)),
            pl.BlockSpec((None, None, 1, LANES), lambda b, h: (b, h, 0, 0)),
        ],
        out_shape=[
            jax.ShapeDtypeStruct((q.shape[0], MLSTM_INNER), F32),
            jax.ShapeDtypeStruct((batch, 1, 2, H, DH, DH), F32),
            jax.ShapeDtypeStruct((batch, 2, H, 1, DH), F32),
            jax.ShapeDtypeStruct((batch, H, 1, LANES), F32),
        ],
        compiler_params=_params("parallel", "parallel"),
        name="mlstm_prompt",
    )(q, kt, v, gcol, grow, mh_g.reshape(1, MLSTM_INNER))


def _mlstm_sample_kernel(q_ref, kt_ref, v_ref, gcol_ref, grow_ref, mhg_ref, C0_ref, n0_ref, m0_ref, hn_in_ref,
                         hn_ref, hf_ref, hb_ref, C_ref, n_ref, m_ref, *, n_chunks):
    L = MLSTM_CHUNK
    H = MLSTM_HEADS
    h = pl.program_id(1)
    fmask, bmask = _chunk_masks(L)
    C_ref[...] = C0_ref[...]
    n_ref[...] = n0_ref[...]
    m_ref[...] = m0_ref[...]

    def one(c, direction):
        rows = pl.ds(pl.multiple_of(c * L, L), L)
        q, kt, v = q_ref[rows, :], kt_ref[c], v_ref[rows, :]
        qk = _dot(q, kt)
        gcol = gcol_ref[rows, :]
        b_col = _gate_cols(gcol, h)[direction]
        grow = lambda idx: grow_ref[c, pl.ds(idx, 1), :]
        b_row = grow((2 * direction + 1) * H + h)
        i_row = grow(2 * direction * H + h)
        tot = b_row[:, L - 1:L] if direction == 0 else b_row[:, 0:1]
        mask = fmask if direction == 0 else bmask
        m_prev = m_ref[direction][:, 0:1]
        hh, C_new, n_new, m_new = _chunk_dir(qk, q, kt, v, b_col, b_row, i_row, tot, mask, m_prev,
                                             C_ref[direction], n_ref[direction])
        C_ref[direction] = C_new
        n_ref[direction] = n_new
        m_ref[direction] = jnp.broadcast_to(m_new, (1, LANES))
        (hf_ref if direction == 0 else hb_ref)[rows, :] = hh

    def body(i, carry):
        one(i, 0)
        one(n_chunks - 1 - i, 1)
        return carry

    lax.fori_loop(0, n_chunks, body, 0)

    def norm(c, carry):
        rows = pl.ds(pl.multiple_of(c * L, L), L)
        hn_ref[rows, :] = _head_layernorm(hf_ref[rows, :] + hb_ref[rows, :], mhg_ref[...])
        return carry

    lax.fori_loop(0, n_chunks, norm, 0)


def mlstm_sample(q, kt, v, gcol, grow, mh_g, C0, n0, m0, hn, n_prompt, batch, seq):
    L, H, DH = MLSTM_CHUNK, MLSTM_HEADS, MLSTM_HEAD_DIM
    nc = seq // L
    row_off = n_prompt // seq
    tile_off = n_prompt // L // nc
    return pl.pallas_call(
        functools.partial(_mlstm_sample_kernel, n_chunks=nc),
        grid=(batch, H),
        in_specs=[
            pl.BlockSpec((seq, DH), lambda b, h: (b + row_off, h)),
            pl.BlockSpec((nc, DH, L), lambda b, h: (b + tile_off, h, 0)),
            pl.BlockSpec((seq, DH), lambda b, h: (b + row_off, h)),
            pl.BlockSpec((seq, GATE_W), lambda b, h: (b + row_off, 0)),
            pl.BlockSpec((nc, GATE_W, L), lambda b, h: (b + tile_off, 0, 0)),
            pl.BlockSpec((1, DH), lambda b, h: (0, h)),
            pl.BlockSpec((None, 2, None, DH, DH), lambda b, h: (b, 0, h, 0, 0)),
            pl.BlockSpec((None, 2, None, 1, DH), lambda b, h: (b, 0, h, 0, 0)),
            pl.BlockSpec((None, None, 2, 1, LANES), lambda b, h: (b, h, 0, 0, 0)),
            pl.BlockSpec(memory_space=pl.ANY),
        ],
        out_specs=pl.BlockSpec((seq, DH), lambda b, h: (b + row_off, h)),
        out_shape=jax.ShapeDtypeStruct(hn.shape, F32),
        input_output_aliases={9: 0},
        scratch_shapes=[
            pltpu.VMEM((seq, DH), F32),
            pltpu.VMEM((seq, DH), F32),
            pltpu.VMEM((2, DH, DH), F32),
            pltpu.VMEM((2, 1, DH), F32),
            pltpu.VMEM((2, 1, LANES), F32),
        ],
        compiler_params=_params("parallel", "parallel"),
        name="mlstm_sample",
    )(q, kt, v, gcol, grow, mh_g.reshape(1, MLSTM_INNER), C0, n0, m0, hn)


def _rope_tables(seq):
    t = jnp.arange(seq)
    row = (t // GRID_W).astype(F32)
    col = (t % GRID_W).astype(F32)
    n_freq = HEAD_DIM // 4
    inv = ROPE_THETA ** (-jnp.arange(n_freq, dtype=F32) / n_freq)
    ang = jnp.concatenate([row[:, None] * inv, col[:, None] * inv], axis=-1)
    cos, sin = jnp.cos(ang), jnp.sin(ang)
    reps = PROJ_N_TILE // HEAD_DIM
    cos_t = jnp.tile(jnp.concatenate([cos, cos], axis=-1), (1, reps))
    sin_t = jnp.tile(jnp.concatenate([-sin, sin], axis=-1), (1, reps))
    ident = jnp.ones((TOKEN_TILE, PROJ_N_TILE), F32)
    return (jnp.concatenate([cos_t, ident], axis=0), jnp.concatenate([sin_t, 0.0 * ident], axis=0))


def _block_diag(w):
    per = LANES // MLSTM_QKV_BLOCK
    w4 = w.reshape(-1, per, MLSTM_QKV_BLOCK, MLSTM_QKV_BLOCK)
    eye = jnp.eye(per, dtype=w.dtype)
    return jnp.einsum('kacd,ab->kacbd', w4, eye).reshape(-1, LANES, LANES)


def _diff_lambda_init(layer_idx):
    return 0.8 - 0.6 * math.exp(-0.3 * layer_idx)


def kernel(x_prompt, x_sample, cache_gqa_k, cache_gqa_v, cache_diff_k, cache_diff_v, state_mlstm_C, state_mlstm_n, state_mlstm_m, c, c_ctx, norm_g, w_mod, b_mod, gqa_w_in, gqa_q_norm_g, gqa_k_norm_g, gqa_w_out, diff_w_in, diff_lambda_q1, diff_lambda_k1, diff_lambda_q2, diff_lambda_k2, diff_subln_g, diff_w_out, mlstm_w_in, mlstm_conv_w, mlstm_conv_b, mlstm_w_q, mlstm_w_k, mlstm_w_v, mlstm_w_gate_f, mlstm_b_gate_f, mlstm_w_gate_b, mlstm_b_gate_b, mlstm_mh_norm_g, mlstm_skip, mlstm_w_out, final_norm_g):
    Bp, Sp, D = x_prompt.shape
    Bs, Ss, _ = x_sample.shape
    Tp, Ts = Bp * Sp, Bs * Ss
    assert Sp == MLSTM_CHUNK and Ss % TOKEN_TILE == 0 and Tp % TOKEN_TILE == 0
    H, DH = MLSTM_HEADS, MLSTM_HEAD_DIM

    x = jnp.concatenate([x_prompt.reshape(Tp, D), x_sample.reshape(Ts, D)], axis=0)
    cond = jnp.zeros((SUBLANES, D), F32).at[0].set(c_ctx).at[1:1 + Bs].set(c)
    mod4 = adaln_all(cond, w_mod, b_mod).reshape(DEPTH, SUBLANES, 1, 3 * D)
    rope_tabs = _rope_tables(Ss)
    gi = jnp.arange(PROJ_N_TILE) // HEAD_DIM
    gmat = (gi[:, None] == gi[None, :]).astype(BF16)
    tile_g = lambda g: jnp.tile(g, PROJ_N_TILE // HEAD_DIM).reshape(1, PROJ_N_TILE)

    def with_cache(new, cache):
        w = new.shape[1]
        both = jnp.concatenate([new[Tp:].reshape(Bs, Ss, w), cache.reshape(Bs, -1, w)], axis=1)
        return both.reshape(-1, w).astype(BF16)

    gqa_k_list, gqa_v_list, diff_k_list, diff_v_list = [], [], [], []
    mC_list, mn_list, mm_list = [], [], []
    for i in range(DEPTH):
        kind, j = i % N_MIXERS, i // N_MIXERS
        common = dict(n_prompt=Tp, sample_seq=Ss)
        if kind == 0:
            segs = ((GQA_Q_W, BF16, "q_norm"), (GQA_KV_W, F32, "k_norm"), (GQA_KV_W, F32, "plain"), (GQA_Q_W, F32, "plain"))
            q, k, v, z = proj_in(x, norm_g[i], mod4, i, gqa_w_in[j].astype(BF16), segs, rope_tabs=rope_tabs,
                                 head_norm=(gmat, tile_g(gqa_q_norm_g[j]), tile_g(gqa_k_norm_g[j])), **common)
            o = attention(_gqa_attn_kernel, q, k, v, Bp, Sp, Sp, 0)
            k_all = with_cache(k, cache_gqa_k[:, j])
            v_all = with_cache(v, cache_gqa_v[:, j])
            o = attention(_gqa_attn_kernel, q, k_all, v_all, Bs, Ss, k_all.shape[0] // Bs, Tp, out=o)
            x = proj_out(o, z, x, mod4, i, gqa_w_out[j].astype(BF16), **common)
            gqa_k_list.append(k[:Tp].reshape(Bp, Sp, GQA_KV_HEADS, HEAD_DIM))
            gqa_v_list.append(v[:Tp].reshape(Bp, Sp, GQA_KV_HEADS, HEAD_DIM))
        elif kind == 1:
            lam_init = _diff_lambda_init(i)
            segs = ((DIFF_QK_W, BF16, "q"), (DIFF_QK_W, F32, "k"), (DIFF_BRANCH, F32, "plain"), (DIFF_BRANCH, F32, "plain"))
            q, k, v, z = proj_in(x, norm_g[i], mod4, i, diff_w_in[j].astype(BF16), segs, rope_tabs=rope_tabs, **common)
            lam_vecs = jnp.stack([diff_lambda_q1[j], diff_lambda_k1[j], diff_lambda_q2[j], diff_lambda_k2[j]])
            extra = (lam_vecs, diff_subln_g[j].reshape(1, DIFF_V_DIM))
            kern = functools.partial(_diff_attn_kernel, lam_init=lam_init)
            o = attention(kern, q, k, v, Bp, Sp, Sp, 0, extra)
            k_all = with_cache(k, cache_diff_k[:, j])
            v_all = with_cache(v, cache_diff_v[:, j])
            o = attention(kern, q, k_all, v_all, Bs, Ss, k_all.shape[0] // Bs, Tp, extra, out=o)
            x = proj_out(o, z, x, mod4, i, diff_w_out[j].astype(BF16), **common)
            diff_k_list.append(k[:Tp].reshape(Bp, Sp, DIFF_HEADS, 2, HEAD_DIM))
            diff_v_list.append(v[:Tp].reshape(Bp, Sp, DIFF_HEADS, DIFF_V_DIM))
        else:
            segs = ((MLSTM_INNER, F32, "plain"), (MLSTM_INNER, F32, "plain"))
            xm, z = proj_in(x, norm_g[i], mod4, i, mlstm_w_in[j].astype(BF16), segs, **common)
            bq = _block_diag(mlstm_w_q[j]).astype(BF16)
            bk = _block_diag(mlstm_w_k[j])
            bkt = (bk * (DH ** -0.5)).transpose(0, 2, 1).astype(BF16)
            bv = _block_diag(mlstm_w_v[j]).astype(BF16)
            wg = jnp.concatenate([mlstm_w_gate_f[j][:, :H], mlstm_w_gate_f[j][:, H:],
                                  mlstm_w_gate_b[j][:, :H], mlstm_w_gate_b[j][:, H:]], axis=-1)
            wg = jnp.pad(wg, ((0, 0), (0, LANES - GATE_W))).astype(BF16).reshape(3, MLSTM_INNER, LANES)
            bg = jnp.pad(jnp.concatenate([mlstm_b_gate_f[j], mlstm_b_gate_b[j]]), (0, LANES - GATE_W)).reshape(1, LANES)
            xc, q, kt, v, gcol, grow = mlstm_pre(xm, mlstm_conv_w[j], mlstm_conv_b[j], bq, bk.astype(BF16), bkt, bv,
                                                 wg, bg, **common)
            hn, Cp, np_, mp_ = mlstm_prompt(q, kt, v, gcol, grow, mlstm_mh_norm_g[j], Bp)
            C0 = state_mlstm_C[:, j]
            n0 = state_mlstm_n[:, j].reshape(Bs, 2, H, 1, DH)
            m0 = jnp.broadcast_to(state_mlstm_m[:, j].transpose(0, 2, 1)[..., None, None], (Bs, H, 2, 1, LANES))
            hn = mlstm_sample(q, kt, v, gcol, grow, mlstm_mh_norm_g[j], C0, n0, m0, hn, Tp, Bs, Ss)
            x = proj_out(hn, z, x, mod4, i, mlstm_w_out[j].astype(BF16), xc=xc, skip=mlstm_skip[j], **common)
            mC_list.append(Cp)
            mn_list.append(np_.reshape(Bp, 1, 2, H, DH))
            mm_list.append(mp_[:, :, 0, 0:2].transpose(0, 2, 1)[:, None])

    cat1 = lambda parts: parts[0] if len(parts) == 1 else jnp.concatenate(parts, axis=1)
    y_prompt = final_norm(x, final_norm_g, 0, Tp).reshape(Bp, Sp, D)
    y_sample = final_norm(x, final_norm_g, Tp, Ts).reshape(Bs, Ss, D)
    return (y_prompt, y_sample,
            jnp.stack(gqa_k_list, axis=1), jnp.stack(gqa_v_list, axis=1),
            jnp.stack(diff_k_list, axis=1), jnp.stack(diff_v_list, axis=1),
            cat1(mC_list), cat1(mn_list), cat1(mm_list))
```

```python
import functools
import math

import jax
import jax.numpy as jnp
from jax import lax
from jax.experimental import pallas as pl
from jax.experimental.pallas import tpu as pltpu

D_MODEL = 1024
DEPTH = 4
GRID_W = 64
N_MIXERS = 3
HEAD_DIM = 64
ROPE_THETA = 10000.0
NORM_EPS = 1e-6

GQA_HEADS = 16
GQA_KV_HEADS = 4
GQA_GROUP = GQA_HEADS // GQA_KV_HEADS
GQA_Q_W = GQA_HEADS * HEAD_DIM
GQA_KV_W = GQA_KV_HEADS * HEAD_DIM

DIFF_HEADS = 8
DIFF_QK_W = DIFF_HEADS * 2 * HEAD_DIM
DIFF_V_DIM = 2 * HEAD_DIM
DIFF_BRANCH = DIFF_HEADS * DIFF_V_DIM

MLSTM_INNER = 2 * D_MODEL
MLSTM_HEADS = 4
MLSTM_HEAD_DIM = MLSTM_INNER // MLSTM_HEADS
MLSTM_QKV_BLOCK = 4
MLSTM_CONV_K = 4

LANES = 128
SUBLANES = 8
VMEM_LIMIT_BYTES = 56 * 1024 * 1024

TOKEN_TILE = 512
PROJ_N_TILE = 256
ATTN_Q_TILE = 256
MLSTM_CHUNK = 256
GATE_W = 4 * MLSTM_HEADS

F32 = jnp.float32
BF16 = jnp.bfloat16


def _params(*semantics):
    return pltpu.CompilerParams(dimension_semantics=semantics, vmem_limit_bytes=VMEM_LIMIT_BYTES)


def _silu(x):
    return x * (1.0 / (1.0 + jnp.exp(-x)))


def _dot(a, b):
    return jnp.dot(a, b, preferred_element_type=F32)


def _dot_nt(a, b):
    return lax.dot_general(a, b, (((1,), (1,)), ((), ())), preferred_element_type=F32)


class TokenLayout:
    def __init__(self, n_prompt, seq, tile):
        self.prompt_tiles = n_prompt // tile
        self.tiles_per_seq = seq // tile

    def is_prompt(self, m):
        return m < self.prompt_tiles

    def cond_row(self, m):
        return jnp.where(m < self.prompt_tiles, 0, 1 + (m - self.prompt_tiles) // self.tiles_per_seq)

    def prompt_block(self, m):
        return jnp.minimum(m, self.prompt_tiles - 1)

    def sample_block(self, m):
        return jnp.maximum(m - self.prompt_tiles, 0)


def _mod_kernel(cond_ref, w_ref, b_ref, o_ref):
    a = _silu(cond_ref[...]).astype(BF16)
    o_ref[...] = _dot(a, w_ref[...].astype(BF16)) + b_ref[...]


def adaln_all(cond, w_mod, b_mod):
    tn = 1024
    return pl.pallas_call(
        _mod_kernel,
        grid=(DEPTH, 3 * D_MODEL // tn),
        in_specs=[
            pl.BlockSpec((SUBLANES, D_MODEL), lambda i, n: (0, 0)),
            pl.BlockSpec((None, D_MODEL, tn), lambda i, n: (i, 0, n)),
            pl.BlockSpec((None, 1, tn), lambda i, n: (i, 0, n)),
        ],
        out_specs=pl.BlockSpec((None, SUBLANES, tn), lambda i, n: (i, 0, n)),
        out_shape=jax.ShapeDtypeStruct((DEPTH, SUBLANES, 3 * D_MODEL), F32),
        compiler_params=_params("parallel", "parallel"),
        name="adaln_mod",
    )(cond, w_mod, b_mod.reshape(DEPTH, 1, 3 * D_MODEL))


def _swap_half_heads(y):
    w = y.shape[-1]
    lane = lax.broadcasted_iota(jnp.int32, y.shape, 1)
    upper = (lane & (HEAD_DIM // 2)) != 0
    return jnp.where(upper, pltpu.roll(y, HEAD_DIM // 2, 1), pltpu.roll(y, w - HEAD_DIM // 2, 1))


def _proj_in_kernel(*refs, segs, layout, has_norm, has_rope):
    it = iter(refs)
    x_ref, g_ref, mod_ref, w_ref = next(it), next(it), next(it), next(it)
    if has_rope:
        cos_ref, sin_ref = next(it), next(it)
    if has_norm:
        gmat_ref, qg_ref, kg_ref = next(it), next(it), next(it)
    out_refs = [(next(it), next(it)) if split else (next(it),) for _, _, split in segs]
    cw = PROJ_N_TILE
    is_prompt = layout.is_prompt(pl.program_id(0))

    x = x_ref[...]
    y = x * lax.rsqrt(jnp.mean(x * x, axis=-1, keepdims=True) + NORM_EPS) * g_ref[...]
    shift = mod_ref[:, 0:D_MODEL]
    scale = mod_ref[:, D_MODEL:2 * D_MODEL]
    h = (y * (1.0 + scale) + shift).astype(BF16)

    col = 0
    for (width, mode, split), o_refs in zip(segs, out_refs):
        for c in range(width // cw):
            acc = _dot(h, w_ref[:, col:col + cw])
            col += cw
            if mode in ("q_norm", "k_norm"):
                ss = _dot((acc * acc).astype(BF16), gmat_ref[...])
                gvec = qg_ref[...] if mode == "q_norm" else kg_ref[...]
                acc = acc * lax.rsqrt(ss * (1.0 / HEAD_DIM) + NORM_EPS) * gvec
            if has_rope and mode != "plain":
                acc = acc * cos_ref[...] + _swap_half_heads(acc) * sin_ref[...]
            if mode in ("q_norm", "q"):
                acc = acc * (HEAD_DIM ** -0.5)
            cols = slice(c * cw, (c + 1) * cw)
            if split:
                def store(ref, acc=acc, cols=cols):
                    ref[:, cols] = acc.astype(ref.dtype)
                pl.when(is_prompt)(functools.partial(store, o_refs[0]))
                pl.when(jnp.logical_not(is_prompt))(functools.partial(store, o_refs[1]))
            else:
                o_refs[0][:, cols] = acc.astype(o_refs[0].dtype)


def proj_in(x, norm_g, mod4, layer, w, segs, n_prompt, sample_seq, rope_tabs=None, head_norm=None):
    T = x.shape[0]
    tm, cw = TOKEN_TILE, PROJ_N_TILE
    lay = TokenLayout(n_prompt, sample_seq, tm)
    N = w.shape[1]
    assert N == sum(s[0] for s in segs) and all(s[0] % cw == 0 for s in segs)

    in_specs = [
        pl.BlockSpec((tm, D_MODEL), lambda m: (m, 0)),
        pl.BlockSpec((1, D_MODEL), lambda m: (0, 0)),
        pl.BlockSpec((None, None, 1, 3 * D_MODEL), lambda m: (layer, lay.cond_row(m), 0, 0)),
        pl.BlockSpec((D_MODEL, N), lambda m: (0, 0)),
    ]
    args = [x, norm_g.reshape(1, D_MODEL), mod4, w]
    if rope_tabs is not None:
        def pos_block(m):
            return (jnp.where(lay.is_prompt(m), lay.tiles_per_seq, lay.sample_block(m) % lay.tiles_per_seq), 0)
        in_specs += [pl.BlockSpec((tm, cw), pos_block), pl.BlockSpec((tm, cw), pos_block)]
        args += list(rope_tabs)
    if head_norm is not None:
        in_specs += [pl.BlockSpec((cw, cw), lambda m: (0, 0)),
                     pl.BlockSpec((1, cw), lambda m: (0, 0)),
                     pl.BlockSpec((1, cw), lambda m: (0, 0))]
        args += list(head_norm)

    out_specs, out_shape, kern_segs = [], [], []
    for width, mode, out in segs:
        split = isinstance(out, str)
        kern_segs.append((width, mode, split))
        if split:
            out_specs += [pl.BlockSpec((tm, width), lambda m: (lay.prompt_block(m), 0)),
                          pl.BlockSpec((tm, width), lambda m: (lay.sample_block(m), 0))]
            out_shape += [jax.ShapeDtypeStruct((n_prompt, width), F32),
                          jax.ShapeDtypeStruct((T - n_prompt, width), BF16)]
        else:
            out_specs.append(pl.BlockSpec((tm, width), lambda m: (m, 0)))
            out_shape.append(jax.ShapeDtypeStruct((T, width), out))

    kern = functools.partial(_proj_in_kernel, segs=tuple(kern_segs), layout=lay,
                             has_norm=head_norm is not None, has_rope=rope_tabs is not None)
    return pl.pallas_call(
        kern,
        grid=(T // tm,),
        in_specs=in_specs,
        out_specs=out_specs,
        out_shape=out_shape,
        compiler_params=_params("arbitrary"),
        name="proj_in",
    )(*args)


def _proj_out_kernel(*refs, layout, with_skip):
    if with_skip:
        op_ref, os_ref, xc_ref, skip_ref, z_ref, x_ref, mod_ref, w_ref, out_ref = refs
    else:
        op_ref, os_ref, z_ref, x_ref, mod_ref, w_ref, out_ref = refs
    a = jnp.where(layout.is_prompt(pl.program_id(0)), op_ref[...].astype(F32), os_ref[...].astype(F32))
    if with_skip:
        a = a + skip_ref[...] * xc_ref[...].astype(F32)
    z = z_ref[...].astype(F32)
    a = (a * _silu(z)).astype(BF16)
    gate = mod_ref[:, 2 * D_MODEL:3 * D_MODEL]
    out_ref[...] = x_ref[...] + gate * _dot(a, w_ref[...])


def proj_out(o_prompt, o_sample, z, x, mod4, layer, w_out, n_prompt, sample_seq, xc=None, skip=None):
    T = x.shape[0]
    K = o_prompt.shape[1]
    tm = TOKEN_TILE
    lay = TokenLayout(n_prompt, sample_seq, tm)
    row = lambda m: (m, 0)
    in_specs = [pl.BlockSpec((tm, K), lambda m: (lay.prompt_block(m), 0)),
                pl.BlockSpec((tm, K), lambda m: (lay.sample_block(m), 0))]
    args = [o_prompt, o_sample]
    if xc is not None:
        in_specs += [pl.BlockSpec((tm, K), row), pl.BlockSpec((1, K), lambda m: (0, 0))]
        args += [xc, skip.reshape(1, K)]
    in_specs += [
        pl.BlockSpec((tm, K), row),
        pl.BlockSpec((tm, D_MODEL), row),
        pl.BlockSpec((None, None, 1, 3 * D_MODEL), lambda m: (layer, lay.cond_row(m), 0, 0)),
        pl.BlockSpec((K, D_MODEL), lambda m: (0, 0)),
    ]
    args += [z, x, mod4, w_out]
    return pl.pallas_call(
        functools.partial(_proj_out_kernel, layout=lay, with_skip=xc is not None),
        grid=(T // tm,),
        in_specs=in_specs,
        out_specs=pl.BlockSpec((tm, D_MODEL), row),
        out_shape=jax.ShapeDtypeStruct((T, D_MODEL), F32),
        compiler_params=_params("parallel"),
        name="proj_out",
    )(*args)


def _final_norm_kernel(x_ref, g_ref, o_ref):
    x = x_ref[...]
    o_ref[...] = x * lax.rsqrt(jnp.mean(x * x, axis=-1, keepdims=True) + NORM_EPS) * g_ref[...]


def final_norm(x, g, row_start, rows):
    tm = TOKEN_TILE
    off = row_start // tm
    return pl.pallas_call(
        _final_norm_kernel,
        grid=(rows // tm,),
        in_specs=[pl.BlockSpec((tm, D_MODEL), lambda m: (m + off, 0)),
                  pl.BlockSpec((1, D_MODEL), lambda m: (0, 0))],
        out_specs=pl.BlockSpec((tm, D_MODEL), lambda m: (m, 0)),
        out_shape=jax.ShapeDtypeStruct((rows, D_MODEL), F32),
        compiler_params=_params("parallel"),
        name="final_norm",
    )(x, g.reshape(1, D_MODEL))


def _low_half(shape):
    return lax.broadcasted_iota(jnp.int32, shape, 1) < HEAD_DIM


def _probs(q, k_parts):
    s = [_dot_nt(q, k) for k in k_parts]
    m = functools.reduce(jnp.maximum, [jnp.max(x, axis=-1, keepdims=True) for x in s])
    return [jnp.exp(x - m).astype(BF16) for x in s]


def _pv(p_parts, v_ref, head, rows):
    acc = None
    for p, r in zip(p_parts, rows):
        t = _dot(p, v_ref[head, r, :])
        acc = t if acc is None else acc + t
    return acc


def _row_parts(k_ref, kc_ref):
    n_new = k_ref.shape[0]
    rows = [slice(0, n_new)]
    if kc_ref is not None:
        rows.append(slice(n_new, n_new + kc_ref.shape[0]))
    return rows


def _gqa_attn_kernel(*refs, has_ctx):
    if has_ctx:
        q_ref, k_ref, v_ref, kc_ref, vc_ref, o_ref, ve_ref, vo_ref = refs
    else:
        q_ref, k_ref, v_ref, o_ref, ve_ref, vo_ref = refs
        kc_ref = vc_ref = None
    rows = _row_parts(k_ref, kc_ref)

    @pl.when(pl.program_id(1) == 0)
    def _():
        for src, r in zip([v_ref, vc_ref][:len(rows)], rows):
            for c in range(GQA_KV_HEADS):
                slab = src[:, (c // 2) * LANES:(c // 2 + 1) * LANES].astype(F32)
                low = _low_half(slab.shape)
                if c % 2 == 0:
                    ve = jnp.where(low, slab, 0.0)
                    vo = pltpu.roll(ve, HEAD_DIM, 1)
                else:
                    vo = jnp.where(low, 0.0, slab)
                    ve = pltpu.roll(vo, HEAD_DIM, 1)
                ones_e = jnp.where(low, 1.0, 0.0)
                ve_ref[c, r, 0:LANES] = ve.astype(BF16)
                ve_ref[c, r, LANES:2 * LANES] = ones_e.astype(BF16)
                vo_ref[c, r, 0:LANES] = vo.astype(BF16)
                vo_ref[c, r, LANES:2 * LANES] = (1.0 - ones_e).astype(BF16)

    k_srcs = [k_ref, kc_ref][:len(rows)]
    for pair in range(GQA_HEADS // 2):
        c = pair // (GQA_GROUP // 2)
        q = q_ref[:, pair * LANES:(pair + 1) * LANES].astype(F32)
        low = _low_half(q.shape)
        q_sw = pltpu.roll(q, HEAD_DIM, 1)
        if c % 2 == 0:
            qe, qo = jnp.where(low, q, 0.0), jnp.where(low, q_sw, 0.0)
        else:
            qe, qo = jnp.where(low, 0.0, q_sw), jnp.where(low, 0.0, q)
        k_parts = [s[:, (c // 2) * LANES:(c // 2 + 1) * LANES].astype(BF16) for s in k_srcs]
        acc = _pv(_probs(qe.astype(BF16), k_parts), ve_ref, c, rows)
        acc = acc + _pv(_probs(qo.astype(BF16), k_parts), vo_ref, c, rows)
        o_ref[:, pair * LANES:(pair + 1) * LANES] = (acc[:, 0:LANES] / acc[:, LANES:2 * LANES]).astype(o_ref.dtype)


def _diff_attn_kernel(*refs, has_ctx, lam_init):
    if has_ctx:
        q_ref, k_ref, v_ref, kc_ref, vc_ref, lam_ref, g_ref, o_ref, vx_ref = refs
    else:
        q_ref, k_ref, v_ref, lam_ref, g_ref, o_ref, vx_ref = refs
        kc_ref = vc_ref = None
    rows = _row_parts(k_ref, kc_ref)

    @pl.when(pl.program_id(1) == 0)
    def _():
        for src, r in zip([v_ref, vc_ref][:len(rows)], rows):
            for h in range(DIFF_HEADS):
                vx_ref[h, r, 0:LANES] = src[:, h * LANES:(h + 1) * LANES].astype(BF16)
                vx_ref[h, r, LANES:2 * LANES] = jnp.ones((r.stop - r.start, LANES), BF16)

    lv = lam_ref[...]
    s1 = jnp.sum(lv[0:1] * lv[1:2], axis=-1, keepdims=True)
    s2 = jnp.sum(lv[2:3] * lv[3:4], axis=-1, keepdims=True)
    lam = jnp.exp(s1) - jnp.exp(s2) + lam_init
    k_srcs = [k_ref, kc_ref][:len(rows)]
    for h in range(DIFF_HEADS):
        q = q_ref[:, h * LANES:(h + 1) * LANES].astype(F32)
        low = _low_half(q.shape)
        q0, q1 = jnp.where(low, q, 0.0).astype(BF16), jnp.where(low, 0.0, q).astype(BF16)
        k_parts = [s[:, h * LANES:(h + 1) * LANES].astype(BF16) for s in k_srcs]
        a0 = _pv(_probs(q0, k_parts), vx_ref, h, rows)
        a1 = _pv(_probs(q1, k_parts), vx_ref, h, rows)
        o = a0[:, 0:LANES] / a0[:, LANES:2 * LANES] - lam * (a1[:, 0:LANES] / a1[:, LANES:2 * LANES])
        o = o * lax.rsqrt(jnp.mean(o * o, axis=-1, keepdims=True) + NORM_EPS) * g_ref[...]
        o_ref[:, h * LANES:(h + 1) * LANES] = (o * (1.0 - lam_init)).astype(o_ref.dtype)


def attention(kernel, v_scratch, q, k, v, batch, sq, q_row0, ctx=None, extra=()):
    tq = ATTN_Q_TILE
    nq = sq // tq
    off = q_row0 // tq
    in_specs = [
        pl.BlockSpec((tq, q.shape[1]), lambda b, i: (b * nq + i + off, 0)),
        pl.BlockSpec((sq, k.shape[1]), lambda b, i: (b, 0)),
        pl.BlockSpec((sq, v.shape[1]), lambda b, i: (b, 0)),
    ]
    args = [q, k, v]
    sk = sq
    if ctx is not None:
        kc, vc, layer = ctx
        sk += kc.shape[2]
        in_specs += [pl.BlockSpec((None, None) + kc.shape[2:], lambda b, i: (b, layer, 0, 0)),
                     pl.BlockSpec((None, None) + vc.shape[2:], lambda b, i: (b, layer, 0, 0))]
        args += [kc, vc]
    for a in extra:
        in_specs.append(pl.BlockSpec(a.shape, lambda b, i: (0, 0)))
    args += list(extra)
    return pl.pallas_call(
        functools.partial(kernel, has_ctx=ctx is not None),
        grid=(batch, nq),
        in_specs=in_specs,
        out_specs=pl.BlockSpec((tq, D_MODEL), lambda b, i: (b * nq + i, 0)),
        out_shape=jax.ShapeDtypeStruct((batch * sq, D_MODEL), BF16),
        scratch_shapes=[pltpu.VMEM((heads, sk, width), BF16) for heads, width in v_scratch],
        compiler_params=_params("parallel", "arbitrary"),
        name="attention",
    )(*args)


def _log_sigmoid(x):
    return jnp.minimum(x, 0.0) - jnp.log1p(jnp.exp(-jnp.abs(x)))


def _mlstm_pre_kernel(cur_ref, prev_ref, next_ref, cw_ref, cb_ref, bq_ref, bk_ref, bkt_ref, bv_ref,
                      wg_ref, bg_ref, xc_ref, q_ref, kt_ref, v_ref, gcol_ref, grow_ref,
                      *, n_prompt_tiles, tiles_per_seq):
    L = MLSTM_CHUNK
    t = pl.program_id(0)
    st = jnp.maximum(t - n_prompt_tiles, 0) % tiles_per_seq
    first = (t < n_prompt_tiles) | (st == 0)
    last = (t < n_prompt_tiles) | (st == tiles_per_seq - 1)
    cur = cur_ref[...]
    prev = jnp.where(first, 0.0, prev_ref[...])
    nxt = jnp.where(last, 0.0, next_ref[...])
    ext = jnp.concatenate([prev, cur, nxt], axis=0)
    n_ext = L + 2 * SUBLANES
    pad_l = MLSTM_CONV_K // 2
    acc = cb_ref[...] + cw_ref[pad_l:pad_l + 1, :] * cur
    for j in range(MLSTM_CONV_K):
        if j == pad_l:
            continue
        sh = (pad_l - j) % n_ext
        acc = acc + cw_ref[j:j + 1, :] * pltpu.roll(ext, sh, 0)[SUBLANES:SUBLANES + L]
    xc = _silu(acc)
    xc_ref[...] = xc.astype(xc_ref.dtype)

    g = jnp.zeros((L, LANES), F32)
    for c in range(MLSTM_INNER // LANES):
        sl = slice(c * LANES, (c + 1) * LANES)
        xcb = xc[:, sl].astype(BF16)
        xmb = cur[:, sl].astype(BF16)
        qc = _dot(xcb, bq_ref[c]).astype(BF16)
        kc = _dot(xcb, bk_ref[c]).astype(BF16)
        vc = _dot(xmb, bv_ref[c]).astype(BF16)
        q_ref[:, sl] = qc
        v_ref[:, sl] = vc
        kt_ref[sl, :] = _dot_nt(bkt_ref[c], xcb).astype(BF16)
        g = g + _dot(qc, wg_ref[0, sl, :]) + _dot(kc, wg_ref[1, sl, :]) + _dot(vc, wg_ref[2, sl, :])
    g = g + bg_ref[...]

    ls = _log_sigmoid(g)
    row = lax.broadcasted_iota(jnp.int32, (L, LANES), 0)
    pre, suf = ls, ls
    s = 1
    while s < L:
        pre = pre + jnp.where(row >= s, pltpu.roll(pre, s, 0), 0.0)
        suf = suf + jnp.where(row < L - s, pltpu.roll(suf, L - s, 0), 0.0)
        s *= 2
    lane = lax.broadcasted_iota(jnp.int32, (L, LANES), 1)
    H = MLSTM_HEADS
    out = jnp.where((lane >= H) & (lane < 2 * H), pre, jnp.where((lane >= 3 * H) & (lane < 4 * H), suf, g))
    gcol_ref[...] = out[:, 0:GATE_W]
    grow_ref[...] = out.T[0:GATE_W, :]


def mlstm_pre(xm, conv_w, conv_b, bq, bk, bkt, bv, wg, bg, n_prompt, sample_seq):
    T = xm.shape[0]
    L = MLSTM_CHUNK
    n_tiles = T // L
    r8 = L // SUBLANES
    n_blk8 = T // SUBLANES
    C = MLSTM_INNER
    kern = functools.partial(_mlstm_pre_kernel, n_prompt_tiles=n_prompt // L, tiles_per_seq=sample_seq // L)
    const3 = lambda t: (0, 0, 0)
    return pl.pallas_call(
        kern,
        grid=(n_tiles,),
        in_specs=[
            pl.BlockSpec((L, C), lambda t: (t, 0)),
            pl.BlockSpec((SUBLANES, C), lambda t: (jnp.maximum(t * r8 - 1, 0), 0)),
            pl.BlockSpec((SUBLANES, C), lambda t: (jnp.minimum((t + 1) * r8, n_blk8 - 1), 0)),
            pl.BlockSpec((MLSTM_CONV_K, C), lambda t: (0, 0)),
            pl.BlockSpec((1, C), lambda t: (0, 0)),
            pl.BlockSpec(bq.shape, const3),
            pl.BlockSpec(bk.shape, const3),
            pl.BlockSpec(bkt.shape, const3),
            pl.BlockSpec(bv.shape, const3),
            pl.BlockSpec(wg.shape, const3),
            pl.BlockSpec((1, LANES), lambda t: (0, 0)),
        ],
        out_specs=[
            pl.BlockSpec((L, C), lambda t: (t, 0)),
            pl.BlockSpec((L, C), lambda t: (t, 0)),
            pl.BlockSpec((None, C, L), lambda t: (t, 0, 0)),
            pl.BlockSpec((L, C), lambda t: (t, 0)),
            pl.BlockSpec((L, GATE_W), lambda t: (t, 0)),
            pl.BlockSpec((None, GATE_W, L), lambda t: (t, 0, 0)),
        ],
        out_shape=[
            jax.ShapeDtypeStruct((T, C), BF16),
            jax.ShapeDtypeStruct((T, C), BF16),
            jax.ShapeDtypeStruct((n_tiles, C, L), BF16),
            jax.ShapeDtypeStruct((T, C), BF16),
            jax.ShapeDtypeStruct((T, GATE_W), F32),
            jax.ShapeDtypeStruct((n_tiles, GATE_W, L), F32),
        ],
        compiler_params=_params("parallel"),
        name="mlstm_pre",
    )(xm, xm, xm, conv_w, conv_b.reshape(1, C), bq, bk, bkt, bv, wg, bg)


def _gate_cols(gcol, h):
    lane = lax.broadcasted_iota(jnp.int32, gcol.shape, 1)
    pick = lambda idx: jnp.sum(jnp.where(lane == idx, gcol, 0.0), axis=-1, keepdims=True)
    return pick(MLSTM_HEADS + h), pick(3 * MLSTM_HEADS + h)


def _chunk_dir(qk, q, kt, v, b_col, b_row, i_row, tot, causal_mask, m_prev, C_prev, n_prev):
    ar = i_row - b_row
    d = jnp.where(causal_mask, b_col + ar, -jnp.inf)
    inter = b_col + m_prev
    mt = jnp.maximum(jnp.max(d, axis=-1, keepdims=True), inter)
    s = qk * jnp.exp(d - mt)
    num = _dot(s.astype(BF16), v)
    den = jnp.sum(s, axis=-1, keepdims=True)
    if C_prev is not None:
        w_inter = jnp.exp(inter - mt)
        num = num + w_inter * _dot(q, C_prev.astype(BF16))
        den = den + w_inter * jnp.sum(q.astype(F32) * n_prev, axis=-1, keepdims=True)
    h = num / jnp.maximum(jnp.abs(den), jnp.exp(-mt))

    g = tot + ar
    m_new = jnp.maximum(tot + m_prev, jnp.max(g, axis=-1, keepdims=True))
    w = jnp.exp(g - m_new)
    C_new = _dot((kt.astype(F32) * w).astype(BF16), v)
    w_hi = w.astype(BF16).astype(F32)
    rid = lax.broadcasted_iota(jnp.int32, (SUBLANES, w.shape[-1]), 0)
    w8 = jnp.where(rid == 0, w_hi, jnp.where(rid == 1, w - w_hi, 0.0)).astype(BF16)
    n8 = _dot_nt(w8, kt)
    n_new = n8[0:1] + n8[1:2]
    if C_prev is not None:
        decay = jnp.exp(tot + m_prev - m_new)
        C_new = decay * C_prev + C_new
        n_new = decay * n_prev + n_new
    return h, C_new, n_new, m_new


def _head_layernorm(h, g):
    mu = jnp.mean(h, axis=-1, keepdims=True)
    hc = h - mu
    var = jnp.mean(hc * hc, axis=-1, keepdims=True)
    return hc * lax.rsqrt(var + NORM_EPS) * g


def _chunk_masks(L):
    t_idx = lax.broadcasted_iota(jnp.int32, (L, L), 0)
    s_idx = lax.broadcasted_iota(jnp.int32, (L, L), 1)
    return s_idx <= t_idx, s_idx >= t_idx


def _mlstm_prompt_kernel(q_ref, kt_ref, v_ref, gcol_ref, grow_ref, mhg_ref,
                         hn_ref, C_ref, n_ref, m_ref):
    L = MLSTM_CHUNK
    H = MLSTM_HEADS
    h = pl.program_id(1)
    q, kt, v = q_ref[...], kt_ref[...], v_ref[...]
    qk = _dot(q, kt)
    fmask, bmask = _chunk_masks(L)
    bf_col, bb_col = _gate_cols(gcol_ref[...], h)
    zero = jnp.zeros((1, 1), F32)
    row = lambda idx: grow_ref[pl.ds(idx, 1), :]
    bf_row, bb_row = row(H + h), row(3 * H + h)
    hf, Cf, nf, mf = _chunk_dir(qk, q, kt, v, bf_col, bf_row, row(h), bf_row[:, L - 1:L], fmask, zero, None, None)
    hb, Cb, nb, mb = _chunk_dir(qk, q, kt, v, bb_col, bb_row, row(2 * H + h), bb_row[:, 0:1], bmask, zero, None, None)
    hn_ref[...] = _head_layernorm(hf + hb, mhg_ref[...]).astype(hn_ref.dtype)
    C_ref[0] = Cf
    C_ref[1] = Cb
    n_ref[0] = nf
    n_ref[1] = nb
    lane = lax.broadcasted_iota(jnp.int32, (1, LANES), 1)
    m_ref[...] = jnp.where(lane == 0, mf, jnp.where(lane == 1, mb, 0.0))


def mlstm_prompt(q, kt, v, gcol, grow, mh_g, batch):
    L, H, DH = MLSTM_CHUNK, MLSTM_HEADS, MLSTM_HEAD_DIM
    return pl.pallas_call(
        _mlstm_prompt_kernel,
        grid=(batch, H),
        in_specs=[
            pl.BlockSpec((L, DH), lambda b, h: (b, h)),
            pl.BlockSpec((None, DH, L), lambda b, h: (b, h, 0)),
            pl.BlockSpec((L, DH), lambda b, h: (b, h)),
            pl.BlockSpec((L, GATE_W), lambda b, h: (b, 0)),
            pl.BlockSpec((None, GATE_W, L), lambda b, h: (b, 0, 0)),
            pl.BlockSpec((1, DH), lambda b, h: (0, h)),
        ],
        out_specs=[
            pl.BlockSpec((L, DH), lambda b, h: (b, h)),
            pl.BlockSpec((None, None, 2, None, DH, DH), lambda b, h: (b, 0, 0, h, 0, 0)),
            pl.BlockSpec((None, 2, None, 1, DH), lambda b, h: (b, 0, h, 0, 0)),
            pl.BlockSpec((None, None, 1, LANES), lambda b, h: (b, h, 0, 0)),
        ],
        out_shape=[
            jax.ShapeDtypeStruct((batch * L, MLSTM_INNER), BF16),
            jax.ShapeDtypeStruct((batch, 1, 2, H, DH, DH), F32),
            jax.ShapeDtypeStruct((batch, 2, H, 1, DH), F32),
            jax.ShapeDtypeStruct((batch, H, 1, LANES), F32),
        ],
        compiler_params=_params("parallel", "parallel"),
        name="mlstm_prompt",
    )(q, kt, v, gcol, grow, mh_g.reshape(1, MLSTM_INNER))


def _mlstm_sample_kernel(q_ref, kt_ref, v_ref, gcol_ref, grow_ref, mhg_ref, C0_ref, n0_ref, m0_ref,
                         hn_ref, hf_ref, hb_ref, C_ref, n_ref, m_ref, *, n_chunks):
    L = MLSTM_CHUNK
    H = MLSTM_HEADS
    h = pl.program_id(1)
    fmask, bmask = _chunk_masks(L)
    C_ref[...] = C0_ref[...]
    n_ref[...] = n0_ref[...]
    m_ref[...] = m0_ref[...]

    def one(c, direction):
        rows = pl.ds(pl.multiple_of(c * L, L), L)
        q, kt, v = q_ref[rows, :], kt_ref[c], v_ref[rows, :]
        qk = _dot(q, kt)
        gcol = gcol_ref[rows, :]
        b_col = _gate_cols(gcol, h)[direction]
        grow = lambda idx: grow_ref[c, pl.ds(idx, 1), :]
        b_row = grow((2 * direction + 1) * H + h)
        i_row = grow(2 * direction * H + h)
        tot = b_row[:, L - 1:L] if direction == 0 else b_row[:, 0:1]
        mask = fmask if direction == 0 else bmask
        m_prev = m_ref[direction][:, 0:1]
        hh, C_new, n_new, m_new = _chunk_dir(qk, q, kt, v, b_col, b_row, i_row, tot, mask, m_prev,
                                             C_ref[direction], n_ref[direction])
        C_ref[direction] = C_new
        n_ref[direction] = n_new
        m_ref[direction] = jnp.broadcast_to(m_new, (1, LANES))
        (hf_ref if direction == 0 else hb_ref)[rows, :] = hh

    def body(i, carry):
        one(i, 0)
        one(n_chunks - 1 - i, 1)
        return carry

    lax.fori_loop(0, n_chunks, body, 0)

    def norm(c, carry):
        rows = pl.ds(pl.multiple_of(c * L, L), L)
        hn_ref[rows, :] = _head_layernorm(hf_ref[rows, :] + hb_ref[rows, :], mhg_ref[...]).astype(hn_ref.dtype)
        return carry

    lax.fori_loop(0, n_chunks, norm, 0)


def mlstm_sample(q, kt, v, gcol, grow, mh_g, C0, n0, m0, n_prompt, batch, seq):
    L, H, DH = MLSTM_CHUNK, MLSTM_HEADS, MLSTM_HEAD_DIM
    nc = seq // L
    row_off = n_prompt // seq
    tile_off = n_prompt // L // nc
    return pl.pallas_call(
        functools.partial(_mlstm_sample_kernel, n_chunks=nc),
        grid=(batch, H),
        in_specs=[
            pl.BlockSpec((seq, DH), lambda b, h: (b + row_off, h)),
            pl.BlockSpec((nc, DH, L), lambda b, h: (b + tile_off, h, 0)),
            pl.BlockSpec((seq, DH), lambda b, h: (b + row_off, h)),
            pl.BlockSpec((seq, GATE_W), lambda b, h: (b + row_off, 0)),
            pl.BlockSpec((nc, GATE_W, L), lambda b, h: (b + tile_off, 0, 0)),
            pl.BlockSpec((1, DH), lambda b, h: (0, h)),
            pl.BlockSpec((None, 2, None, DH, DH), lambda b, h: (b, 0, h, 0, 0)),
            pl.BlockSpec((None, 2, None, 1, DH), lambda b, h: (b, 0, h, 0, 0)),
            pl.BlockSpec((None, None, 2, 1, LANES), lambda b, h: (b, h, 0, 0, 0)),
        ],
        out_specs=pl.BlockSpec((seq, DH), lambda b, h: (b, h)),
        out_shape=jax.ShapeDtypeStruct((batch * seq, MLSTM_INNER), BF16),
        scratch_shapes=[
            pltpu.VMEM((seq, DH), F32),
            pltpu.VMEM((seq, DH), F32),
            pltpu.VMEM((2, DH, DH), F32),
            pltpu.VMEM((2, 1, DH), F32),
            pltpu.VMEM((2, 1, LANES), F32),
        ],
        compiler_params=_params("parallel", "parallel"),
        name="mlstm_sample",
    )(q, kt, v, gcol, grow, mh_g.reshape(1, MLSTM_INNER), C0, n0, m0)


def _rope_tables(seq):
    t = jnp.arange(seq)
    row = (t // GRID_W).astype(F32)
    col = (t % GRID_W).astype(F32)
    n_freq = HEAD_DIM // 4
    inv = ROPE_THETA ** (-jnp.arange(n_freq, dtype=F32) / n_freq)
    ang = jnp.concatenate([row[:, None] * inv, col[:, None] * inv], axis=-1)
    cos, sin = jnp.cos(ang), jnp.sin(ang)
    reps = PROJ_N_TILE // HEAD_DIM
    cos_t = jnp.tile(jnp.concatenate([cos, cos], axis=-1), (1, reps))
    sin_t = jnp.tile(jnp.concatenate([-sin, sin], axis=-1), (1, reps))
    ident = jnp.ones((TOKEN_TILE, PROJ_N_TILE), F32)
    return (jnp.concatenate([cos_t, ident], axis=0), jnp.concatenate([sin_t, 0.0 * ident], axis=0))


def _block_diag(w):
    per = LANES // MLSTM_QKV_BLOCK
    w4 = w.reshape(-1, per, MLSTM_QKV_BLOCK, MLSTM_QKV_BLOCK)
    eye = jnp.eye(per, dtype=w.dtype)
    return jnp.einsum('kacd,ab->kacbd', w4, eye).reshape(-1, LANES, LANES)


def _diff_lambda_init(layer_idx):
    return 0.8 - 0.6 * math.exp(-0.3 * layer_idx)


def kernel(x_prompt, x_sample, cache_gqa_k, cache_gqa_v, cache_diff_k, cache_diff_v, state_mlstm_C, state_mlstm_n, state_mlstm_m, c, c_ctx, norm_g, w_mod, b_mod, gqa_w_in, gqa_q_norm_g, gqa_k_norm_g, gqa_w_out, diff_w_in, diff_lambda_q1, diff_lambda_k1, diff_lambda_q2, diff_lambda_k2, diff_subln_g, diff_w_out, mlstm_w_in, mlstm_conv_w, mlstm_conv_b, mlstm_w_q, mlstm_w_k, mlstm_w_v, mlstm_w_gate_f, mlstm_b_gate_f, mlstm_w_gate_b, mlstm_b_gate_b, mlstm_mh_norm_g, mlstm_skip, mlstm_w_out, final_norm_g):
    Bp, Sp, D = x_prompt.shape
    Bs, Ss, _ = x_sample.shape
    Tp, Ts = Bp * Sp, Bs * Ss
    assert Sp == MLSTM_CHUNK and Ss % TOKEN_TILE == 0 and Tp % TOKEN_TILE == 0
    H, DH = MLSTM_HEADS, MLSTM_HEAD_DIM
    past = cache_gqa_k.shape[2]

    x = jnp.concatenate([x_prompt.reshape(Tp, D), x_sample.reshape(Ts, D)], axis=0)
    cond = jnp.zeros((SUBLANES, D), F32).at[0].set(c_ctx).at[1:1 + Bs].set(c)
    mod4 = adaln_all(cond, w_mod, b_mod).reshape(DEPTH, SUBLANES, 1, 3 * D)
    rope_tabs = _rope_tables(Ss)
    gi = jnp.arange(PROJ_N_TILE) // HEAD_DIM
    gmat = (gi[:, None] == gi[None, :]).astype(BF16)
    tile_g = lambda g: jnp.tile(g, PROJ_N_TILE // HEAD_DIM).reshape(1, PROJ_N_TILE)
    ctx_gqa = (cache_gqa_k.reshape(Bs, -1, past, GQA_KV_W), cache_gqa_v.reshape(Bs, -1, past, GQA_KV_W))
    ctx_diff = (cache_diff_k.reshape(Bs, -1, past, DIFF_QK_W), cache_diff_v.reshape(Bs, -1, past, DIFF_BRANCH))

    gqa_k_list, gqa_v_list, diff_k_list, diff_v_list = [], [], [], []
    mC_list, mn_list, mm_list = [], [], []
    for i in range(DEPTH):
        kind, j = i % N_MIXERS, i // N_MIXERS
        common = dict(n_prompt=Tp, sample_seq=Ss)
        if kind == 0:
            segs = ((GQA_Q_W, "q_norm", BF16), (GQA_KV_W, "k_norm", "split"), (GQA_KV_W, "plain", "split"),
                    (GQA_Q_W, "plain", BF16))
            q, kp, ks, vp, vs, z = proj_in(
                x, norm_g[i], mod4, i, gqa_w_in[j].astype(BF16), segs, rope_tabs=rope_tabs,
                head_norm=(gmat, tile_g(gqa_q_norm_g[j]), tile_g(gqa_k_norm_g[j])), **common)
            scr = [(GQA_KV_HEADS, 2 * LANES)] * 2
            kern = _gqa_attn_kernel
            o_p = attention(kern, scr, q, kp, vp, Bp, Sp, 0)
            o_s = attention(kern, scr, q, ks, vs, Bs, Ss, Tp, ctx=(*ctx_gqa, j))
            x = proj_out(o_p, o_s, z, x, mod4, i, gqa_w_out[j].astype(BF16), **common)
            gqa_k_list.append(kp.reshape(Bp, 1, Sp, GQA_KV_HEADS, HEAD_DIM))
            gqa_v_list.append(vp.reshape(Bp, 1, Sp, GQA_KV_HEADS, HEAD_DIM))
        elif kind == 1:
            lam_init = _diff_lambda_init(i)
            segs = ((DIFF_QK_W, "q", BF16), (DIFF_QK_W, "k", "split"), (DIFF_BRANCH, "plain", "split"),
                    (DIFF_BRANCH, "plain", BF16))
            q, kp, ks, vp, vs, z = proj_in(x, norm_g[i], mod4, i, diff_w_in[j].astype(BF16), segs,
                                           rope_tabs=rope_tabs, **common)
            lam_vecs = jnp.stack([diff_lambda_q1[j], diff_lambda_k1[j], diff_lambda_q2[j], diff_lambda_k2[j]])
            extra = (lam_vecs, diff_subln_g[j].reshape(1, DIFF_V_DIM))
            scr = [(DIFF_HEADS, 2 * LANES)]
            kern = functools.partial(_diff_attn_kernel, lam_init=lam_init)
            o_p = attention(kern, scr, q, kp, vp, Bp, Sp, 0, extra=extra)
            o_s = attention(kern, scr, q, ks, vs, Bs, Ss, Tp, ctx=(*ctx_diff, j), extra=extra)
            x = proj_out(o_p, o_s, z, x, mod4, i, diff_w_out[j].astype(BF16), **common)
            diff_k_list.append(kp.reshape(Bp, 1, Sp, DIFF_HEADS, 2, HEAD_DIM))
            diff_v_list.append(vp.reshape(Bp, 1, Sp, DIFF_HEADS, DIFF_V_DIM))
        else:
            segs = ((MLSTM_INNER, "plain", F32), (MLSTM_INNER, "plain", BF16))
            xm, z = proj_in(x, norm_g[i], mod4, i, mlstm_w_in[j].astype(BF16), segs, **common)
            bq = _block_diag(mlstm_w_q[j]).astype(BF16)
            bk = _block_diag(mlstm_w_k[j])
            bkt = (bk * (DH ** -0.5)).transpose(0, 2, 1).astype(BF16)
            bv = _block_diag(mlstm_w_v[j]).astype(BF16)
            wg = jnp.concatenate([mlstm_w_gate_f[j][:, :H], mlstm_w_gate_f[j][:, H:],
                                  mlstm_w_gate_b[j][:, :H], mlstm_w_gate_b[j][:, H:]], axis=-1)
            wg = jnp.pad(wg, ((0, 0), (0, LANES - GATE_W))).astype(BF16).reshape(3, MLSTM_INNER, LANES)
            bg = jnp.pad(jnp.concatenate([mlstm_b_gate_f[j], mlstm_b_gate_b[j]]), (0, LANES - GATE_W)).reshape(1, LANES)
            xc, q, kt, v, gcol, grow = mlstm_pre(xm, mlstm_conv_w[j], mlstm_conv_b[j], bq, bk.astype(BF16), bkt, bv,
                                                 wg, bg, **common)
            hn_p, Cp, np_, mp_ = mlstm_prompt(q, kt, v, gcol, grow, mlstm_mh_norm_g[j], Bp)
            C0 = state_mlstm_C[:, j]
            n0 = state_mlstm_n[:, j].reshape(Bs, 2, H, 1, DH)
            m0 = jnp.broadcast_to(state_mlstm_m[:, j].transpose(0, 2, 1)[..., None, None], (Bs, H, 2, 1, LANES))
            hn_s = mlstm_sample(q, kt, v, gcol, grow, mlstm_mh_norm_g[j], C0, n0, m0, Tp, Bs, Ss)
            x = proj_out(hn_p, hn_s, z, x, mod4, i, mlstm_w_out[j].astype(BF16), xc=xc, skip=mlstm_skip[j], **common)
            mC_list.append(Cp)
            mn_list.append(np_.reshape(Bp, 1, 2, H, DH))
            mm_list.append(mp_[:, :, 0, 0:2].transpose(0, 2, 1)[:, None])

    cat1 = lambda parts: parts[0] if len(parts) == 1 else jnp.concatenate(parts, axis=1)
    y_prompt = final_norm(x, final_norm_g, 0, Tp).reshape(Bp, Sp, D)
    y_sample = final_norm(x, final_norm_g, Tp, Ts).reshape(Bs, Ss, D)
    return (y_prompt, y_sample, cat1(gqa_k_list), cat1(gqa_v_list), cat1(diff_k_list), cat1(diff_v_list),
            cat1(mC_list), cat1(mn_list), cat1(mm_list))
```

```python
import functools
import math

import jax
import jax.numpy as jnp
from jax import lax
from jax.experimental import pallas as pl
from jax.experimental.pallas import tpu as pltpu

D_MODEL = 1024
DEPTH = 4
GRID_W = 64
N_MIXERS = 3
HEAD_DIM = 64
ROPE_THETA = 10000.0
NORM_EPS = 1e-6

GQA_HEADS = 16
GQA_KV_HEADS = 4
GQA_GROUP = GQA_HEADS // GQA_KV_HEADS
GQA_Q_W = GQA_HEADS * HEAD_DIM
GQA_KV_W = GQA_KV_HEADS * HEAD_DIM

DIFF_HEADS = 8
DIFF_QK_W = DIFF_HEADS * 2 * HEAD_DIM
DIFF_V_DIM = 2 * HEAD_DIM
DIFF_BRANCH = DIFF_HEADS * DIFF_V_DIM

MLSTM_INNER = 2 * D_MODEL
MLSTM_HEADS = 4
MLSTM_HEAD_DIM = MLSTM_INNER // MLSTM_HEADS
MLSTM_QKV_BLOCK = 4
MLSTM_CONV_K = 4

LANES = 128
SUBLANES = 8
VMEM_LIMIT_BYTES = 56 * 1024 * 1024

TOKEN_TILE = 512
PROJ_N_TILE = 256
ATTN_Q_TILE = 256
MLSTM_CHUNK = 256
GATE_W = 4 * MLSTM_HEADS

F32 = jnp.float32
BF16 = jnp.bfloat16


def _params(*semantics):
    return pltpu.CompilerParams(dimension_semantics=semantics, vmem_limit_bytes=VMEM_LIMIT_BYTES)


def _silu(x):
    return x * (1.0 / (1.0 + jnp.exp(-x)))


def _dot(a, b):
    return jnp.dot(a, b, preferred_element_type=F32)


def _dot_nt(a, b):
    return lax.dot_general(a, b, (((1,), (1,)), ((), ())), preferred_element_type=F32)


class TokenLayout:
    def __init__(self, n_prompt, seq, tile):
        self.prompt_tiles = n_prompt // tile
        self.tiles_per_seq = seq // tile

    def is_prompt(self, m):
        return m < self.prompt_tiles

    def cond_row(self, m):
        return jnp.where(m < self.prompt_tiles, 0, 1 + (m - self.prompt_tiles) // self.tiles_per_seq)

    def prompt_block(self, m):
        return jnp.minimum(m, self.prompt_tiles - 1)

    def sample_block(self, m):
        return jnp.maximum(m - self.prompt_tiles, 0)


def _mod_kernel(cond_ref, w_ref, b_ref, o_ref):
    a = _silu(cond_ref[...]).astype(BF16)
    o_ref[...] = _dot(a, w_ref[...].astype(BF16)) + b_ref[...]


def adaln_all(cond, w_mod, b_mod):
    tn = 1024
    return pl.pallas_call(
        _mod_kernel,
        grid=(DEPTH, 3 * D_MODEL // tn),
        in_specs=[
            pl.BlockSpec((SUBLANES, D_MODEL), lambda i, n: (0, 0)),
            pl.BlockSpec((None, D_MODEL, tn), lambda i, n: (i, 0, n)),
            pl.BlockSpec((None, 1, tn), lambda i, n: (i, 0, n)),
        ],
        out_specs=pl.BlockSpec((None, SUBLANES, tn), lambda i, n: (i, 0, n)),
        out_shape=jax.ShapeDtypeStruct((DEPTH, SUBLANES, 3 * D_MODEL), F32),
        compiler_params=_params("parallel", "parallel"),
        name="adaln_mod",
    )(cond, w_mod, b_mod.reshape(DEPTH, 1, 3 * D_MODEL))


def _swap_half_heads(y):
    w = y.shape[-1]
    lane = lax.broadcasted_iota(jnp.int32, y.shape, 1)
    upper = (lane & (HEAD_DIM // 2)) != 0
    return jnp.where(upper, pltpu.roll(y, HEAD_DIM // 2, 1), pltpu.roll(y, w - HEAD_DIM // 2, 1))


def _x_specs(x, lay, tm):
    if isinstance(x, tuple):
        return [pl.BlockSpec((tm, D_MODEL), lambda m: (lay.prompt_block(m), 0)),
                pl.BlockSpec((tm, D_MODEL), lambda m: (lay.sample_block(m), 0))], list(x)
    return [pl.BlockSpec((tm, D_MODEL), lambda m: (m, 0))], [x]


def _two_paths(is_prompt, path):
    pl.when(is_prompt)(functools.partial(path, True))
    pl.when(jnp.logical_not(is_prompt))(functools.partial(path, False))


def _proj_in_kernel(*refs, segs, layout, has_norm, has_rope, x_pair):
    it = iter(refs)
    x_refs = (next(it), next(it)) if x_pair else (next(it),) * 2
    g_ref, mod_ref, w_ref = next(it), next(it), next(it)
    if has_rope:
        cos_ref, sin_ref = next(it), next(it)
    if has_norm:
        gmat_ref, qg_ref, kg_ref = next(it), next(it), next(it)
    out_refs = []
    for _, _, kind, _ in segs:
        pair = (next(it), next(it)) if kind != "full" else (next(it),) * 2
        out_refs.append(pair + ((next(it),) if kind == "cache" else ()))
    cw = PROJ_N_TILE

    def store_cache(ref, trail, acc, col0):
        seq_len, hd = ref.shape[1], trail[-1]
        for s in range(ref.shape[0]):
            for p in range(cw // hd):
                idx, rem = [], (col0 + p * hd) // hd
                for n in reversed(trail[:-1]):
                    idx.insert(0, rem % n)
                    rem //= n
                ref[(s, slice(None), *idx, slice(None))] = acc[s * seq_len:(s + 1) * seq_len, p * hd:(p + 1) * hd]

    def path(prompt):
        x = x_refs[0 if prompt else 1][...]
        y = x * lax.rsqrt(jnp.mean(x * x, axis=-1, keepdims=True) + NORM_EPS) * g_ref[...]
        shift = mod_ref[:, 0:D_MODEL]
        scale = mod_ref[:, D_MODEL:2 * D_MODEL]
        h = (y * (1.0 + scale) + shift).astype(BF16)
        col = 0
        for (width, mode, kind, trail), o_refs in zip(segs, out_refs):
            o_ref = o_refs[0 if prompt else 1]
            for c in range(width // cw):
                acc = _dot(h, w_ref[:, col:col + cw])
                col += cw
                if mode in ("q_norm", "k_norm"):
                    ss = _dot((acc * acc).astype(BF16), gmat_ref[...])
                    gvec = qg_ref[...] if mode == "q_norm" else kg_ref[...]
                    acc = acc * lax.rsqrt(ss * (1.0 / HEAD_DIM) + NORM_EPS) * gvec
                if has_rope and mode != "plain" and not prompt:
                    acc = acc * cos_ref[...] + _swap_half_heads(acc) * sin_ref[...]
                if mode in ("q_norm", "q"):
                    acc = acc * (HEAD_DIM ** -0.5)
                o_ref[:, c * cw:(c + 1) * cw] = acc.astype(o_ref.dtype)
                if kind == "cache" and prompt:
                    store_cache(o_refs[2], trail, acc, c * cw)

    _two_paths(layout.is_prompt(pl.program_id(0)), path)


def proj_in(x, norm_g, mod4, layer, w, segs, n_prompt, sample_seq, rope_tabs=None, head_norm=None):
    tm, cw = TOKEN_TILE, PROJ_N_TILE
    lay = TokenLayout(n_prompt, sample_seq, tm)
    T = x[0].shape[0] + x[1].shape[0] if isinstance(x, tuple) else x.shape[0]
    N = w.shape[1]
    assert N == sum(s[0] for s in segs) and all(s[0] % cw == 0 for s in segs)

    in_specs, args = _x_specs(x, lay, tm)
    in_specs += [
        pl.BlockSpec((1, D_MODEL), lambda m: (0, 0)),
        pl.BlockSpec((None, None, 1, 3 * D_MODEL), lambda m: (layer, lay.cond_row(m), 0, 0)),
        pl.BlockSpec((D_MODEL, N), lambda m: (0, 0)),
    ]
    args += [norm_g.reshape(1, D_MODEL), mod4, w]
    if rope_tabs is not None:
        def pos_block(m):
            return (lay.sample_block(m) % lay.tiles_per_seq, 0)
        in_specs += [pl.BlockSpec((tm, cw), pos_block), pl.BlockSpec((tm, cw), pos_block)]
        args += list(rope_tabs)
    if head_norm is not None:
        in_specs += [pl.BlockSpec((cw, cw), lambda m: (0, 0)),
                     pl.BlockSpec((1, cw), lambda m: (0, 0)),
                     pl.BlockSpec((1, cw), lambda m: (0, 0))]
        args += list(head_norm)

    out_specs, out_shape, kern_segs = [], [], []
    for width, mode, out in segs:
        if isinstance(out, (str, tuple)):
            cache = isinstance(out, tuple)
            kern_segs.append((width, mode, "cache" if cache else "split", out[1] if cache else None))
            out_specs += [pl.BlockSpec((tm, width), lambda m: (lay.prompt_block(m), 0)),
                          pl.BlockSpec((tm, width), lambda m: (lay.sample_block(m), 0))]
            out_shape += [jax.ShapeDtypeStruct((n_prompt, width), BF16 if cache else F32),
                          jax.ShapeDtypeStruct((T - n_prompt, width), BF16)]
            if cache:
                seq_len, trail = out
                zeros = (0,) * (2 + len(trail))
                out_specs.append(pl.BlockSpec((tm // seq_len, None, seq_len) + tuple(trail),
                                              lambda m, zeros=zeros: (lay.prompt_block(m),) + zeros))
                out_shape.append(jax.ShapeDtypeStruct((n_prompt // seq_len, 1, seq_len) + tuple(trail), F32))
        else:
            kern_segs.append((width, mode, "full", None))
            out_specs.append(pl.BlockSpec((tm, width), lambda m: (m, 0)))
            out_shape.append(jax.ShapeDtypeStruct((T, width), out))

    kern = functools.partial(_proj_in_kernel, segs=tuple(kern_segs), layout=lay, x_pair=isinstance(x, tuple),
                             has_norm=head_norm is not None, has_rope=rope_tabs is not None)
    return pl.pallas_call(
        kern,
        grid=(T // tm,),
        in_specs=in_specs,
        out_specs=out_specs,
        out_shape=out_shape,
        compiler_params=_params("arbitrary"),
        name="proj_in",
    )(*args)


def _proj_out_kernel(*refs, layout, with_skip, x_pair, final):
    it = iter(refs)
    o_refs = (next(it), next(it))
    if with_skip:
        xc_ref, skip_ref = next(it), next(it)
    z_ref = next(it)
    x_refs = (next(it), next(it)) if x_pair else (next(it),) * 2
    mod_ref, w_ref = next(it), next(it)
    if final:
        fg_ref = next(it)
    out_refs = (next(it), next(it)) if final else (next(it),) * 2

    def path(prompt):
        sel = 0 if prompt else 1
        a = o_refs[sel][...].astype(F32)
        if with_skip:
            a = a + skip_ref[...] * xc_ref[...].astype(F32)
        a = (a * _silu(z_ref[...].astype(F32))).astype(BF16)
        gate = mod_ref[:, 2 * D_MODEL:3 * D_MODEL]
        x = x_refs[sel][...] + gate * _dot(a, w_ref[...])
        if final:
            x = x * lax.rsqrt(jnp.mean(x * x, axis=-1, keepdims=True) + NORM_EPS) * fg_ref[...]
        out_refs[sel][...] = x

    _two_paths(layout.is_prompt(pl.program_id(0)), path)


def proj_out(o_prompt, o_sample, z, x, mod4, layer, w_out, n_prompt, sample_seq, xc=None, skip=None,
             final_g=None):
    T = z.shape[0]
    K = o_prompt.shape[1]
    tm = TOKEN_TILE
    lay = TokenLayout(n_prompt, sample_seq, tm)
    row = lambda m: (m, 0)
    prompt_row = lambda m: (lay.prompt_block(m), 0)
    sample_row = lambda m: (lay.sample_block(m), 0)
    in_specs = [pl.BlockSpec((tm, K), prompt_row), pl.BlockSpec((tm, K), sample_row)]
    args = [o_prompt, o_sample]
    if xc is not None:
        in_specs += [pl.BlockSpec((tm, K), row), pl.BlockSpec((1, K), lambda m: (0, 0))]
        args += [xc, skip.reshape(1, K)]
    in_specs.append(pl.BlockSpec((tm, K), row))
    args.append(z)
    x_specs, x_args = _x_specs(x, lay, tm)
    in_specs += x_specs + [
        pl.BlockSpec((None, None, 1, 3 * D_MODEL), lambda m: (layer, lay.cond_row(m), 0, 0)),
        pl.BlockSpec((K, D_MODEL), lambda m: (0, 0)),
    ]
    args += x_args + [mod4, w_out]
    if final_g is not None:
        in_specs.append(pl.BlockSpec((1, D_MODEL), lambda m: (0, 0)))
        args.append(final_g.reshape(1, D_MODEL))
        out_specs = [pl.BlockSpec((tm, D_MODEL), prompt_row), pl.BlockSpec((tm, D_MODEL), sample_row)]
        out_shape = [jax.ShapeDtypeStruct((n_prompt, D_MODEL), F32),
                     jax.ShapeDtypeStruct((T - n_prompt, D_MODEL), F32)]
    else:
        out_specs = pl.BlockSpec((tm, D_MODEL), row)
        out_shape = jax.ShapeDtypeStruct((T, D_MODEL), F32)
    kern = functools.partial(_proj_out_kernel, layout=lay, with_skip=xc is not None,
                             x_pair=isinstance(x, tuple), final=final_g is not None)
    return pl.pallas_call(
        kern,
        grid=(T // tm,),
        in_specs=in_specs,
        out_specs=out_specs,
        out_shape=out_shape,
        compiler_params=_params("arbitrary"),
        name="proj_out",
    )(*args)


def _low_half(shape):
    return lax.broadcasted_iota(jnp.int32, shape, 1) < HEAD_DIM


def _probs(q, k_parts):
    s = [_dot_nt(q, k) for k in k_parts]
    m = functools.reduce(jnp.maximum, [jnp.max(x, axis=-1, keepdims=True) for x in s])
    return [jnp.exp(x - m).astype(BF16) for x in s]


def _pv(p_parts, v_ref, head, rows):
    acc = None
    for p, r in zip(p_parts, rows):
        t = _dot(p, v_ref[head, r, :])
        acc = t if acc is None else acc + t
    return acc


def _row_parts(k_ref, kc_ref):
    n_new = k_ref.shape[0]
    rows = [slice(0, n_new)]
    if kc_ref is not None:
        rows.append(slice(n_new, n_new + kc_ref.shape[0]))
    return rows


def _gqa_attn_kernel(*refs, has_ctx):
    if has_ctx:
        q_ref, k_ref, v_ref, kc_ref, vc_ref, o_ref, ve_ref, vo_ref = refs
    else:
        q_ref, k_ref, v_ref, o_ref, ve_ref, vo_ref = refs
        kc_ref = vc_ref = None
    rows = _row_parts(k_ref, kc_ref)

    @pl.when(pl.program_id(1) == 0)
    def _():
        for src, r in zip([v_ref, vc_ref][:len(rows)], rows):
            for c in range(GQA_KV_HEADS):
                slab = src[:, (c // 2) * LANES:(c // 2 + 1) * LANES].astype(F32)
                low = _low_half(slab.shape)
                if c % 2 == 0:
                    ve = jnp.where(low, slab, 0.0)
                    vo = pltpu.roll(ve, HEAD_DIM, 1)
                else:
                    vo = jnp.where(low, 0.0, slab)
                    ve = pltpu.roll(vo, HEAD_DIM, 1)
                ones_e = jnp.where(low, 1.0, 0.0)
                ve_ref[c, r, 0:LANES] = ve.astype(BF16)
                ve_ref[c, r, LANES:2 * LANES] = ones_e.astype(BF16)
                vo_ref[c, r, 0:LANES] = vo.astype(BF16)
                vo_ref[c, r, LANES:2 * LANES] = (1.0 - ones_e).astype(BF16)

    k_srcs = [k_ref, kc_ref][:len(rows)]
    for pair in range(GQA_HEADS // 2):
        c = pair // (GQA_GROUP // 2)
        q = q_ref[:, pair * LANES:(pair + 1) * LANES].astype(F32)
        low = _low_half(q.shape)
        q_sw = pltpu.roll(q, HEAD_DIM, 1)
        if c % 2 == 0:
            qe, qo = jnp.where(low, q, 0.0), jnp.where(low, q_sw, 0.0)
        else:
            qe, qo = jnp.where(low, 0.0, q_sw), jnp.where(low, 0.0, q)
        k_parts = [s[:, (c // 2) * LANES:(c // 2 + 1) * LANES].astype(BF16) for s in k_srcs]
        acc = _pv(_probs(qe.astype(BF16), k_parts), ve_ref, c, rows)
        acc = acc + _pv(_probs(qo.astype(BF16), k_parts), vo_ref, c, rows)
        o_ref[:, pair * LANES:(pair + 1) * LANES] = (acc[:, 0:LANES] / acc[:, LANES:2 * LANES]).astype(o_ref.dtype)


def _diff_attn_kernel(*refs, has_ctx, lam_init):
    if has_ctx:
        q_ref, k_ref, v_ref, kc_ref, vc_ref, lam_ref, g_ref, o_ref, vx_ref = refs
    else:
        q_ref, k_ref, v_ref, lam_ref, g_ref, o_ref, vx_ref = refs
        kc_ref = vc_ref = None
    rows = _row_parts(k_ref, kc_ref)

    @pl.when(pl.program_id(1) == 0)
    def _():
        for src, r in zip([v_ref, vc_ref][:len(rows)], rows):
            for h in range(DIFF_HEADS):
                vx_ref[h, r, 0:LANES] = src[:, h * LANES:(h + 1) * LANES].astype(BF16)
                vx_ref[h, r, LANES:2 * LANES] = jnp.ones((r.stop - r.start, LANES), BF16)

    lv = lam_ref[...]
    s1 = jnp.sum(lv[0:1] * lv[1:2], axis=-1, keepdims=True)
    s2 = jnp.sum(lv[2:3] * lv[3:4], axis=-1, keepdims=True)
    lam = jnp.exp(s1) - jnp.exp(s2) + lam_init
    k_srcs = [k_ref, kc_ref][:len(rows)]
    for h in range(DIFF_HEADS):
        q = q_ref[:, h * LANES:(h + 1) * LANES].astype(F32)
        low = _low_half(q.shape)
        q0, q1 = jnp.where(low, q, 0.0).astype(BF16), jnp.where(low, 0.0, q).astype(BF16)
        k_parts = [s[:, h * LANES:(h + 1) * LANES].astype(BF16) for s in k_srcs]
        a0 = _pv(_probs(q0, k_parts), vx_ref, h, rows)
        a1 = _pv(_probs(q1, k_parts), vx_ref, h, rows)
        o = a0[:, 0:LANES] / a0[:, LANES:2 * LANES] - lam * (a1[:, 0:LANES] / a1[:, LANES:2 * LANES])
        o = o * lax.rsqrt(jnp.mean(o * o, axis=-1, keepdims=True) + NORM_EPS) * g_ref[...]
        o_ref[:, h * LANES:(h + 1) * LANES] = (o * (1.0 - lam_init)).astype(o_ref.dtype)


def attention(kernel, v_scratch, q, k, v, batch, sq, q_row0, ctx=None, extra=()):
    tq = ATTN_Q_TILE
    nq = sq // tq
    off = q_row0 // tq
    in_specs = [
        pl.BlockSpec((tq, q.shape[1]), lambda b, i: (b * nq + i + off, 0)),
        pl.BlockSpec((sq, k.shape[1]), lambda b, i: (b, 0)),
        pl.BlockSpec((sq, v.shape[1]), lambda b, i: (b, 0)),
    ]
    args = [q, k, v]
    sk = sq
    if ctx is not None:
        kc, vc, layer = ctx
        sk += kc.shape[2]
        in_specs += [pl.BlockSpec((None, None) + kc.shape[2:], lambda b, i: (b, layer, 0, 0)),
                     pl.BlockSpec((None, None) + vc.shape[2:], lambda b, i: (b, layer, 0, 0))]
        args += [kc, vc]
    for a in extra:
        in_specs.append(pl.BlockSpec(a.shape, lambda b, i: (0, 0)))
    args += list(extra)
    return pl.pallas_call(
        functools.partial(kernel, has_ctx=ctx is not None),
        grid=(batch, nq),
        in_specs=in_specs,
        out_specs=pl.BlockSpec((tq, D_MODEL), lambda b, i: (b * nq + i, 0)),
        out_shape=jax.ShapeDtypeStruct((batch * sq, D_MODEL), BF16),
        scratch_shapes=[pltpu.VMEM((heads, sk, width), BF16) for heads, width in v_scratch],
        compiler_params=_params("parallel", "arbitrary"),
        name="attention",
    )(*args)


def _log_sigmoid(x):
    return jnp.minimum(x, 0.0) - jnp.log1p(jnp.exp(-jnp.abs(x)))


def _mlstm_pre_kernel(cur_ref, prev_ref, next_ref, cw_ref, cb_ref, bq_ref, bk_ref, bkt_ref, bv_ref,
                      wg_ref, bg_ref, xc_ref, q_ref, kt_ref, v_ref, gcol_ref, grow_ref,
                      *, n_prompt_tiles, tiles_per_seq):
    L = MLSTM_CHUNK
    t = pl.program_id(0)
    st = jnp.maximum(t - n_prompt_tiles, 0) % tiles_per_seq
    first = (t < n_prompt_tiles) | (st == 0)
    last = (t < n_prompt_tiles) | (st == tiles_per_seq - 1)
    cur = cur_ref[...]
    prev = jnp.where(first, 0.0, prev_ref[...])
    nxt = jnp.where(last, 0.0, next_ref[...])
    ext = jnp.concatenate([prev, cur, nxt], axis=0)
    n_ext = L + 2 * SUBLANES
    pad_l = MLSTM_CONV_K // 2
    acc = cb_ref[...] + cw_ref[pad_l:pad_l + 1, :] * cur
    for j in range(MLSTM_CONV_K):
        if j == pad_l:
            continue
        sh = (pad_l - j) % n_ext
        acc = acc + cw_ref[j:j + 1, :] * pltpu.roll(ext, sh, 0)[SUBLANES:SUBLANES + L]
    xc = _silu(acc)
    xc_ref[...] = xc.astype(xc_ref.dtype)

    g = jnp.zeros((L, LANES), F32)
    for c in range(MLSTM_INNER // LANES):
        sl = slice(c * LANES, (c + 1) * LANES)
        xcb = xc[:, sl].astype(BF16)
        xmb = cur[:, sl].astype(BF16)
        qc = _dot(xcb, bq_ref[c]).astype(BF16)
        kc = _dot(xcb, bk_ref[c]).astype(BF16)
        vc = _dot(xmb, bv_ref[c]).astype(BF16)
        q_ref[:, sl] = qc
        v_ref[:, sl] = vc
        kt_ref[sl, :] = _dot_nt(bkt_ref[c], xcb).astype(BF16)
        g = g + _dot(qc, wg_ref[0, sl, :]) + _dot(kc, wg_ref[1, sl, :]) + _dot(vc, wg_ref[2, sl, :])
    g = g + bg_ref[...]

    ls = _log_sigmoid(g)
    row = lax.broadcasted_iota(jnp.int32, (L, LANES), 0)
    pre, suf = ls, ls
    s = 1
    while s < L:
        pre = pre + jnp.where(row >= s, pltpu.roll(pre, s, 0), 0.0)
        suf = suf + jnp.where(row < L - s, pltpu.roll(suf, L - s, 0), 0.0)
        s *= 2
    lane = lax.broadcasted_iota(jnp.int32, (L, LANES), 1)
    H = MLSTM_HEADS
    out = jnp.where((lane >= H) & (lane < 2 * H), pre, jnp.where((lane >= 3 * H) & (lane < 4 * H), suf, g))
    gcol_ref[...] = out[:, 0:GATE_W]
    grow_ref[...] = out.T[0:GATE_W, :]


def mlstm_pre(xm, conv_w, conv_b, bq, bk, bkt, bv, wg, bg, n_prompt, sample_seq):
    T = xm.shape[0]
    L = MLSTM_CHUNK
    n_tiles = T // L
    r8 = L // SUBLANES
    n_blk8 = T // SUBLANES
    C = MLSTM_INNER
    kern = functools.partial(_mlstm_pre_kernel, n_prompt_tiles=n_prompt // L, tiles_per_seq=sample_seq // L)
    const3 = lambda t: (0, 0, 0)
    return pl.pallas_call(
        kern,
        grid=(n_tiles,),
        in_specs=[
            pl.BlockSpec((L, C), lambda t: (t, 0)),
            pl.BlockSpec((SUBLANES, C), lambda t: (jnp.maximum(t * r8 - 1, 0), 0)),
            pl.BlockSpec((SUBLANES, C), lambda t: (jnp.minimum((t + 1) * r8, n_blk8 - 1), 0)),
            pl.BlockSpec((MLSTM_CONV_K, C), lambda t: (0, 0)),
            pl.BlockSpec((1, C), lambda t: (0, 0)),
            pl.BlockSpec(bq.shape, const3),
            pl.BlockSpec(bk.shape, const3),
            pl.BlockSpec(bkt.shape, const3),
            pl.BlockSpec(bv.shape, const3),
            pl.BlockSpec(wg.shape, const3),
            pl.BlockSpec((1, LANES), lambda t: (0, 0)),
        ],
        out_specs=[
            pl.BlockSpec((L, C), lambda t: (t, 0)),
            pl.BlockSpec((L, C), lambda t: (t, 0)),
            pl.BlockSpec((None, C, L), lambda t: (t, 0, 0)),
            pl.BlockSpec((L, C), lambda t: (t, 0)),
            pl.BlockSpec((L, GATE_W), lambda t: (t, 0)),
            pl.BlockSpec((None, GATE_W, L), lambda t: (t, 0, 0)),
        ],
        out_shape=[
            jax.ShapeDtypeStruct((T, C), BF16),
            jax.ShapeDtypeStruct((T, C), BF16),
            jax.ShapeDtypeStruct((n_tiles, C, L), BF16),
            jax.ShapeDtypeStruct((T, C), BF16),
            jax.ShapeDtypeStruct((T, GATE_W), F32),
            jax.ShapeDtypeStruct((n_tiles, GATE_W, L), F32),
        ],
        compiler_params=_params("parallel"),
        name="mlstm_pre",
    )(xm, xm, xm, conv_w, conv_b.reshape(1, C), bq, bk, bkt, bv, wg, bg)


def _gate_cols(gcol, h):
    lane = lax.broadcasted_iota(jnp.int32, gcol.shape, 1)
    pick = lambda idx: jnp.sum(jnp.where(lane == idx, gcol, 0.0), axis=-1, keepdims=True)
    return pick(MLSTM_HEADS + h), pick(3 * MLSTM_HEADS + h)


def _chunk_dir(qk, q, kt, v, b_col, b_row, i_row, tot, causal_mask, m_prev, C_prev, n_prev):
    ar = i_row - b_row
    d = jnp.where(causal_mask, b_col + ar, -jnp.inf)
    inter = b_col + m_prev
    mt = jnp.maximum(jnp.max(d, axis=-1, keepdims=True), inter)
    s = qk * jnp.exp(d - mt)
    num = _dot(s.astype(BF16), v)
    den = jnp.sum(s, axis=-1, keepdims=True)
    if C_prev is not None:
        w_inter = jnp.exp(inter - mt)
        num = num + w_inter * _dot(q, C_prev.astype(BF16))
        den = den + w_inter * jnp.sum(q.astype(F32) * n_prev, axis=-1, keepdims=True)
    h = num / jnp.maximum(jnp.abs(den), jnp.exp(-mt))

    g = tot + ar
    m_new = jnp.maximum(tot + m_prev, jnp.max(g, axis=-1, keepdims=True))
    w = jnp.exp(g - m_new)
    C_new = _dot((kt.astype(F32) * w).astype(BF16), v)
    w_hi = w.astype(BF16).astype(F32)
    rid = lax.broadcasted_iota(jnp.int32, (SUBLANES, w.shape[-1]), 0)
    w8 = jnp.where(rid == 0, w_hi, jnp.where(rid == 1, w - w_hi, 0.0)).astype(BF16)
    n8 = _dot_nt(w8, kt)
    n_new = n8[0:1] + n8[1:2]
    if C_prev is not None:
        decay = jnp.exp(tot + m_prev - m_new)
        C_new = decay * C_prev + C_new
        n_new = decay * n_prev + n_new
    return h, C_new, n_new, m_new


def _head_layernorm(h, g):
    mu = jnp.mean(h, axis=-1, keepdims=True)
    hc = h - mu
    var = jnp.mean(hc * hc, axis=-1, keepdims=True)
    return hc * lax.rsqrt(var + NORM_EPS) * g


def _chunk_masks(L):
    t_idx = lax.broadcasted_iota(jnp.int32, (L, L), 0)
    s_idx = lax.broadcasted_iota(jnp.int32, (L, L), 1)
    return s_idx <= t_idx, s_idx >= t_idx


def _mlstm_prompt_kernel(q_ref, kt_ref, v_ref, gcol_ref, grow_ref, mhg_ref,
                         hn_ref, C_ref, n_ref, m_ref):
    L = MLSTM_CHUNK
    H, DH = MLSTM_HEADS, MLSTM_HEAD_DIM
    fmask, bmask = _chunk_masks(L)
    zero = jnp.zeros((1, 1), F32)
    lane = lax.broadcasted_iota(jnp.int32, (1, LANES), 1)
    col = lambda idx: gcol_ref[:, idx:idx + 1]
    row = lambda idx: grow_ref[idx:idx + 1, :]
    for h in range(H):
        ch = slice(h * DH, (h + 1) * DH)
        q, kt, v = q_ref[:, ch], kt_ref[ch, :], v_ref[:, ch]
        qk = _dot(q, kt)
        bf_row, bb_row = row(H + h), row(3 * H + h)
        hf, Cf, nf, mf = _chunk_dir(qk, q, kt, v, col(H + h), bf_row, row(h), bf_row[:, L - 1:L], fmask,
                                    zero, None, None)
        hb, Cb, nb, mb = _chunk_dir(qk, q, kt, v, col(3 * H + h), bb_row, row(2 * H + h), bb_row[:, 0:1], bmask,
                                    zero, None, None)
        hn_ref[:, ch] = _head_layernorm(hf + hb, mhg_ref[:, ch]).astype(hn_ref.dtype)
        C_ref[0, h] = Cf
        C_ref[1, h] = Cb
        n_ref[0, h] = nf
        n_ref[1, h] = nb
        m_ref[h] = jnp.where(lane == 0, mf, jnp.where(lane == 1, mb, 0.0))


def mlstm_prompt(q, kt, v, gcol, grow, mh_g, batch):
    L, H, DH, C = MLSTM_CHUNK, MLSTM_HEADS, MLSTM_HEAD_DIM, MLSTM_INNER
    return pl.pallas_call(
        _mlstm_prompt_kernel,
        grid=(batch,),
        in_specs=[
            pl.BlockSpec((L, C), lambda b: (b, 0)),
            pl.BlockSpec((None, C, L), lambda b: (b, 0, 0)),
            pl.BlockSpec((L, C), lambda b: (b, 0)),
            pl.BlockSpec((L, GATE_W), lambda b: (b, 0)),
            pl.BlockSpec((None, GATE_W, L), lambda b: (b, 0, 0)),
            pl.BlockSpec((1, C), lambda b: (0, 0)),
        ],
        out_specs=[
            pl.BlockSpec((L, C), lambda b: (b, 0)),
            pl.BlockSpec((None, None, 2, H, DH, DH), lambda b: (b, 0, 0, 0, 0, 0)),
            pl.BlockSpec((None, 2, H, 1, DH), lambda b: (b, 0, 0, 0, 0)),
            pl.BlockSpec((None, H, 1, LANES), lambda b: (b, 0, 0, 0)),
        ],
        out_shape=[
            jax.ShapeDtypeStruct((batch * L, C), BF16),
            jax.ShapeDtypeStruct((batch, 1, 2, H, DH, DH), F32),
            jax.ShapeDtypeStruct((batch, 2, H, 1, DH), F32),
            jax.ShapeDtypeStruct((batch, H, 1, LANES), F32),
        ],
        compiler_params=_params("parallel"),
        name="mlstm_prompt",
    )(q, kt, v, gcol, grow, mh_g.reshape(1, C))


def _mlstm_sample_kernel(q_ref, kt_ref, v_ref, gcol_ref, grow_ref, mhg_ref, C0_ref, n0_ref, m0_ref,
                         hn_ref, hf_ref, hb_ref, C_ref, n_ref, m_ref, *, n_chunks):
    L = MLSTM_CHUNK
    H = MLSTM_HEADS
    h = pl.program_id(1)
    fmask, bmask = _chunk_masks(L)
    C_ref[...] = C0_ref[...]
    n_ref[...] = n0_ref[...]
    m_ref[...] = m0_ref[...]

    def one(c, direction):
        rows = pl.ds(pl.multiple_of(c * L, L), L)
        q, kt, v = q_ref[rows, :], kt_ref[c], v_ref[rows, :]
        qk = _dot(q, kt)
        gcol = gcol_ref[rows, :]
        b_col = _gate_cols(gcol, h)[direction]
        grow = lambda idx: grow_ref[c, pl.ds(idx, 1), :]
        b_row = grow((2 * direction + 1) * H + h)
        i_row = grow(2 * direction * H + h)
        tot = b_row[:, L - 1:L] if direction == 0 else b_row[:, 0:1]
        mask = fmask if direction == 0 else bmask
        m_prev = m_ref[direction][:, 0:1]
        hh, C_new, n_new, m_new = _chunk_dir(qk, q, kt, v, b_col, b_row, i_row, tot, mask, m_prev,
                                             C_ref[direction], n_ref[direction])
        C_ref[direction] = C_new
        n_ref[direction] = n_new
        m_ref[direction] = jnp.broadcast_to(m_new, (1, LANES))
        (hf_ref if direction == 0 else hb_ref)[rows, :] = hh

    def body(i, carry):
        one(i, 0)
        one(n_chunks - 1 - i, 1)
        return carry

    lax.fori_loop(0, n_chunks, body, 0)

    def norm(c, carry):
        rows = pl.ds(pl.multiple_of(c * L, L), L)
        hn_ref[rows, :] = _head_layernorm(hf_ref[rows, :] + hb_ref[rows, :], mhg_ref[...]).astype(hn_ref.dtype)
        return carry

    lax.fori_loop(0, n_chunks, norm, 0)


def mlstm_sample(q, kt, v, gcol, grow, mh_g, C0, n0, m0, n_prompt, batch, seq):
    L, H, DH = MLSTM_CHUNK, MLSTM_HEADS, MLSTM_HEAD_DIM
    nc = seq // L
    row_off = n_prompt // seq
    tile_off = n_prompt // L // nc
    return pl.pallas_call(
        functools.partial(_mlstm_sample_kernel, n_chunks=nc),
        grid=(batch, H),
        in_specs=[
            pl.BlockSpec((seq, DH), lambda b, h: (b + row_off, h)),
            pl.BlockSpec((nc, DH, L), lambda b, h: (b + tile_off, h, 0)),
            pl.BlockSpec((seq, DH), lambda b, h: (b + row_off, h)),
            pl.BlockSpec((seq, GATE_W), lambda b, h: (b + row_off, 0)),
            pl.BlockSpec((nc, GATE_W, L), lambda b, h: (b + tile_off, 0, 0)),
            pl.BlockSpec((1, DH), lambda b, h: (0, h)),
            pl.BlockSpec((None, 2, None, DH, DH), lambda b, h: (b, 0, h, 0, 0)),
            pl.BlockSpec((None, 2, None, 1, DH), lambda b, h: (b, 0, h, 0, 0)),
            pl.BlockSpec((None, None, 2, 1, LANES), lambda b, h: (b, h, 0, 0, 0)),
        ],
        out_specs=pl.BlockSpec((seq, DH), lambda b, h: (b, h)),
        out_shape=jax.ShapeDtypeStruct((batch * seq, MLSTM_INNER), BF16),
        scratch_shapes=[
            pltpu.VMEM((seq, DH), F32),
            pltpu.VMEM((seq, DH), F32),
            pltpu.VMEM((2, DH, DH), F32),
            pltpu.VMEM((2, 1, DH), F32),
            pltpu.VMEM((2, 1, LANES), F32),
        ],
        compiler_params=_params("parallel", "parallel"),
        name="mlstm_sample",
    )(q, kt, v, gcol, grow, mh_g.reshape(1, MLSTM_INNER), C0, n0, m0)


def _rope_tables(seq):
    t = jnp.arange(seq)
    row = (t // GRID_W).astype(F32)
    col = (t % GRID_W).astype(F32)
    n_freq = HEAD_DIM // 4
    inv = ROPE_THETA ** (-jnp.arange(n_freq, dtype=F32) / n_freq)
    ang = jnp.concatenate([row[:, None] * inv, col[:, None] * inv], axis=-1)
    cos, sin = jnp.cos(ang), jnp.sin(ang)
    reps = PROJ_N_TILE // HEAD_DIM
    return (jnp.tile(jnp.concatenate([cos, cos], axis=-1), (1, reps)),
            jnp.tile(jnp.concatenate([-sin, sin], axis=-1), (1, reps)))


def _block_diag(w):
    per = LANES // MLSTM_QKV_BLOCK
    w4 = w.reshape(-1, per, MLSTM_QKV_BLOCK, MLSTM_QKV_BLOCK)
    eye = jnp.eye(per, dtype=w.dtype)
    return jnp.einsum('kacd,ab->kacbd', w4, eye).reshape(-1, LANES, LANES)


def _diff_lambda_init(layer_idx):
    return 0.8 - 0.6 * math.exp(-0.3 * layer_idx)


def kernel(x_prompt, x_sample, cache_gqa_k, cache_gqa_v, cache_diff_k, cache_diff_v, state_mlstm_C, state_mlstm_n, state_mlstm_m, c, c_ctx, norm_g, w_mod, b_mod, gqa_w_in, gqa_q_norm_g, gqa_k_norm_g, gqa_w_out, diff_w_in, diff_lambda_q1, diff_lambda_k1, diff_lambda_q2, diff_lambda_k2, diff_subln_g, diff_w_out, mlstm_w_in, mlstm_conv_w, mlstm_conv_b, mlstm_w_q, mlstm_w_k, mlstm_w_v, mlstm_w_gate_f, mlstm_b_gate_f, mlstm_w_gate_b, mlstm_b_gate_b, mlstm_mh_norm_g, mlstm_skip, mlstm_w_out, final_norm_g):
    Bp, Sp, D = x_prompt.shape
    Bs, Ss, _ = x_sample.shape
    Tp, Ts = Bp * Sp, Bs * Ss
    assert Sp == MLSTM_CHUNK and Ss % TOKEN_TILE == 0 and Tp % TOKEN_TILE == 0
    H, DH = MLSTM_HEADS, MLSTM_HEAD_DIM
    past = cache_gqa_k.shape[2]

    x = (x_prompt.reshape(Tp, D), x_sample.reshape(Ts, D))
    cond = jnp.zeros((SUBLANES, D), F32).at[0].set(c_ctx).at[1:1 + Bs].set(c)
    mod4 = adaln_all(cond, w_mod, b_mod).reshape(DEPTH, SUBLANES, 1, 3 * D)
    rope_tabs = _rope_tables(Ss)
    gi = jnp.arange(PROJ_N_TILE) // HEAD_DIM
    gmat = (gi[:, None] == gi[None, :]).astype(BF16)
    tile_g = lambda g: jnp.tile(g, PROJ_N_TILE // HEAD_DIM).reshape(1, PROJ_N_TILE)
    ctx_gqa = (cache_gqa_k.reshape(Bs, -1, past, GQA_KV_W), cache_gqa_v.reshape(Bs, -1, past, GQA_KV_W))
    ctx_diff = (cache_diff_k.reshape(Bs, -1, past, DIFF_QK_W), cache_diff_v.reshape(Bs, -1, past, DIFF_BRANCH))

    gqa_k_list, gqa_v_list, diff_k_list, diff_v_list = [], [], [], []
    mC_list, mn_list, mm_list = [], [], []
    for i in range(DEPTH):
        kind, j = i % N_MIXERS, i // N_MIXERS
        common = dict(n_prompt=Tp, sample_seq=Ss)
        last = dict(final_g=final_norm_g) if i == DEPTH - 1 else {}
        if kind == 0:
            segs = ((GQA_Q_W, "q_norm", BF16), (GQA_KV_W, "k_norm", "split"), (GQA_KV_W, "plain", "split"),
                    (GQA_Q_W, "plain", BF16))
            q, kp, ks, vp, vs, z = proj_in(
                x, norm_g[i], mod4, i, gqa_w_in[j].astype(BF16), segs, rope_tabs=rope_tabs,
                head_norm=(gmat, tile_g(gqa_q_norm_g[j]), tile_g(gqa_k_norm_g[j])), **common)
            scr = [(GQA_KV_HEADS, 2 * LANES)] * 2
            kern = _gqa_attn_kernel
            o_p = attention(kern, scr, q, kp, vp, Bp, Sp, 0)
            o_s = attention(kern, scr, q, ks, vs, Bs, Ss, Tp, ctx=(*ctx_gqa, j))
            x = proj_out(o_p, o_s, z, x, mod4, i, gqa_w_out[j].astype(BF16), **common, **last)
            gqa_k_list.append(kp.reshape(Bp, 1, Sp, GQA_KV_HEADS, HEAD_DIM))
            gqa_v_list.append(vp.reshape(Bp, 1, Sp, GQA_KV_HEADS, HEAD_DIM))
        elif kind == 1:
            lam_init = _diff_lambda_init(i)
            segs = ((DIFF_QK_W, "q", BF16), (DIFF_QK_W, "k", (Sp, (DIFF_HEADS, 2, HEAD_DIM))),
                    (DIFF_BRANCH, "plain", "split"), (DIFF_BRANCH, "plain", BF16))
            q, kp, ks, k_cache, vp, vs, z = proj_in(x, norm_g[i], mod4, i, diff_w_in[j].astype(BF16), segs,
                                                    rope_tabs=rope_tabs, **common)
            lam_vecs = jnp.stack([diff_lambda_q1[j], diff_lambda_k1[j], diff_lambda_q2[j], diff_lambda_k2[j]])
            extra = (lam_vecs, diff_subln_g[j].reshape(1, DIFF_V_DIM))
            scr = [(DIFF_HEADS, 2 * LANES)]
            kern = functools.partial(_diff_attn_kernel, lam_init=lam_init)
            o_p = attention(kern, scr, q, kp, vp, Bp, Sp, 0, extra=extra)
            o_s = attention(kern, scr, q, ks, vs, Bs, Ss, Tp, ctx=(*ctx_diff, j), extra=extra)
            x = proj_out(o_p, o_s, z, x, mod4, i, diff_w_out[j].astype(BF16), **common, **last)
            diff_k_list.append(k_cache)
            diff_v_list.append(vp.reshape(Bp, 1, Sp, DIFF_HEADS, DIFF_V_DIM))
        else:
            segs = ((MLSTM_INNER, "plain", F32), (MLSTM_INNER, "plain", BF16))
            xm, z = proj_in(x, norm_g[i], mod4, i, mlstm_w_in[j].astype(BF16), segs, **common)
            bq = _block_diag(mlstm_w_q[j]).astype(BF16)
            bk = _block_diag(mlstm_w_k[j])
            bkt = (bk * (DH ** -0.5)).transpose(0, 2, 1).astype(BF16)
            bv = _block_diag(mlstm_w_v[j]).astype(BF16)
            wg = jnp.concatenate([mlstm_w_gate_f[j][:, :H], mlstm_w_gate_f[j][:, H:],
                                  mlstm_w_gate_b[j][:, :H], mlstm_w_gate_b[j][:, H:]], axis=-1)
            wg = jnp.pad(wg, ((0, 0), (0, LANES - GATE_W))).astype(BF16).reshape(3, MLSTM_INNER, LANES)
            bg = jnp.pad(jnp.concatenate([mlstm_b_gate_f[j], mlstm_b_gate_b[j]]), (0, LANES - GATE_W)).reshape(1, LANES)
            xc, q, kt, v, gcol, grow = mlstm_pre(xm, mlstm_conv_w[j], mlstm_conv_b[j], bq, bk.astype(BF16), bkt, bv,
                                                 wg, bg, **common)
            hn_p, Cp, np_, mp_ = mlstm_prompt(q, kt, v, gcol, grow, mlstm_mh_norm_g[j], Bp)
            C0 = state_mlstm_C[:, j]
            n0 = state_mlstm_n[:, j].reshape(Bs, 2, H, 1, DH)
            m0 = jnp.broadcast_to(state_mlstm_m[:, j].transpose(0, 2, 1)[..., None, None], (Bs, H, 2, 1, LANES))
            hn_s = mlstm_sample(q, kt, v, gcol, grow, mlstm_mh_norm_g[j], C0, n0, m0, Tp, Bs, Ss)
            x = proj_out(hn_p, hn_s, z, x, mod4, i, mlstm_w_out[j].astype(BF16), xc=xc, skip=mlstm_skip[j],
                         **common, **last)
            mC_list.append(Cp)
            mn_list.append(np_.reshape(Bp, 1, 2, H, DH))
            mm_list.append(mp_[:, :, 0, 0:2].transpose(0, 2, 1)[:, None])

    cat1 = lambda parts: parts[0] if len(parts) == 1 else jnp.concatenate(parts, axis=1)
    y_prompt, y_sample = x
    return (y_prompt.reshape(Bp, Sp, D), y_sample.reshape(Bs, Ss, D), cat1(gqa_k_list), cat1(gqa_v_list), cat1(diff_k_list), cat1(diff_v_list),
            cat1(mC_list), cat1(mn_list), cat1(mm_list))
```

```python
import functools
import math

import jax
import jax.numpy as jnp
from jax import lax
from jax.experimental import pallas as pl
from jax.experimental.pallas import tpu as pltpu

D_MODEL = 1024
DEPTH = 4
GRID_W = 64
N_MIXERS = 3
HEAD_DIM = 64
ROPE_THETA = 10000.0
NORM_EPS = 1e-6

GQA_HEADS = 16
GQA_KV_HEADS = 4
GQA_GROUP = GQA_HEADS // GQA_KV_HEADS
GQA_Q_W = GQA_HEADS * HEAD_DIM
GQA_KV_W = GQA_KV_HEADS * HEAD_DIM

DIFF_HEADS = 8
DIFF_QK_W = DIFF_HEADS * 2 * HEAD_DIM
DIFF_V_DIM = 2 * HEAD_DIM
DIFF_BRANCH = DIFF_HEADS * DIFF_V_DIM

MLSTM_INNER = 2 * D_MODEL
MLSTM_HEADS = 4
MLSTM_HEAD_DIM = MLSTM_INNER // MLSTM_HEADS
MLSTM_QKV_BLOCK = 4
MLSTM_CONV_K = 4

LANES = 128
SUBLANES = 8
VMEM_LIMIT_BYTES = 56 * 1024 * 1024

TOKEN_TILE = 512
PROJ_N_TILE = 256
ATTN_Q_TILE = 256
MLSTM_CHUNK = 256
GATE_W = 4 * MLSTM_HEADS

F32 = jnp.float32
BF16 = jnp.bfloat16


def _params(*semantics):
    return pltpu.CompilerParams(dimension_semantics=semantics, vmem_limit_bytes=VMEM_LIMIT_BYTES)


def _silu(x):
    return x * (1.0 / (1.0 + jnp.exp(-x)))


def _dot(a, b):
    return jnp.dot(a, b, preferred_element_type=F32)


def _dot_nt(a, b):
    return lax.dot_general(a, b, (((1,), (1,)), ((), ())), preferred_element_type=F32)


class TokenLayout:
    def __init__(self, n_prompt, seq, tile):
        self.prompt_tiles = n_prompt // tile
        self.tiles_per_seq = seq // tile

    def is_prompt(self, m):
        return m < self.prompt_tiles

    def cond_row(self, m):
        return jnp.where(m < self.prompt_tiles, 0, 1 + (m - self.prompt_tiles) // self.tiles_per_seq)

    def prompt_block(self, m):
        return jnp.minimum(m, self.prompt_tiles - 1)

    def sample_block(self, m):
        return jnp.maximum(m - self.prompt_tiles, 0)


def _mod_kernel(cond_ref, w_ref, b_ref, o_ref):
    a = _silu(cond_ref[...]).astype(BF16)
    o_ref[...] = _dot(a, w_ref[...].astype(BF16)) + b_ref[...]


def adaln_all(cond, w_mod, b_mod):
    tn = 1024
    return pl.pallas_call(
        _mod_kernel,
        grid=(DEPTH, 3 * D_MODEL // tn),
        in_specs=[
            pl.BlockSpec((SUBLANES, D_MODEL), lambda i, n: (0, 0)),
            pl.BlockSpec((None, D_MODEL, tn), lambda i, n: (i, 0, n)),
            pl.BlockSpec((None, 1, tn), lambda i, n: (i, 0, n)),
        ],
        out_specs=pl.BlockSpec((None, SUBLANES, tn), lambda i, n: (i, 0, n)),
        out_shape=jax.ShapeDtypeStruct((DEPTH, SUBLANES, 3 * D_MODEL), F32),
        compiler_params=_params("parallel", "parallel"),
        name="adaln_mod",
    )(cond, w_mod, b_mod.reshape(DEPTH, 1, 3 * D_MODEL))


def _swap_half_heads(y):
    w = y.shape[-1]
    lane = lax.broadcasted_iota(jnp.int32, y.shape, 1)
    upper = (lane & (HEAD_DIM // 2)) != 0
    return jnp.where(upper, pltpu.roll(y, HEAD_DIM // 2, 1), pltpu.roll(y, w - HEAD_DIM // 2, 1))


def _x_specs(x, lay, tm):
    if isinstance(x, tuple):
        return [pl.BlockSpec((tm, D_MODEL), lambda m: (lay.prompt_block(m), 0)),
                pl.BlockSpec((tm, D_MODEL), lambda m: (lay.sample_block(m), 0))], list(x)
    return [pl.BlockSpec((tm, D_MODEL), lambda m: (m, 0))], [x]


def _resident_spec(stacked, layer):
    return pl.BlockSpec((None,) + stacked.shape[1:], lambda m: (layer, 0, 0), pipeline_mode=pl.Buffered(1))


def _two_paths(is_prompt, path):
    pl.when(is_prompt)(functools.partial(path, True))
    pl.when(jnp.logical_not(is_prompt))(functools.partial(path, False))


def _proj_in_kernel(*refs, segs, layout, has_norm, has_rope, x_pair):
    it = iter(refs)
    x_refs = (next(it), next(it)) if x_pair else (next(it),) * 2
    g_ref, mod_ref, w_ref = next(it), next(it), next(it)
    if has_rope:
        cos_ref, sin_ref = next(it), next(it)
    if has_norm:
        gmat_ref, qg_ref, kg_ref = next(it), next(it), next(it)
    out_refs = []
    for _, _, kind, _ in segs:
        pair = (next(it), next(it)) if kind != "full" else (next(it),) * 2
        out_refs.append(pair + ((next(it),) if kind == "cache" else ()))
    cw = PROJ_N_TILE

    def store_cache(ref, trail, acc, col0):
        seq_len, hd = ref.shape[1], trail[-1]
        rolled = pltpu.roll(acc, acc.shape[1] - hd, 1)
        for s in range(ref.shape[0]):
            for p in range(cw // hd):
                idx, rem = [], (col0 + p * hd) // hd
                for n in reversed(trail[:-1]):
                    idx.insert(0, rem % n)
                    rem //= n
                src, lo = (acc, p * hd) if (p * hd) % LANES == 0 else (rolled, p * hd - hd)
                ref[(s, slice(None), *idx, slice(None))] = src[s * seq_len:(s + 1) * seq_len, lo:lo + hd]

    def path(prompt):
        x = x_refs[0 if prompt else 1][...]
        y = x * lax.rsqrt(jnp.mean(x * x, axis=-1, keepdims=True) + NORM_EPS) * g_ref[...]
        shift = mod_ref[:, 0:D_MODEL]
        scale = mod_ref[:, D_MODEL:2 * D_MODEL]
        h = (y * (1.0 + scale) + shift).astype(BF16)
        n_chunks = sum(s[0] for s in segs) // cw
        accs = [_dot(h, w_ref[:, i * cw:(i + 1) * cw].astype(BF16)) for i in range(n_chunks)]
        col = 0
        for (width, mode, kind, trail), o_refs in zip(segs, out_refs):
            o_ref = o_refs[0 if prompt else 1]
            for c in range(width // cw):
                acc = accs[col // cw]
                col += cw
                if mode in ("q_norm", "k_norm"):
                    ss = _dot((acc * acc).astype(BF16), gmat_ref[...])
                    gvec = qg_ref[...] if mode == "q_norm" else kg_ref[...]
                    acc = acc * lax.rsqrt(ss * (1.0 / HEAD_DIM) + NORM_EPS) * gvec
                if has_rope and mode != "plain" and not prompt:
                    acc = acc * cos_ref[...] + _swap_half_heads(acc) * sin_ref[...]
                if mode in ("q_norm", "q"):
                    acc = acc * (HEAD_DIM ** -0.5)
                o_ref[:, c * cw:(c + 1) * cw] = acc.astype(o_ref.dtype)
                if kind == "cache" and prompt:
                    store_cache(o_refs[2], trail, acc, c * cw)

    _two_paths(layout.is_prompt(pl.program_id(0)), path)


def proj_in(x, norm_g, mod4, layer, w, w_layer, segs, n_prompt, sample_seq, rope_tabs=None, head_norm=None):
    tm, cw = TOKEN_TILE, PROJ_N_TILE
    lay = TokenLayout(n_prompt, sample_seq, tm)
    T = x[0].shape[0] + x[1].shape[0] if isinstance(x, tuple) else x.shape[0]
    N = w.shape[2]
    assert N == sum(s[0] for s in segs) and all(s[0] % cw == 0 for s in segs)

    in_specs, args = _x_specs(x, lay, tm)
    in_specs += [
        pl.BlockSpec((1, D_MODEL), lambda m: (0, 0)),
        pl.BlockSpec((None, None, 1, 3 * D_MODEL), lambda m: (layer, lay.cond_row(m), 0, 0)),
        _resident_spec(w, w_layer),
    ]
    args += [norm_g.reshape(1, D_MODEL), mod4, w]
    if rope_tabs is not None:
        def pos_block(m):
            return (lay.sample_block(m) % lay.tiles_per_seq, 0)
        in_specs += [pl.BlockSpec((tm, cw), pos_block), pl.BlockSpec((tm, cw), pos_block)]
        args += list(rope_tabs)
    if head_norm is not None:
        in_specs += [pl.BlockSpec((cw, cw), lambda m: (0, 0)),
                     pl.BlockSpec((1, cw), lambda m: (0, 0)),
                     pl.BlockSpec((1, cw), lambda m: (0, 0))]
        args += list(head_norm)

    out_specs, out_shape, kern_segs = [], [], []
    for width, mode, out in segs:
        if isinstance(out, (str, tuple)):
            cache = isinstance(out, tuple)
            kern_segs.append((width, mode, "cache" if cache else "split", out[1] if cache else None))
            out_specs += [pl.BlockSpec((tm, width), lambda m: (lay.prompt_block(m), 0)),
                          pl.BlockSpec((tm, width), lambda m: (lay.sample_block(m), 0))]
            out_shape += [jax.ShapeDtypeStruct((n_prompt, width), BF16 if cache else F32),
                          jax.ShapeDtypeStruct((T - n_prompt, width), BF16)]
            if cache:
                seq_len, trail = out
                zeros = (0,) * (2 + len(trail))
                out_specs.append(pl.BlockSpec((tm // seq_len, None, seq_len) + tuple(trail),
                                              lambda m, zeros=zeros: (lay.prompt_block(m),) + zeros))
                out_shape.append(jax.ShapeDtypeStruct((n_prompt // seq_len, 1, seq_len) + tuple(trail), F32))
        else:
            kern_segs.append((width, mode, "full", None))
            out_specs.append(pl.BlockSpec((tm, width), lambda m: (m, 0)))
            out_shape.append(jax.ShapeDtypeStruct((T, width), out))

    kern = functools.partial(_proj_in_kernel, segs=tuple(kern_segs), layout=lay, x_pair=isinstance(x, tuple),
                             has_norm=head_norm is not None, has_rope=rope_tabs is not None)
    return pl.pallas_call(
        kern,
        grid=(T // tm,),
        in_specs=in_specs,
        out_specs=out_specs,
        out_shape=out_shape,
        compiler_params=_params("arbitrary"),
        name="proj_in",
    )(*args)


def _proj_out_kernel(*refs, layout, with_skip, x_pair, final):
    it = iter(refs)
    o_refs = (next(it), next(it))
    if with_skip:
        xc_ref, skip_ref = next(it), next(it)
    z_ref = next(it)
    x_refs = (next(it), next(it)) if x_pair else (next(it),) * 2
    mod_ref, w_ref = next(it), next(it)
    if final:
        fg_ref = next(it)
    out_refs = (next(it), next(it)) if final else (next(it),) * 2

    def path(prompt):
        sel = 0 if prompt else 1
        a = o_refs[sel][...].astype(F32)
        if with_skip:
            a = a + skip_ref[...] * xc_ref[...].astype(F32)
        a = (a * _silu(z_ref[...].astype(F32))).astype(BF16)
        gate = mod_ref[:, 2 * D_MODEL:3 * D_MODEL]
        x = x_refs[sel][...] + gate * _dot(a, w_ref[...].astype(BF16))
        if final:
            x = x * lax.rsqrt(jnp.mean(x * x, axis=-1, keepdims=True) + NORM_EPS) * fg_ref[...]
        out_refs[sel][...] = x

    _two_paths(layout.is_prompt(pl.program_id(0)), path)


def proj_out(o_prompt, o_sample, z, x, mod4, layer, w_out, w_layer, n_prompt, sample_seq, xc=None, skip=None,
             final_g=None):
    T = z.shape[0]
    K = o_prompt.shape[1]
    tm = TOKEN_TILE
    lay = TokenLayout(n_prompt, sample_seq, tm)
    row = lambda m: (m, 0)
    prompt_row = lambda m: (lay.prompt_block(m), 0)
    sample_row = lambda m: (lay.sample_block(m), 0)
    in_specs = [pl.BlockSpec((tm, K), prompt_row), pl.BlockSpec((tm, K), sample_row)]
    args = [o_prompt, o_sample]
    if xc is not None:
        in_specs += [pl.BlockSpec((tm, K), row), pl.BlockSpec((1, K), lambda m: (0, 0))]
        args += [xc, skip.reshape(1, K)]
    in_specs.append(pl.BlockSpec((tm, K), row))
    args.append(z)
    x_specs, x_args = _x_specs(x, lay, tm)
    in_specs += x_specs + [
        pl.BlockSpec((None, None, 1, 3 * D_MODEL), lambda m: (layer, lay.cond_row(m), 0, 0)),
        _resident_spec(w_out, w_layer),
    ]
    args += x_args + [mod4, w_out]
    if final_g is not None:
        in_specs.append(pl.BlockSpec((1, D_MODEL), lambda m: (0, 0)))
        args.append(final_g.reshape(1, D_MODEL))
        out_specs = [pl.BlockSpec((tm, D_MODEL), prompt_row), pl.BlockSpec((tm, D_MODEL), sample_row)]
        out_shape = [jax.ShapeDtypeStruct((n_prompt, D_MODEL), F32),
                     jax.ShapeDtypeStruct((T - n_prompt, D_MODEL), F32)]
    else:
        out_specs = pl.BlockSpec((tm, D_MODEL), row)
        out_shape = jax.ShapeDtypeStruct((T, D_MODEL), F32)
    kern = functools.partial(_proj_out_kernel, layout=lay, with_skip=xc is not None,
                             x_pair=isinstance(x, tuple), final=final_g is not None)
    return pl.pallas_call(
        kern,
        grid=(T // tm,),
        in_specs=in_specs,
        out_specs=out_specs,
        out_shape=out_shape,
        compiler_params=_params("arbitrary"),
        name="proj_out",
    )(*args)


def _low_half(shape):
    return lax.broadcasted_iota(jnp.int32, shape, 1) < HEAD_DIM


def _attend_group(items, rows):
    scores = [[_dot_nt(q, k) for k in k_parts] for q, k_parts, _, _ in items]
    probs = []
    for s in scores:
        m = functools.reduce(jnp.maximum, [jnp.max(x, axis=-1, keepdims=True) for x in s])
        probs.append([jnp.exp(x - m).astype(BF16) for x in s])
    out = []
    for p_parts, (_, _, v_ref, head) in zip(probs, items):
        acc = None
        for p, r in zip(p_parts, rows):
            t = _dot(p, v_ref[head, r, :])
            acc = t if acc is None else acc + t
        out.append(acc)
    return out


def _group_size(n_keys):
    return 16 if n_keys <= 512 else 4


def _row_parts(k_ref, kc_ref):
    n_new = k_ref.shape[0]
    rows = [slice(0, n_new)]
    if kc_ref is not None:
        rows.append(slice(n_new, n_new + kc_ref.shape[0]))
    return rows


def _gqa_attn_kernel(*refs, has_ctx):
    if has_ctx:
        q_ref, k_ref, v_ref, kc_ref, vc_ref, o_ref, ve_ref, vo_ref = refs
    else:
        q_ref, k_ref, v_ref, o_ref, ve_ref, vo_ref = refs
        kc_ref = vc_ref = None
    rows = _row_parts(k_ref, kc_ref)

    @pl.when(pl.program_id(1) == 0)
    def _():
        for src, r in zip([v_ref, vc_ref][:len(rows)], rows):
            for c in range(GQA_KV_HEADS):
                slab = src[:, (c // 2) * LANES:(c // 2 + 1) * LANES].astype(F32)
                low = _low_half(slab.shape)
                if c % 2 == 0:
                    ve = jnp.where(low, slab, 0.0)
                    vo = pltpu.roll(ve, HEAD_DIM, 1)
                else:
                    vo = jnp.where(low, 0.0, slab)
                    ve = pltpu.roll(vo, HEAD_DIM, 1)
                ones_e = jnp.where(low, 1.0, 0.0)
                ve_ref[c, r, 0:LANES] = ve.astype(BF16)
                ve_ref[c, r, LANES:2 * LANES] = ones_e.astype(BF16)
                vo_ref[c, r, 0:LANES] = vo.astype(BF16)
                vo_ref[c, r, LANES:2 * LANES] = (1.0 - ones_e).astype(BF16)

    k_srcs = [k_ref, kc_ref][:len(rows)]
    group = _group_size(rows[-1].stop)
    n_pairs = GQA_HEADS // 2
    for p0 in range(0, n_pairs, group // 2):
        items = []
        for pair in range(p0, p0 + group // 2):
            c = pair // (GQA_GROUP // 2)
            q = q_ref[:, pair * LANES:(pair + 1) * LANES].astype(F32)
            low = _low_half(q.shape)
            q_sw = pltpu.roll(q, HEAD_DIM, 1)
            if c % 2 == 0:
                qe, qo = jnp.where(low, q, 0.0), jnp.where(low, q_sw, 0.0)
            else:
                qe, qo = jnp.where(low, 0.0, q_sw), jnp.where(low, 0.0, q)
            k_parts = [s[:, (c // 2) * LANES:(c // 2 + 1) * LANES].astype(BF16) for s in k_srcs]
            items += [(qe.astype(BF16), k_parts, ve_ref, c), (qo.astype(BF16), k_parts, vo_ref, c)]
        accs = _attend_group(items, rows)
        for i, pair in enumerate(range(p0, p0 + group // 2)):
            acc = accs[2 * i] + accs[2 * i + 1]
            o_ref[:, pair * LANES:(pair + 1) * LANES] = (acc[:, 0:LANES] / acc[:, LANES:2 * LANES]).astype(o_ref.dtype)


def _diff_attn_kernel(*refs, has_ctx, lam_init):
    if has_ctx:
        q_ref, k_ref, v_ref, kc_ref, vc_ref, lam_ref, g_ref, o_ref, vx_ref = refs
    else:
        q_ref, k_ref, v_ref, lam_ref, g_ref, o_ref, vx_ref = refs
        kc_ref = vc_ref = None
    rows = _row_parts(k_ref, kc_ref)

    @pl.when(pl.program_id(1) == 0)
    def _():
        for src, r in zip([v_ref, vc_ref][:len(rows)], rows):
            for h in range(DIFF_HEADS):
                vx_ref[h, r, 0:LANES] = src[:, h * LANES:(h + 1) * LANES].astype(BF16)
                vx_ref[h, r, LANES:2 * LANES] = jnp.ones((r.stop - r.start, LANES), BF16)

    lv = lam_ref[...]
    s1 = jnp.sum(lv[0:1] * lv[1:2], axis=-1, keepdims=True)
    s2 = jnp.sum(lv[2:3] * lv[3:4], axis=-1, keepdims=True)
    lam = jnp.exp(s1) - jnp.exp(s2) + lam_init
    k_srcs = [k_ref, kc_ref][:len(rows)]
    group = _group_size(rows[-1].stop)
    for h0 in range(0, DIFF_HEADS, group // 2):
        items = []
        for h in range(h0, h0 + group // 2):
            q = q_ref[:, h * LANES:(h + 1) * LANES].astype(F32)
            low = _low_half(q.shape)
            k_parts = [s[:, h * LANES:(h + 1) * LANES].astype(BF16) for s in k_srcs]
            items += [(jnp.where(low, q, 0.0).astype(BF16), k_parts, vx_ref, h),
                      (jnp.where(low, 0.0, q).astype(BF16), k_parts, vx_ref, h)]
        accs = _attend_group(items, rows)
        for i, h in enumerate(range(h0, h0 + group // 2)):
            a0, a1 = accs[2 * i], accs[2 * i + 1]
            o = a0[:, 0:LANES] / a0[:, LANES:2 * LANES] - lam * (a1[:, 0:LANES] / a1[:, LANES:2 * LANES])
            o = o * lax.rsqrt(jnp.mean(o * o, axis=-1, keepdims=True) + NORM_EPS) * g_ref[...]
            o_ref[:, h * LANES:(h + 1) * LANES] = (o * (1.0 - lam_init)).astype(o_ref.dtype)


def attention(kernel, v_scratch, q, k, v, batch, sq, q_row0, ctx=None, extra=()):
    tq = ATTN_Q_TILE
    nq = sq // tq
    off = q_row0 // tq
    in_specs = [
        pl.BlockSpec((tq, q.shape[1]), lambda b, i: (b * nq + i + off, 0)),
        pl.BlockSpec((sq, k.shape[1]), lambda b, i: (b, 0)),
        pl.BlockSpec((sq, v.shape[1]), lambda b, i: (b, 0)),
    ]
    args = [q, k, v]
    sk = sq
    if ctx is not None:
        kc, vc, layer = ctx
        sk += kc.shape[2]
        in_specs += [pl.BlockSpec((None, None) + kc.shape[2:], lambda b, i: (b, layer, 0, 0)),
                     pl.BlockSpec((None, None) + vc.shape[2:], lambda b, i: (b, layer, 0, 0))]
        args += [kc, vc]
    for a in extra:
        in_specs.append(pl.BlockSpec(a.shape, lambda b, i: (0, 0)))
    args += list(extra)
    return pl.pallas_call(
        functools.partial(kernel, has_ctx=ctx is not None),
        grid=(batch, nq),
        in_specs=in_specs,
        out_specs=pl.BlockSpec((tq, D_MODEL), lambda b, i: (b * nq + i, 0)),
        out_shape=jax.ShapeDtypeStruct((batch * sq, D_MODEL), BF16),
        scratch_shapes=[pltpu.VMEM((heads, sk, width), BF16) for heads, width in v_scratch],
        compiler_params=_params("parallel", "arbitrary"),
        name="attention",
    )(*args)


def _log_sigmoid(x):
    return jnp.minimum(x, 0.0) - jnp.log1p(jnp.exp(-jnp.abs(x)))


def _mlstm_pre_kernel(cur_ref, prev_ref, next_ref, cw_ref, cb_ref, bq_ref, bk_ref, bkt_ref, bv_ref,
                      wg_ref, bg_ref, xc_ref, q_ref, kt_ref, v_ref, gcol_ref, grow_ref,
                      *, n_prompt_tiles, tiles_per_seq):
    L = MLSTM_CHUNK
    t = pl.program_id(0)
    st = jnp.maximum(t - n_prompt_tiles, 0) % tiles_per_seq
    first = (t < n_prompt_tiles) | (st == 0)
    last = (t < n_prompt_tiles) | (st == tiles_per_seq - 1)
    cur = cur_ref[...]
    prev = jnp.where(first, 0.0, prev_ref[...])
    nxt = jnp.where(last, 0.0, next_ref[...])
    ext = jnp.concatenate([prev, cur, nxt], axis=0)
    n_ext = L + 2 * SUBLANES
    pad_l = MLSTM_CONV_K // 2
    acc = cb_ref[...] + cw_ref[pad_l:pad_l + 1, :] * cur
    for j in range(MLSTM_CONV_K):
        if j == pad_l:
            continue
        sh = (pad_l - j) % n_ext
        acc = acc + cw_ref[j:j + 1, :] * pltpu.roll(ext, sh, 0)[SUBLANES:SUBLANES + L]
    xc = _silu(acc)
    xc_ref[...] = xc.astype(xc_ref.dtype)

    g = jnp.zeros((L, LANES), F32)
    for c in range(MLSTM_INNER // LANES):
        sl = slice(c * LANES, (c + 1) * LANES)
        xcb = xc[:, sl].astype(BF16)
        xmb = cur[:, sl].astype(BF16)
        qc = _dot(xcb, bq_ref[c]).astype(BF16)
        kc = _dot(xcb, bk_ref[c]).astype(BF16)
        vc = _dot(xmb, bv_ref[c]).astype(BF16)
        q_ref[:, sl] = qc
        v_ref[:, sl] = vc
        kt_ref[sl, :] = _dot_nt(bkt_ref[c], xcb).astype(BF16)
        g = g + _dot(qc, wg_ref[0, sl, :]) + _dot(kc, wg_ref[1, sl, :]) + _dot(vc, wg_ref[2, sl, :])
    g = g + bg_ref[...]

    ls = _log_sigmoid(g)
    row = lax.broadcasted_iota(jnp.int32, (L, LANES), 0)
    pre, suf = ls, ls
    s = 1
    while s < L:
        pre = pre + jnp.where(row >= s, pltpu.roll(pre, s, 0), 0.0)
        suf = suf + jnp.where(row < L - s, pltpu.roll(suf, L - s, 0), 0.0)
        s *= 2
    lane = lax.broadcasted_iota(jnp.int32, (L, LANES), 1)
    H = MLSTM_HEADS
    out = jnp.where((lane >= H) & (lane < 2 * H), pre, jnp.where((lane >= 3 * H) & (lane < 4 * H), suf, g))
    gcol_ref[...] = out[:, 0:GATE_W]
    grow_ref[...] = out.T[0:GATE_W, :]


def mlstm_pre(xm, conv_w, conv_b, bq, bk, bkt, bv, wg, bg, n_prompt, sample_seq):
    T = xm.shape[0]
    L = MLSTM_CHUNK
    n_tiles = T // L
    r8 = L // SUBLANES
    n_blk8 = T // SUBLANES
    C = MLSTM_INNER
    kern = functools.partial(_mlstm_pre_kernel, n_prompt_tiles=n_prompt // L, tiles_per_seq=sample_seq // L)
    const3 = lambda t: (0, 0, 0)
    return pl.pallas_call(
        kern,
        grid=(n_tiles,),
        in_specs=[
            pl.BlockSpec((L, C), lambda t: (t, 0)),
            pl.BlockSpec((SUBLANES, C), lambda t: (jnp.maximum(t * r8 - 1, 0), 0)),
            pl.BlockSpec((SUBLANES, C), lambda t: (jnp.minimum((t + 1) * r8, n_blk8 - 1), 0)),
            pl.BlockSpec((MLSTM_CONV_K, C), lambda t: (0, 0)),
            pl.BlockSpec((1, C), lambda t: (0, 0)),
            pl.BlockSpec(bq.shape, const3),
            pl.BlockSpec(bk.shape, const3),
            pl.BlockSpec(bkt.shape, const3),
            pl.BlockSpec(bv.shape, const3),
            pl.BlockSpec(wg.shape, const3),
            pl.BlockSpec((1, LANES), lambda t: (0, 0)),
        ],
        out_specs=[
            pl.BlockSpec((L, C), lambda t: (t, 0)),
            pl.BlockSpec((L, C), lambda t: (t, 0)),
            pl.BlockSpec((None, C, L), lambda t: (t, 0, 0)),
            pl.BlockSpec((L, C), lambda t: (t, 0)),
            pl.BlockSpec((L, GATE_W), lambda t: (t, 0)),
            pl.BlockSpec((None, GATE_W, L), lambda t: (t, 0, 0)),
        ],
        out_shape=[
            jax.ShapeDtypeStruct((T, C), BF16),
            jax.ShapeDtypeStruct((T, C), BF16),
            jax.ShapeDtypeStruct((n_tiles, C, L), BF16),
            jax.ShapeDtypeStruct((T, C), BF16),
            jax.ShapeDtypeStruct((T, GATE_W), F32),
            jax.ShapeDtypeStruct((n_tiles, GATE_W, L), F32),
        ],
        compiler_params=_params("parallel"),
        name="mlstm_pre",
    )(xm, xm, xm, conv_w, conv_b.reshape(1, C), bq, bk, bkt, bv, wg, bg)


def _gate_cols(gcol, h):
    lane = lax.broadcasted_iota(jnp.int32, gcol.shape, 1)
    pick = lambda idx: jnp.sum(jnp.where(lane == idx, gcol, 0.0), axis=-1, keepdims=True)
    return pick(MLSTM_HEADS + h), pick(3 * MLSTM_HEADS + h)


def _chunk_weights(qk, b_col, b_row, i_row, tot, causal_mask, m_prev):
    ar = i_row - b_row
    d = jnp.where(causal_mask, b_col + ar, -jnp.inf)
    inter = b_col + m_prev
    mt = jnp.maximum(jnp.max(d, axis=-1, keepdims=True), inter)
    s = qk * jnp.exp(d - mt)
    g = tot + ar
    m_new = jnp.maximum(tot + m_prev, jnp.max(g, axis=-1, keepdims=True))
    w = jnp.exp(g - m_new)
    w_hi = w.astype(BF16).astype(F32)
    rid = lax.broadcasted_iota(jnp.int32, (SUBLANES, w.shape[-1]), 0)
    w8 = jnp.where(rid == 0, w_hi, jnp.where(rid == 1, w - w_hi, 0.0)).astype(BF16)
    return dict(s=s.astype(BF16), den=jnp.sum(s, axis=-1, keepdims=True), mt=mt, inter=inter, w=w, w8=w8,
                m_new=m_new, decay_arg=tot + m_prev - m_new)


def _chunk_matmuls(wts, q, kt, v, C_prev):
    num = _dot(wts["s"], v)
    qC = None if C_prev is None else _dot(q, C_prev.astype(BF16))
    C_add = _dot((kt.astype(F32) * wts["w"]).astype(BF16), v)
    n8 = _dot_nt(wts["w8"], kt)
    return num, qC, C_add, n8[0:1] + n8[1:2]


def _chunk_finish(wts, mats, q, C_prev, n_prev):
    num, qC, C_new, n_new = mats
    den, mt = wts["den"], wts["mt"]
    if C_prev is not None:
        w_inter = jnp.exp(wts["inter"] - mt)
        num = num + w_inter * qC
        den = den + w_inter * jnp.sum(q.astype(F32) * n_prev, axis=-1, keepdims=True)
        decay = jnp.exp(wts["decay_arg"])
        C_new = decay * C_prev + C_new
        n_new = decay * n_prev + n_new
    h = num / jnp.maximum(jnp.abs(den), jnp.exp(-mt))
    return h, C_new, n_new, wts["m_new"]


def _chunk_dir(qk, q, kt, v, b_col, b_row, i_row, tot, causal_mask, m_prev, C_prev, n_prev):
    wts = _chunk_weights(qk, b_col, b_row, i_row, tot, causal_mask, m_prev)
    return _chunk_finish(wts, _chunk_matmuls(wts, q, kt, v, C_prev), q, C_prev, n_prev)


def _head_layernorm(h, g):
    mu = jnp.mean(h, axis=-1, keepdims=True)
    hc = h - mu
    var = jnp.mean(hc * hc, axis=-1, keepdims=True)
    return hc * lax.rsqrt(var + NORM_EPS) * g


def _chunk_masks(L):
    t_idx = lax.broadcasted_iota(jnp.int32, (L, L), 0)
    s_idx = lax.broadcasted_iota(jnp.int32, (L, L), 1)
    return s_idx <= t_idx, s_idx >= t_idx


def _mlstm_prompt_kernel(q_ref, kt_ref, v_ref, gcol_ref, grow_ref, mhg_ref,
                         hn_ref, C_ref, n_ref, m_ref):
    L = MLSTM_CHUNK
    H, DH = MLSTM_HEADS, MLSTM_HEAD_DIM
    fmask, bmask = _chunk_masks(L)
    zero = jnp.zeros((1, 1), F32)
    lane = lax.broadcasted_iota(jnp.int32, (1, LANES), 1)
    col = lambda idx: gcol_ref[:, idx:idx + 1]
    row = lambda idx: grow_ref[idx:idx + 1, :]
    chans = [slice(h * DH, (h + 1) * DH) for h in range(H)]
    qks = [_dot(q_ref[:, ch], kt_ref[ch, :]) for ch in chans]
    wts = []
    for h in range(H):
        bf_row, bb_row = row(H + h), row(3 * H + h)
        wts.append((_chunk_weights(qks[h], col(H + h), bf_row, row(h), bf_row[:, L - 1:L], fmask, zero),
                    _chunk_weights(qks[h], col(3 * H + h), bb_row, row(2 * H + h), bb_row[:, 0:1], bmask, zero)))
    mats = [[_chunk_matmuls(w, q_ref[:, ch], kt_ref[ch, :], v_ref[:, ch], None) for w in wts[h]]
            for h, ch in enumerate(chans)]
    for h, ch in enumerate(chans):
        (hf, Cf, nf, mf), (hb, Cb, nb, mb) = [_chunk_finish(w, m, None, None, None) for w, m in zip(wts[h], mats[h])]
        hn_ref[:, ch] = _head_layernorm(hf + hb, mhg_ref[:, ch]).astype(hn_ref.dtype)
        C_ref[0, h] = Cf
        C_ref[1, h] = Cb
        n_ref[0, h] = nf
        n_ref[1, h] = nb
        m_ref[h] = jnp.where(lane == 0, mf, jnp.where(lane == 1, mb, 0.0))


def mlstm_prompt(q, kt, v, gcol, grow, mh_g, batch):
    L, H, DH, C = MLSTM_CHUNK, MLSTM_HEADS, MLSTM_HEAD_DIM, MLSTM_INNER
    return pl.pallas_call(
        _mlstm_prompt_kernel,
        grid=(batch,),
        in_specs=[
            pl.BlockSpec((L, C), lambda b: (b, 0)),
            pl.BlockSpec((None, C, L), lambda b: (b, 0, 0)),
            pl.BlockSpec((L, C), lambda b: (b, 0)),
            pl.BlockSpec((L, GATE_W), lambda b: (b, 0)),
            pl.BlockSpec((None, GATE_W, L), lambda b: (b, 0, 0)),
            pl.BlockSpec((1, C), lambda b: (0, 0)),
        ],
        out_specs=[
            pl.BlockSpec((L, C), lambda b: (b, 0)),
            pl.BlockSpec((None, None, 2, H, DH, DH), lambda b: (b, 0, 0, 0, 0, 0)),
            pl.BlockSpec((None, 2, H, 1, DH), lambda b: (b, 0, 0, 0, 0)),
            pl.BlockSpec((None, H, 1, LANES), lambda b: (b, 0, 0, 0)),
        ],
        out_shape=[
            jax.ShapeDtypeStruct((batch * L, C), BF16),
            jax.ShapeDtypeStruct((batch, 1, 2, H, DH, DH), F32),
            jax.ShapeDtypeStruct((batch, 2, H, 1, DH), F32),
            jax.ShapeDtypeStruct((batch, H, 1, LANES), F32),
        ],
        compiler_params=_params("parallel"),
        name="mlstm_prompt",
    )(q, kt, v, gcol, grow, mh_g.reshape(1, C))


def _mlstm_sample_kernel(q_ref, kt_ref, v_ref, gcol_ref, grow_ref, mhg_ref, C0_ref, n0_ref, m0_ref,
                         hn_ref, hf_ref, hb_ref, C_ref, n_ref, m_ref, *, n_chunks):
    L = MLSTM_CHUNK
    H = MLSTM_HEADS
    h = pl.program_id(1)
    fmask, bmask = _chunk_masks(L)
    C_ref[...] = C0_ref[...]
    n_ref[...] = n0_ref[...]
    m_ref[...] = m0_ref[...]

    def one(c, direction):
        rows = pl.ds(pl.multiple_of(c * L, L), L)
        q, kt, v = q_ref[rows, :], kt_ref[c], v_ref[rows, :]
        qk = _dot(q, kt)
        gcol = gcol_ref[rows, :]
        b_col = _gate_cols(gcol, h)[direction]
        grow = lambda idx: grow_ref[c, pl.ds(idx, 1), :]
        b_row = grow((2 * direction + 1) * H + h)
        i_row = grow(2 * direction * H + h)
        tot = b_row[:, L - 1:L] if direction == 0 else b_row[:, 0:1]
        mask = fmask if direction == 0 else bmask
        m_prev = m_ref[direction][:, 0:1]
        hh, C_new, n_new, m_new = _chunk_dir(qk, q, kt, v, b_col, b_row, i_row, tot, mask, m_prev,
                                             C_ref[direction], n_ref[direction])
        C_ref[direction] = C_new
        n_ref[direction] = n_new
        m_ref[direction] = jnp.broadcast_to(m_new, (1, LANES))
        (hf_ref if direction == 0 else hb_ref)[rows, :] = hh

    def body(i, carry):
        one(i, 0)
        one(n_chunks - 1 - i, 1)
        return carry

    lax.fori_loop(0, n_chunks, body, 0)

    def norm(c, carry):
        rows = pl.ds(pl.multiple_of(c * L, L), L)
        hn_ref[rows, :] = _head_layernorm(hf_ref[rows, :] + hb_ref[rows, :], mhg_ref[...]).astype(hn_ref.dtype)
        return carry

    lax.fori_loop(0, n_chunks, norm, 0)


def mlstm_sample(q, kt, v, gcol, grow, mh_g, C0, n0, m0, n_prompt, batch, seq):
    L, H, DH = MLSTM_CHUNK, MLSTM_HEADS, MLSTM_HEAD_DIM
    nc = seq // L
    row_off = n_prompt // seq
    tile_off = n_prompt // L // nc
    return pl.pallas_call(
        functools.partial(_mlstm_sample_kernel, n_chunks=nc),
        grid=(batch, H),
        in_specs=[
            pl.BlockSpec((seq, DH), lambda b, h: (b + row_off, h)),
            pl.BlockSpec((nc, DH, L), lambda b, h: (b + tile_off, h, 0)),
            pl.BlockSpec((seq, DH), lambda b, h: (b + row_off, h)),
            pl.BlockSpec((seq, GATE_W), lambda b, h: (b + row_off, 0)),
            pl.BlockSpec((nc, GATE_W, L), lambda b, h: (b + tile_off, 0, 0)),
            pl.BlockSpec((1, DH), lambda b, h: (0, h)),
            pl.BlockSpec((None, 2, None, DH, DH), lambda b, h: (b, 0, h, 0, 0)),
            pl.BlockSpec((None, 2, None, 1, DH), lambda b, h: (b, 0, h, 0, 0)),
            pl.BlockSpec((None, None, 2, 1, LANES), lambda b, h: (b, h, 0, 0, 0)),
        ],
        out_specs=pl.BlockSpec((seq, DH), lambda b, h: (b, h)),
        out_shape=jax.ShapeDtypeStruct((batch * seq, MLSTM_INNER), BF16),
        scratch_shapes=[
            pltpu.VMEM((seq, DH), F32),
            pltpu.VMEM((seq, DH), F32),
            pltpu.VMEM((2, DH, DH), F32),
            pltpu.VMEM((2, 1, DH), F32),
            pltpu.VMEM((2, 1, LANES), F32),
        ],
        compiler_params=_params("parallel", "parallel"),
        name="mlstm_sample",
    )(q, kt, v, gcol, grow, mh_g.reshape(1, MLSTM_INNER), C0, n0, m0)


def _rope_tables(seq):
    t = jnp.arange(seq)
    row = (t // GRID_W).astype(F32)
    col = (t % GRID_W).astype(F32)
    n_freq = HEAD_DIM // 4
    inv = ROPE_THETA ** (-jnp.arange(n_freq, dtype=F32) / n_freq)
    ang = jnp.concatenate([row[:, None] * inv, col[:, None] * inv], axis=-1)
    cos, sin = jnp.cos(ang), jnp.sin(ang)
    reps = PROJ_N_TILE // HEAD_DIM
    return (jnp.tile(jnp.concatenate([cos, cos], axis=-1), (1, reps)),
            jnp.tile(jnp.concatenate([-sin, sin], axis=-1), (1, reps)))


def _block_diag(w):
    per = LANES // MLSTM_QKV_BLOCK
    w4 = w.reshape(-1, per, MLSTM_QKV_BLOCK, MLSTM_QKV_BLOCK)
    eye = jnp.eye(per, dtype=w.dtype)
    return jnp.einsum('kacd,ab->kacbd', w4, eye).reshape(-1, LANES, LANES)


def _diff_lambda_init(layer_idx):
    return 0.8 - 0.6 * math.exp(-0.3 * layer_idx)


def kernel(x_prompt, x_sample, cache_gqa_k, cache_gqa_v, cache_diff_k, cache_diff_v, state_mlstm_C, state_mlstm_n, state_mlstm_m, c, c_ctx, norm_g, w_mod, b_mod, gqa_w_in, gqa_q_norm_g, gqa_k_norm_g, gqa_w_out, diff_w_in, diff_lambda_q1, diff_lambda_k1, diff_lambda_q2, diff_lambda_k2, diff_subln_g, diff_w_out, mlstm_w_in, mlstm_conv_w, mlstm_conv_b, mlstm_w_q, mlstm_w_k, mlstm_w_v, mlstm_w_gate_f, mlstm_b_gate_f, mlstm_w_gate_b, mlstm_b_gate_b, mlstm_mh_norm_g, mlstm_skip, mlstm_w_out, final_norm_g):
    Bp, Sp, D = x_prompt.shape
    Bs, Ss, _ = x_sample.shape
    Tp, Ts = Bp * Sp, Bs * Ss
    assert Sp == MLSTM_CHUNK and Ss % TOKEN_TILE == 0 and Tp % TOKEN_TILE == 0
    H, DH = MLSTM_HEADS, MLSTM_HEAD_DIM
    past = cache_gqa_k.shape[2]

    x = (x_prompt.reshape(Tp, D), x_sample.reshape(Ts, D))
    cond = jnp.zeros((SUBLANES, D), F32).at[0].set(c_ctx).at[1:1 + Bs].set(c)
    mod4 = adaln_all(cond, w_mod, b_mod).reshape(DEPTH, SUBLANES, 1, 3 * D)
    rope_tabs = _rope_tables(Ss)
    gi = jnp.arange(PROJ_N_TILE) // HEAD_DIM
    gmat = (gi[:, None] == gi[None, :]).astype(BF16)
    tile_g = lambda g: jnp.tile(g, PROJ_N_TILE // HEAD_DIM).reshape(1, PROJ_N_TILE)
    ctx_gqa = (cache_gqa_k.reshape(Bs, -1, past, GQA_KV_W), cache_gqa_v.reshape(Bs, -1, past, GQA_KV_W))
    ctx_diff = (cache_diff_k.reshape(Bs, -1, past, DIFF_QK_W), cache_diff_v.reshape(Bs, -1, past, DIFF_BRANCH))

    gqa_k_list, gqa_v_list, diff_k_list, diff_v_list = [], [], [], []
    mC_list, mn_list, mm_list = [], [], []
    for i in range(DEPTH):
        kind, j = i % N_MIXERS, i // N_MIXERS
        common = dict(n_prompt=Tp, sample_seq=Ss)
        last = dict(final_g=final_norm_g) if i == DEPTH - 1 else {}
        if kind == 0:
            segs = ((GQA_Q_W, "q_norm", BF16), (GQA_KV_W, "k_norm", "split"), (GQA_KV_W, "plain", "split"),
                    (GQA_Q_W, "plain", BF16))
            q, kp, ks, vp, vs, z = proj_in(
                x, norm_g[i], mod4, i, gqa_w_in, j, segs, rope_tabs=rope_tabs,
                head_norm=(gmat, tile_g(gqa_q_norm_g[j]), tile_g(gqa_k_norm_g[j])), **common)
            scr = [(GQA_KV_HEADS, 2 * LANES)] * 2
            kern = _gqa_attn_kernel
            o_p = attention(kern, scr, q, kp, vp, Bp, Sp, 0)
            o_s = attention(kern, scr, q, ks, vs, Bs, Ss, Tp, ctx=(*ctx_gqa, j))
            x = proj_out(o_p, o_s, z, x, mod4, i, gqa_w_out, j, **common, **last)
            gqa_k_list.append(kp.reshape(Bp, 1, Sp, GQA_KV_HEADS, HEAD_DIM))
            gqa_v_list.append(vp.reshape(Bp, 1, Sp, GQA_KV_HEADS, HEAD_DIM))
        elif kind == 1:
            lam_init = _diff_lambda_init(i)
            segs = ((DIFF_QK_W, "q", BF16), (DIFF_QK_W, "k", (Sp, (DIFF_HEADS, 2, HEAD_DIM))),
                    (DIFF_BRANCH, "plain", "split"), (DIFF_BRANCH, "plain", BF16))
            q, kp, ks, k_cache, vp, vs, z = proj_in(x, norm_g[i], mod4, i, diff_w_in, j, segs,
                                                    rope_tabs=rope_tabs, **common)
            lam_vecs = jnp.stack([diff_lambda_q1[j], diff_lambda_k1[j], diff_lambda_q2[j], diff_lambda_k2[j]])
            extra = (lam_vecs, diff_subln_g[j].reshape(1, DIFF_V_DIM))
            scr = [(DIFF_HEADS, 2 * LANES)]
            kern = functools.partial(_diff_attn_kernel, lam_init=lam_init)
            o_p = attention(kern, scr, q, kp, vp, Bp, Sp, 0, extra=extra)
            o_s = attention(kern, scr, q, ks, vs, Bs, Ss, Tp, ctx=(*ctx_diff, j), extra=extra)
            x = proj_out(o_p, o_s, z, x, mod4, i, diff_w_out, j, **common, **last)
            diff_k_list.append(k_cache)
            diff_v_list.append(vp.reshape(Bp, 1, Sp, DIFF_HEADS, DIFF_V_DIM))
        else:
            segs = ((MLSTM_INNER, "plain", F32), (MLSTM_INNER, "plain", BF16))
            xm, z = proj_in(x, norm_g[i], mod4, i, mlstm_w_in, j, segs, **common)
            bq = _block_diag(mlstm_w_q[j]).astype(BF16)
            bk = _block_diag(mlstm_w_k[j])
            bkt = (bk * (DH ** -0.5)).transpose(0, 2, 1).astype(BF16)
            bv = _block_diag(mlstm_w_v[j]).astype(BF16)
            wg = jnp.concatenate([mlstm_w_gate_f[j][:, :H], mlstm_w_gate_f[j][:, H:],
                                  mlstm_w_gate_b[j][:, :H], mlstm_w_gate_b[j][:, H:]], axis=-1)
            wg = jnp.pad(wg, ((0, 0), (0, LANES - GATE_W))).astype(BF16).reshape(3, MLSTM_INNER, LANES)
            bg = jnp.pad(jnp.concatenate([mlstm_b_gate_f[j], mlstm_b_gate_b[j]]), (0, LANES - GATE_W)).reshape(1, LANES)
            xc, q, kt, v, gcol, grow = mlstm_pre(xm, mlstm_conv_w[j], mlstm_conv_b[j], bq, bk.astype(BF16), bkt, bv,
                                                 wg, bg, **common)
            hn_p, Cp, np_, mp_ = mlstm_prompt(q, kt, v, gcol, grow, mlstm_mh_norm_g[j], Bp)
            C0 = state_mlstm_C[:, j]
            n0 = state_mlstm_n[:, j].reshape(Bs, 2, H, 1, DH)
            m0 = jnp.broadcast_to(state_mlstm_m[:, j].transpose(0, 2, 1)[..., None, None], (Bs, H, 2, 1, LANES))
            hn_s = mlstm_sample(q, kt, v, gcol, grow, mlstm_mh_norm_g[j], C0, n0, m0, Tp, Bs, Ss)
            x = proj_out(hn_p, hn_s, z, x, mod4, i, mlstm_w_out, j, xc=xc, skip=mlstm_skip[j],
                         **common, **last)
            mC_list.append(Cp)
            mn_list.append(np_.reshape(Bp, 1, 2, H, DH))
            mm_list.append(mp_[:, :, 0, 0:2].transpose(0, 2, 1)[:, None])

    cat1 = lambda parts: parts[0] if len(parts) == 1 else jnp.concatenate(parts, axis=1)
    y_prompt, y_sample = x
    return (y_prompt.reshape(Bp, Sp, D), y_sample.reshape(Bs, Ss, D), cat1(gqa_k_list), cat1(gqa_v_list), cat1(diff_k_list), cat1(diff_v_list),
            cat1(mC_list), cat1(mn_list), cat1(mm_list))
```

```python
import functools
import math

import jax
import jax.numpy as jnp
from jax import lax
from jax.experimental import pallas as pl
from jax.experimental.pallas import tpu as pltpu

D_MODEL = 1024
DEPTH = 4
GRID_W = 64
N_MIXERS = 3
HEAD_DIM = 64
ROPE_THETA = 10000.0
NORM_EPS = 1e-6

GQA_HEADS = 16
GQA_KV_HEADS = 4
GQA_GROUP = GQA_HEADS // GQA_KV_HEADS
GQA_Q_W = GQA_HEADS * HEAD_DIM
GQA_KV_W = GQA_KV_HEADS * HEAD_DIM

DIFF_HEADS = 8
DIFF_QK_W = DIFF_HEADS * 2 * HEAD_DIM
DIFF_V_DIM = 2 * HEAD_DIM
DIFF_BRANCH = DIFF_HEADS * DIFF_V_DIM

MLSTM_INNER = 2 * D_MODEL
MLSTM_HEADS = 4
MLSTM_HEAD_DIM = MLSTM_INNER // MLSTM_HEADS
MLSTM_QKV_BLOCK = 4
MLSTM_CONV_K = 4

LANES = 128
SUBLANES = 8
VMEM_LIMIT_BYTES = 56 * 1024 * 1024

TOKEN_TILE = 512
PROJ_N_TILE = 256
ATTN_Q_TILE = 256
MLSTM_CHUNK = 256
GATE_W = 4 * MLSTM_HEADS

F32 = jnp.float32
BF16 = jnp.bfloat16


def _params(*semantics):
    return pltpu.CompilerParams(dimension_semantics=semantics, vmem_limit_bytes=VMEM_LIMIT_BYTES)


def _silu(x):
    return x * (1.0 / (1.0 + jnp.exp(-x)))


def _dot(a, b):
    return jnp.dot(a, b, preferred_element_type=F32)


def _dot_nt(a, b):
    return lax.dot_general(a, b, (((1,), (1,)), ((), ())), preferred_element_type=F32)


class TokenLayout:
    def __init__(self, n_prompt, seq, tile):
        self.prompt_tiles = n_prompt // tile
        self.tiles_per_seq = seq // tile

    def is_prompt(self, m):
        return m < self.prompt_tiles

    def cond_row(self, m):
        return jnp.where(m < self.prompt_tiles, 0, 1 + (m - self.prompt_tiles) // self.tiles_per_seq)

    def prompt_block(self, m):
        return jnp.minimum(m, self.prompt_tiles - 1)

    def sample_block(self, m):
        return jnp.maximum(m - self.prompt_tiles, 0)


def _mod_kernel(cond_ref, w_ref, b_ref, o_ref):
    a = _silu(cond_ref[...]).astype(BF16)
    o_ref[...] = _dot(a, w_ref[...].astype(BF16)) + b_ref[...]


def adaln_all(cond, w_mod, b_mod):
    tn = 1024
    return pl.pallas_call(
        _mod_kernel,
        grid=(DEPTH, 3 * D_MODEL // tn),
        in_specs=[
            pl.BlockSpec((SUBLANES, D_MODEL), lambda i, n: (0, 0)),
            pl.BlockSpec((None, D_MODEL, tn), lambda i, n: (i, 0, n)),
            pl.BlockSpec((None, 1, tn), lambda i, n: (i, 0, n)),
        ],
        out_specs=pl.BlockSpec((None, SUBLANES, tn), lambda i, n: (i, 0, n)),
        out_shape=jax.ShapeDtypeStruct((DEPTH, SUBLANES, 3 * D_MODEL), F32),
        compiler_params=_params("parallel", "parallel"),
        name="adaln_mod",
    )(cond, w_mod, b_mod.reshape(DEPTH, 1, 3 * D_MODEL))


def _swap_half_heads(y):
    w = y.shape[-1]
    lane = lax.broadcasted_iota(jnp.int32, y.shape, 1)
    upper = (lane & (HEAD_DIM // 2)) != 0
    return jnp.where(upper, pltpu.roll(y, HEAD_DIM // 2, 1), pltpu.roll(y, w - HEAD_DIM // 2, 1))


def _x_specs(x, lay, tm):
    if isinstance(x, tuple):
        return [pl.BlockSpec((tm, D_MODEL), lambda m: (lay.prompt_block(m), 0)),
                pl.BlockSpec((tm, D_MODEL), lambda m: (lay.sample_block(m), 0))], list(x)
    return [pl.BlockSpec((tm, D_MODEL), lambda m: (m, 0))], [x]


def _resident_spec(stacked, layer):
    return pl.BlockSpec((None,) + stacked.shape[1:], lambda m: (layer, 0, 0), pipeline_mode=pl.Buffered(1))


def _two_paths(is_prompt, path):
    pl.when(is_prompt)(functools.partial(path, True))
    pl.when(jnp.logical_not(is_prompt))(functools.partial(path, False))


def _proj_in_kernel(*refs, segs, layout, has_norm, has_rope, x_pair):
    it = iter(refs)
    x_refs = (next(it), next(it)) if x_pair else (next(it),) * 2
    g_ref, mod_ref, w_ref = next(it), next(it), next(it)
    if has_rope:
        cos_ref, sin_ref = next(it), next(it)
    if has_norm:
        gmat_ref, qg_ref, kg_ref = next(it), next(it), next(it)
    out_refs = []
    for _, _, kind, _ in segs:
        pair = (next(it), next(it)) if kind != "full" else (next(it),) * 2
        out_refs.append(pair + ((next(it),) if kind == "cache" else ()))
    cw = PROJ_N_TILE

    def store_cache(ref, acc, col0):
        seq_len = ref.shape[2]
        for s in range(ref.shape[0]):
            ref[s, col0:col0 + cw, :] = acc[s * seq_len:(s + 1) * seq_len, :].T

    def path(prompt):
        x = x_refs[0 if prompt else 1][...]
        y = x * lax.rsqrt(jnp.mean(x * x, axis=-1, keepdims=True) + NORM_EPS) * g_ref[...]
        shift = mod_ref[:, 0:D_MODEL]
        scale = mod_ref[:, D_MODEL:2 * D_MODEL]
        h = (y * (1.0 + scale) + shift).astype(BF16)
        n_chunks = sum(s[0] for s in segs) // cw
        accs = [_dot(h, w_ref[:, i * cw:(i + 1) * cw].astype(BF16)) for i in range(n_chunks)]
        col = 0
        for (width, mode, kind, trail), o_refs in zip(segs, out_refs):
            o_ref = o_refs[0 if prompt else 1]
            for c in range(width // cw):
                acc = accs[col // cw]
                col += cw
                if mode in ("q_norm", "k_norm"):
                    ss = _dot((acc * acc).astype(BF16), gmat_ref[...])
                    gvec = qg_ref[...] if mode == "q_norm" else kg_ref[...]
                    acc = acc * lax.rsqrt(ss * (1.0 / HEAD_DIM) + NORM_EPS) * gvec
                if has_rope and mode != "plain" and not prompt:
                    acc = acc * cos_ref[...] + _swap_half_heads(acc) * sin_ref[...]
                if mode in ("q_norm", "q"):
                    acc = acc * (HEAD_DIM ** -0.5)
                o_ref[:, c * cw:(c + 1) * cw] = acc.astype(o_ref.dtype)
                if kind == "cache" and prompt:
                    store_cache(o_refs[2], acc, c * cw)

    _two_paths(layout.is_prompt(pl.program_id(0)), path)


def proj_in(x, norm_g, mod4, layer, w, w_layer, segs, n_prompt, sample_seq, rope_tabs=None, head_norm=None):
    tm, cw = TOKEN_TILE, PROJ_N_TILE
    lay = TokenLayout(n_prompt, sample_seq, tm)
    T = x[0].shape[0] + x[1].shape[0] if isinstance(x, tuple) else x.shape[0]
    N = w.shape[2]
    assert N == sum(s[0] for s in segs) and all(s[0] % cw == 0 for s in segs)

    in_specs, args = _x_specs(x, lay, tm)
    in_specs += [
        pl.BlockSpec((1, D_MODEL), lambda m: (0, 0)),
        pl.BlockSpec((None, None, 1, 3 * D_MODEL), lambda m: (layer, lay.cond_row(m), 0, 0)),
        _resident_spec(w, w_layer),
    ]
    args += [norm_g.reshape(1, D_MODEL), mod4, w]
    if rope_tabs is not None:
        def pos_block(m):
            return (lay.sample_block(m) % lay.tiles_per_seq, 0)
        in_specs += [pl.BlockSpec((tm, cw), pos_block), pl.BlockSpec((tm, cw), pos_block)]
        args += list(rope_tabs)
    if head_norm is not None:
        in_specs += [pl.BlockSpec((cw, cw), lambda m: (0, 0)),
                     pl.BlockSpec((1, cw), lambda m: (0, 0)),
                     pl.BlockSpec((1, cw), lambda m: (0, 0))]
        args += list(head_norm)

    out_specs, out_shape, kern_segs = [], [], []
    for width, mode, out in segs:
        if isinstance(out, (str, tuple)):
            cache = isinstance(out, tuple)
            kern_segs.append((width, mode, "cache" if cache else "split", None))
            out_specs += [pl.BlockSpec((tm, width), lambda m: (lay.prompt_block(m), 0)),
                          pl.BlockSpec((tm, width), lambda m: (lay.sample_block(m), 0))]
            out_shape += [jax.ShapeDtypeStruct((n_prompt, width), BF16 if cache else F32),
                          jax.ShapeDtypeStruct((T - n_prompt, width), BF16)]
            if cache:
                seq_len = out[1]
                out_specs.append(pl.BlockSpec((tm // seq_len, None, width, seq_len),
                                              lambda m: (lay.prompt_block(m), 0, 0, 0)))
                out_shape.append(jax.ShapeDtypeStruct((n_prompt // seq_len, 1, width, seq_len), F32))
        else:
            kern_segs.append((width, mode, "full", None))
            out_specs.append(pl.BlockSpec((tm, width), lambda m: (m, 0)))
            out_shape.append(jax.ShapeDtypeStruct((T, width), out))

    kern = functools.partial(_proj_in_kernel, segs=tuple(kern_segs), layout=lay, x_pair=isinstance(x, tuple),
                             has_norm=head_norm is not None, has_rope=rope_tabs is not None)
    return pl.pallas_call(
        kern,
        grid=(T // tm,),
        in_specs=in_specs,
        out_specs=out_specs,
        out_shape=out_shape,
        compiler_params=_params("arbitrary"),
        name="proj_in",
    )(*args)


def _proj_out_kernel(*refs, layout, with_skip, x_pair, final):
    it = iter(refs)
    o_refs = (next(it), next(it))
    if with_skip:
        xc_ref, skip_ref = next(it), next(it)
    z_ref = next(it)
    x_refs = (next(it), next(it)) if x_pair else (next(it),) * 2
    mod_ref, w_ref = next(it), next(it)
    if final:
        fg_ref = next(it)
    out_refs = (next(it), next(it)) if final else (next(it),) * 2

    def path(prompt):
        sel = 0 if prompt else 1
        a = o_refs[sel][...].astype(F32)
        if with_skip:
            a = a + skip_ref[...] * xc_ref[...].astype(F32)
        a = (a * _silu(z_ref[...].astype(F32))).astype(BF16)
        gate = mod_ref[:, 2 * D_MODEL:3 * D_MODEL]
        x = x_refs[sel][...] + gate * _dot(a, w_ref[...].astype(BF16))
        if final:
            x = x * lax.rsqrt(jnp.mean(x * x, axis=-1, keepdims=True) + NORM_EPS) * fg_ref[...]
        out_refs[sel][...] = x

    _two_paths(layout.is_prompt(pl.program_id(0)), path)


def proj_out(o_prompt, o_sample, z, x, mod4, layer, w_out, w_layer, n_prompt, sample_seq, xc=None, skip=None,
             final_g=None):
    T = z.shape[0]
    K = o_prompt.shape[1]
    tm = TOKEN_TILE
    lay = TokenLayout(n_prompt, sample_seq, tm)
    row = lambda m: (m, 0)
    prompt_row = lambda m: (lay.prompt_block(m), 0)
    sample_row = lambda m: (lay.sample_block(m), 0)
    in_specs = [pl.BlockSpec((tm, K), prompt_row), pl.BlockSpec((tm, K), sample_row)]
    args = [o_prompt, o_sample]
    if xc is not None:
        in_specs += [pl.BlockSpec((tm, K), row), pl.BlockSpec((1, K), lambda m: (0, 0))]
        args += [xc, skip.reshape(1, K)]
    in_specs.append(pl.BlockSpec((tm, K), row))
    args.append(z)
    x_specs, x_args = _x_specs(x, lay, tm)
    in_specs += x_specs + [
        pl.BlockSpec((None, None, 1, 3 * D_MODEL), lambda m: (layer, lay.cond_row(m), 0, 0)),
        _resident_spec(w_out, w_layer),
    ]
    args += x_args + [mod4, w_out]
    if final_g is not None:
        in_specs.append(pl.BlockSpec((1, D_MODEL), lambda m: (0, 0)))
        args.append(final_g.reshape(1, D_MODEL))
        out_specs = [pl.BlockSpec((tm, D_MODEL), prompt_row), pl.BlockSpec((tm, D_MODEL), sample_row)]
        out_shape = [jax.ShapeDtypeStruct((n_prompt, D_MODEL), F32),
                     jax.ShapeDtypeStruct((T - n_prompt, D_MODEL), F32)]
    else:
        out_specs = pl.BlockSpec((tm, D_MODEL), row)
        out_shape = jax.ShapeDtypeStruct((T, D_MODEL), F32)
    kern = functools.partial(_proj_out_kernel, layout=lay, with_skip=xc is not None,
                             x_pair=isinstance(x, tuple), final=final_g is not None)
    return pl.pallas_call(
        kern,
        grid=(T // tm,),
        in_specs=in_specs,
        out_specs=out_specs,
        out_shape=out_shape,
        compiler_params=_params("arbitrary"),
        name="proj_out",
    )(*args)


def _low_half(shape):
    return lax.broadcasted_iota(jnp.int32, shape, 1) < HEAD_DIM


def _attend_group(items, rows):
    scores = [[_dot_nt(q, k) for k in k_parts] for q, k_parts, _, _ in items]
    probs = []
    for s in scores:
        m = functools.reduce(jnp.maximum, [jnp.max(x, axis=-1, keepdims=True) for x in s])
        probs.append([jnp.exp(x - m).astype(BF16) for x in s])
    out = []
    for p_parts, (_, _, v_ref, head) in zip(probs, items):
        acc = None
        for p, r in zip(p_parts, rows):
            t = _dot(p, v_ref[head, r, :])
            acc = t if acc is None else acc + t
        out.append(acc)
    return out


def _group_size(n_keys):
    return 16 if n_keys <= 512 else 4


def _row_parts(k_ref, kc_ref):
    n_new = k_ref.shape[0]
    rows = [slice(0, n_new)]
    if kc_ref is not None:
        rows.append(slice(n_new, n_new + kc_ref.shape[0]))
    return rows


def _gqa_attn_kernel(*refs, has_ctx):
    if has_ctx:
        q_ref, k_ref, v_ref, kc_ref, vc_ref, o_ref, ve_ref, vo_ref = refs
    else:
        q_ref, k_ref, v_ref, o_ref, ve_ref, vo_ref = refs
        kc_ref = vc_ref = None
    rows = _row_parts(k_ref, kc_ref)

    @pl.when(pl.program_id(1) == 0)
    def _():
        for src, r in zip([v_ref, vc_ref][:len(rows)], rows):
            for c in range(GQA_KV_HEADS):
                slab = src[:, (c // 2) * LANES:(c // 2 + 1) * LANES].astype(F32)
                low = _low_half(slab.shape)
                if c % 2 == 0:
                    ve = jnp.where(low, slab, 0.0)
                    vo = pltpu.roll(ve, HEAD_DIM, 1)
                else:
                    vo = jnp.where(low, 0.0, slab)
                    ve = pltpu.roll(vo, HEAD_DIM, 1)
                ones_e = jnp.where(low, 1.0, 0.0)
                ve_ref[c, r, 0:LANES] = ve.astype(BF16)
                ve_ref[c, r, LANES:2 * LANES] = ones_e.astype(BF16)
                vo_ref[c, r, 0:LANES] = vo.astype(BF16)
                vo_ref[c, r, LANES:2 * LANES] = (1.0 - ones_e).astype(BF16)

    k_srcs = [k_ref, kc_ref][:len(rows)]
    group = _group_size(rows[-1].stop)
    n_pairs = GQA_HEADS // 2
    for p0 in range(0, n_pairs, group // 2):
        items = []
        for pair in range(p0, p0 + group // 2):
            c = pair // (GQA_GROUP // 2)
            q = q_ref[:, pair * LANES:(pair + 1) * LANES].astype(F32)
            low = _low_half(q.shape)
            q_sw = pltpu.roll(q, HEAD_DIM, 1)
            if c % 2 == 0:
                qe, qo = jnp.where(low, q, 0.0), jnp.where(low, q_sw, 0.0)
            else:
                qe, qo = jnp.where(low, 0.0, q_sw), jnp.where(low, 0.0, q)
            k_parts = [s[:, (c // 2) * LANES:(c // 2 + 1) * LANES].astype(BF16) for s in k_srcs]
            items += [(qe.astype(BF16), k_parts, ve_ref, c), (qo.astype(BF16), k_parts, vo_ref, c)]
        accs = _attend_group(items, rows)
        for i, pair in enumerate(range(p0, p0 + group // 2)):
            acc = accs[2 * i] + accs[2 * i + 1]
            o_ref[:, pair * LANES:(pair + 1) * LANES] = (acc[:, 0:LANES] / acc[:, LANES:2 * LANES]).astype(o_ref.dtype)


def _diff_attn_kernel(*refs, has_ctx, lam_init):
    if has_ctx:
        q_ref, k_ref, v_ref, kc_ref, vc_ref, lam_ref, g_ref, o_ref, vx_ref = refs
    else:
        q_ref, k_ref, v_ref, lam_ref, g_ref, o_ref, vx_ref = refs
        kc_ref = vc_ref = None
    rows = _row_parts(k_ref, kc_ref)

    @pl.when(pl.program_id(1) == 0)
    def _():
        for src, r in zip([v_ref, vc_ref][:len(rows)], rows):
            for h in range(DIFF_HEADS):
                vx_ref[h, r, 0:LANES] = src[:, h * LANES:(h + 1) * LANES].astype(BF16)
                vx_ref[h, r, LANES:2 * LANES] = jnp.ones((r.stop - r.start, LANES), BF16)

    lv = lam_ref[...]
    s1 = jnp.sum(lv[0:1] * lv[1:2], axis=-1, keepdims=True)
    s2 = jnp.sum(lv[2:3] * lv[3:4], axis=-1, keepdims=True)
    lam = jnp.exp(s1) - jnp.exp(s2) + lam_init
    k_srcs = [k_ref, kc_ref][:len(rows)]
    group = _group_size(rows[-1].stop)
    for h0 in range(0, DIFF_HEADS, group // 2):
        items = []
        for h in range(h0, h0 + group // 2):
            q = q_ref[:, h * LANES:(h + 1) * LANES].astype(F32)
            low = _low_half(q.shape)
            k_parts = [s[:, h * LANES:(h + 1) * LANES].astype(BF16) for s in k_srcs]
            items += [(jnp.where(low, q, 0.0).astype(BF16), k_parts, vx_ref, h),
                      (jnp.where(low, 0.0, q).astype(BF16), k_parts, vx_ref, h)]
        accs = _attend_group(items, rows)
        for i, h in enumerate(range(h0, h0 + group // 2)):
            a0, a1 = accs[2 * i], accs[2 * i + 1]
            o = a0[:, 0:LANES] / a0[:, LANES:2 * LANES] - lam * (a1[:, 0:LANES] / a1[:, LANES:2 * LANES])
            o = o * lax.rsqrt(jnp.mean(o * o, axis=-1, keepdims=True) + NORM_EPS) * g_ref[...]
            o_ref[:, h * LANES:(h + 1) * LANES] = (o * (1.0 - lam_init)).astype(o_ref.dtype)


def attention(kernel, v_scratch, q, k, v, batch, sq, q_row0, ctx=None, extra=()):
    tq = ATTN_Q_TILE
    nq = sq // tq
    off = q_row0 // tq
    in_specs = [
        pl.BlockSpec((tq, q.shape[1]), lambda b, i: (b * nq + i + off, 0)),
        pl.BlockSpec((sq, k.shape[1]), lambda b, i: (b, 0)),
        pl.BlockSpec((sq, v.shape[1]), lambda b, i: (b, 0)),
    ]
    args = [q, k, v]
    sk = sq
    if ctx is not None:
        kc, vc, layer = ctx
        sk += kc.shape[2]
        in_specs += [pl.BlockSpec((None, None) + kc.shape[2:], lambda b, i: (b, layer, 0, 0)),
                     pl.BlockSpec((None, None) + vc.shape[2:], lambda b, i: (b, layer, 0, 0))]
        args += [kc, vc]
    for a in extra:
        in_specs.append(pl.BlockSpec(a.shape, lambda b, i: (0, 0)))
    args += list(extra)
    return pl.pallas_call(
        functools.partial(kernel, has_ctx=ctx is not None),
        grid=(batch, nq),
        in_specs=in_specs,
        out_specs=pl.BlockSpec((tq, D_MODEL), lambda b, i: (b * nq + i, 0)),
        out_shape=jax.ShapeDtypeStruct((batch * sq, D_MODEL), BF16),
        scratch_shapes=[pltpu.VMEM((heads, sk, width), BF16) for heads, width in v_scratch],
        compiler_params=_params("parallel", "arbitrary"),
        name="attention",
    )(*args)


def _log_sigmoid(x):
    return jnp.minimum(x, 0.0) - jnp.log1p(jnp.exp(-jnp.abs(x)))


def _mlstm_pre_kernel(cur_ref, prev_ref, next_ref, cw_ref, cb_ref, bq_ref, bk_ref, bkt_ref, bv_ref,
                      wg_ref, bg_ref, xc_ref, q_ref, kt_ref, v_ref, gcol_ref, grow_ref,
                      *, n_prompt_tiles, tiles_per_seq):
    L = MLSTM_CHUNK
    t = pl.program_id(0)
    st = jnp.maximum(t - n_prompt_tiles, 0) % tiles_per_seq
    first = (t < n_prompt_tiles) | (st == 0)
    last = (t < n_prompt_tiles) | (st == tiles_per_seq - 1)
    cur = cur_ref[...]
    prev = jnp.where(first, 0.0, prev_ref[...])
    nxt = jnp.where(last, 0.0, next_ref[...])
    ext = jnp.concatenate([prev, cur, nxt], axis=0)
    n_ext = L + 2 * SUBLANES
    pad_l = MLSTM_CONV_K // 2
    acc = cb_ref[...] + cw_ref[pad_l:pad_l + 1, :] * cur
    for j in range(MLSTM_CONV_K):
        if j == pad_l:
            continue
        sh = (pad_l - j) % n_ext
        acc = acc + cw_ref[j:j + 1, :] * pltpu.roll(ext, sh, 0)[SUBLANES:SUBLANES + L]
    xc = _silu(acc)
    xc_ref[...] = xc.astype(xc_ref.dtype)

    g = jnp.zeros((L, LANES), F32)
    for c in range(MLSTM_INNER // LANES):
        sl = slice(c * LANES, (c + 1) * LANES)
        xcb = xc[:, sl].astype(BF16)
        xmb = cur[:, sl].astype(BF16)
        qc = _dot(xcb, bq_ref[c]).astype(BF16)
        kc = _dot(xcb, bk_ref[c]).astype(BF16)
        vc = _dot(xmb, bv_ref[c]).astype(BF16)
        q_ref[:, sl] = qc
        v_ref[:, sl] = vc
        kt_ref[sl, :] = _dot_nt(bkt_ref[c], xcb).astype(BF16)
        g = g + _dot(qc, wg_ref[0, sl, :]) + _dot(kc, wg_ref[1, sl, :]) + _dot(vc, wg_ref[2, sl, :])
    g = g + bg_ref[...]

    ls = _log_sigmoid(g)
    row = lax.broadcasted_iota(jnp.int32, (L, LANES), 0)
    pre, suf = ls, ls
    s = 1
    while s < L:
        pre = pre + jnp.where(row >= s, pltpu.roll(pre, s, 0), 0.0)
        suf = suf + jnp.where(row < L - s, pltpu.roll(suf, L - s, 0), 0.0)
        s *= 2
    lane = lax.broadcasted_iota(jnp.int32, (L, LANES), 1)
    H = MLSTM_HEADS
    out = jnp.where((lane >= H) & (lane < 2 * H), pre, jnp.where((lane >= 3 * H) & (lane < 4 * H), suf, g))
    gcol_ref[...] = out[:, 0:GATE_W]
    grow_ref[...] = out.T[0:GATE_W, :]


def mlstm_pre(xm, conv_w, conv_b, bq, bk, bkt, bv, wg, bg, n_prompt, sample_seq):
    T = xm.shape[0]
    L = MLSTM_CHUNK
    n_tiles = T // L
    r8 = L // SUBLANES
    n_blk8 = T // SUBLANES
    C = MLSTM_INNER
    kern = functools.partial(_mlstm_pre_kernel, n_prompt_tiles=n_prompt // L, tiles_per_seq=sample_seq // L)
    const3 = lambda t: (0, 0, 0)
    return pl.pallas_call(
        kern,
        grid=(n_tiles,),
        in_specs=[
            pl.BlockSpec((L, C), lambda t: (t, 0)),
            pl.BlockSpec((SUBLANES, C), lambda t: (jnp.maximum(t * r8 - 1, 0), 0)),
            pl.BlockSpec((SUBLANES, C), lambda t: (jnp.minimum((t + 1) * r8, n_blk8 - 1), 0)),
            pl.BlockSpec((MLSTM_CONV_K, C), lambda t: (0, 0)),
            pl.BlockSpec((1, C), lambda t: (0, 0)),
            pl.BlockSpec(bq.shape, const3),
            pl.BlockSpec(bk.shape, const3),
            pl.BlockSpec(bkt.shape, const3),
            pl.BlockSpec(bv.shape, const3),
            pl.BlockSpec(wg.shape, const3),
            pl.BlockSpec((1, LANES), lambda t: (0, 0)),
        ],
        out_specs=[
            pl.BlockSpec((L, C), lambda t: (t, 0)),
            pl.BlockSpec((L, C), lambda t: (t, 0)),
            pl.BlockSpec((None, C, L), lambda t: (t, 0, 0)),
            pl.BlockSpec((L, C), lambda t: (t, 0)),
            pl.BlockSpec((L, GATE_W), lambda t: (t, 0)),
            pl.BlockSpec((None, GATE_W, L), lambda t: (t, 0, 0)),
        ],
        out_shape=[
            jax.ShapeDtypeStruct((T, C), BF16),
            jax.ShapeDtypeStruct((T, C), BF16),
            jax.ShapeDtypeStruct((n_tiles, C, L), BF16),
            jax.ShapeDtypeStruct((T, C), BF16),
            jax.ShapeDtypeStruct((T, GATE_W), F32),
            jax.ShapeDtypeStruct((n_tiles, GATE_W, L), F32),
        ],
        compiler_params=_params("parallel"),
        name="mlstm_pre",
    )(xm, xm, xm, conv_w, conv_b.reshape(1, C), bq, bk, bkt, bv, wg, bg)


def _gate_cols(gcol, h):
    lane = lax.broadcasted_iota(jnp.int32, gcol.shape, 1)
    pick = lambda idx: jnp.sum(jnp.where(lane == idx, gcol, 0.0), axis=-1, keepdims=True)
    return pick(MLSTM_HEADS + h), pick(3 * MLSTM_HEADS + h)


def _chunk_weights(qk, b_col, b_row, i_row, tot, causal_mask, m_prev):
    ar = i_row - b_row
    d = jnp.where(causal_mask, b_col + ar, -jnp.inf)
    inter = b_col + m_prev
    mt = jnp.maximum(jnp.max(d, axis=-1, keepdims=True), inter)
    s = qk * jnp.exp(d - mt)
    g = tot + ar
    m_new = jnp.maximum(tot + m_prev, jnp.max(g, axis=-1, keepdims=True))
    w = jnp.exp(g - m_new)
    w_hi = w.astype(BF16).astype(F32)
    rid = lax.broadcasted_iota(jnp.int32, (SUBLANES, w.shape[-1]), 0)
    w8 = jnp.where(rid == 0, w_hi, jnp.where(rid == 1, w - w_hi, 0.0)).astype(BF16)
    return dict(s=s.astype(BF16), den=jnp.sum(s, axis=-1, keepdims=True), mt=mt, inter=inter, w=w, w8=w8,
                m_new=m_new, decay_arg=tot + m_prev - m_new)


def _chunk_matmuls(wts, q, kt, v, C_prev):
    num = _dot(wts["s"], v)
    qC = None if C_prev is None else _dot(q, C_prev.astype(BF16))
    C_add = _dot((kt.astype(F32) * wts["w"]).astype(BF16), v)
    n8 = _dot_nt(wts["w8"], kt)
    return num, qC, C_add, n8[0:1] + n8[1:2]


def _chunk_finish(wts, mats, q, C_prev, n_prev):
    num, qC, C_new, n_new = mats
    den, mt = wts["den"], wts["mt"]
    if C_prev is not None:
        w_inter = jnp.exp(wts["inter"] - mt)
        num = num + w_inter * qC
        den = den + w_inter * jnp.sum(q.astype(F32) * n_prev, axis=-1, keepdims=True)
        decay = jnp.exp(wts["decay_arg"])
        C_new = decay * C_prev + C_new
        n_new = decay * n_prev + n_new
    h = num / jnp.maximum(jnp.abs(den), jnp.exp(-mt))
    return h, C_new, n_new, wts["m_new"]


def _chunk_dir(qk, q, kt, v, b_col, b_row, i_row, tot, causal_mask, m_prev, C_prev, n_prev):
    wts = _chunk_weights(qk, b_col, b_row, i_row, tot, causal_mask, m_prev)
    return _chunk_finish(wts, _chunk_matmuls(wts, q, kt, v, C_prev), q, C_prev, n_prev)


def _head_layernorm(h, g):
    mu = jnp.mean(h, axis=-1, keepdims=True)
    hc = h - mu
    var = jnp.mean(hc * hc, axis=-1, keepdims=True)
    return hc * lax.rsqrt(var + NORM_EPS) * g


def _chunk_masks(L):
    t_idx = lax.broadcasted_iota(jnp.int32, (L, L), 0)
    s_idx = lax.broadcasted_iota(jnp.int32, (L, L), 1)
    return s_idx <= t_idx, s_idx >= t_idx


def _mlstm_prompt_kernel(q_ref, kt_ref, v_ref, gcol_ref, grow_ref, mhg_ref,
                         hn_ref, C_ref, n_ref, m_ref):
    L = MLSTM_CHUNK
    H, DH = MLSTM_HEADS, MLSTM_HEAD_DIM
    fmask, bmask = _chunk_masks(L)
    zero = jnp.zeros((1, 1), F32)
    lane = lax.broadcasted_iota(jnp.int32, (1, LANES), 1)
    col = lambda idx: gcol_ref[:, idx:idx + 1]
    row = lambda idx: grow_ref[idx:idx + 1, :]
    for h in range(H):
        ch = slice(h * DH, (h + 1) * DH)
        q, kt, v = q_ref[:, ch], kt_ref[ch, :], v_ref[:, ch]
        qk = _dot(q, kt)
        bf_row, bb_row = row(H + h), row(3 * H + h)
        hf, Cf, nf, mf = _chunk_dir(qk, q, kt, v, col(H + h), bf_row, row(h), bf_row[:, L - 1:L], fmask,
                                    zero, None, None)
        hb, Cb, nb, mb = _chunk_dir(qk, q, kt, v, col(3 * H + h), bb_row, row(2 * H + h), bb_row[:, 0:1], bmask,
                                    zero, None, None)
        hn_ref[:, ch] = _head_layernorm(hf + hb, mhg_ref[:, ch]).astype(hn_ref.dtype)
        C_ref[0, h] = Cf
        C_ref[1, h] = Cb
        n_ref[0, h] = nf
        n_ref[1, h] = nb
        m_ref[h] = jnp.where(lane == 0, mf, jnp.where(lane == 1, mb, 0.0))


def mlstm_prompt(q, kt, v, gcol, grow, mh_g, batch):
    L, H, DH, C = MLSTM_CHUNK, MLSTM_HEADS, MLSTM_HEAD_DIM, MLSTM_INNER
    return pl.pallas_call(
        _mlstm_prompt_kernel,
        grid=(batch,),
        in_specs=[
            pl.BlockSpec((L, C), lambda b: (b, 0)),
            pl.BlockSpec((None, C, L), lambda b: (b, 0, 0)),
            pl.BlockSpec((L, C), lambda b: (b, 0)),
            pl.BlockSpec((L, GATE_W), lambda b: (b, 0)),
            pl.BlockSpec((None, GATE_W, L), lambda b: (b, 0, 0)),
            pl.BlockSpec((1, C), lambda b: (0, 0)),
        ],
        out_specs=[
            pl.BlockSpec((L, C), lambda b: (b, 0)),
            pl.BlockSpec((None, None, 2, H, DH, DH), lambda b: (b, 0, 0, 0, 0, 0)),
            pl.BlockSpec((None, 2, H, 1, DH), lambda b: (b, 0, 0, 0, 0)),
            pl.BlockSpec((None, H, 1, LANES), lambda b: (b, 0, 0, 0)),
        ],
        out_shape=[
            jax.ShapeDtypeStruct((batch * L, C), BF16),
            jax.ShapeDtypeStruct((batch, 1, 2, H, DH, DH), F32),
            jax.ShapeDtypeStruct((batch, 2, H, 1, DH), F32),
            jax.ShapeDtypeStruct((batch, H, 1, LANES), F32),
        ],
        compiler_params=_params("parallel"),
        name="mlstm_prompt",
    )(q, kt, v, gcol, grow, mh_g.reshape(1, C))


def _mlstm_sample_kernel(q_ref, kt_ref, v_ref, gcol_ref, grow_ref, mhg_ref, C0_ref, n0_ref, m0_ref,
                         hn_ref, hf_ref, hb_ref, C_ref, n_ref, m_ref, *, n_chunks):
    L = MLSTM_CHUNK
    H = MLSTM_HEADS
    h = pl.program_id(1)
    fmask, bmask = _chunk_masks(L)
    C_ref[...] = C0_ref[...]
    n_ref[...] = n0_ref[...]
    m_ref[...] = m0_ref[...]

    def one(c, direction):
        rows = pl.ds(pl.multiple_of(c * L, L), L)
        q, kt, v = q_ref[rows, :], kt_ref[c], v_ref[rows, :]
        qk = _dot(q, kt)
        gcol = gcol_ref[rows, :]
        b_col = _gate_cols(gcol, h)[direction]
        grow = lambda idx: grow_ref[c, pl.ds(idx, 1), :]
        b_row = grow((2 * direction + 1) * H + h)
        i_row = grow(2 * direction * H + h)
        tot = b_row[:, L - 1:L] if direction == 0 else b_row[:, 0:1]
        mask = fmask if direction == 0 else bmask
        m_prev = m_ref[direction][:, 0:1]
        hh, C_new, n_new, m_new = _chunk_dir(qk, q, kt, v, b_col, b_row, i_row, tot, mask, m_prev,
                                             C_ref[direction], n_ref[direction])
        C_ref[direction] = C_new
        n_ref[direction] = n_new
        m_ref[direction] = jnp.broadcast_to(m_new, (1, LANES))
        (hf_ref if direction == 0 else hb_ref)[rows, :] = hh

    def body(i, carry):
        one(i, 0)
        one(n_chunks - 1 - i, 1)
        return carry

    lax.fori_loop(0, n_chunks, body, 0)

    def norm(c, carry):
        rows = pl.ds(pl.multiple_of(c * L, L), L)
        hn_ref[rows, :] = _head_layernorm(hf_ref[rows, :] + hb_ref[rows, :], mhg_ref[...]).astype(hn_ref.dtype)
        return carry

    lax.fori_loop(0, n_chunks, norm, 0)


def mlstm_sample(q, kt, v, gcol, grow, mh_g, C0, n0, m0, n_prompt, batch, seq):
    L, H, DH = MLSTM_CHUNK, MLSTM_HEADS, MLSTM_HEAD_DIM
    nc = seq // L
    row_off = n_prompt // seq
    tile_off = n_prompt // L // nc
    return pl.pallas_call(
        functools.partial(_mlstm_sample_kernel, n_chunks=nc),
        grid=(batch, H),
        in_specs=[
            pl.BlockSpec((seq, DH), lambda b, h: (b + row_off, h)),
            pl.BlockSpec((nc, DH, L), lambda b, h: (b + tile_off, h, 0)),
            pl.BlockSpec((seq, DH), lambda b, h: (b + row_off, h)),
            pl.BlockSpec((seq, GATE_W), lambda b, h: (b + row_off, 0)),
            pl.BlockSpec((nc, GATE_W, L), lambda b, h: (b + tile_off, 0, 0)),
            pl.BlockSpec((1, DH), lambda b, h: (0, h)),
            pl.BlockSpec((None, 2, None, DH, DH), lambda b, h: (b, 0, h, 0, 0)),
            pl.BlockSpec((None, 2, None, 1, DH), lambda b, h: (b, 0, h, 0, 0)),
            pl.BlockSpec((None, None, 2, 1, LANES), lambda b, h: (b, h, 0, 0, 0)),
        ],
        out_specs=pl.BlockSpec((seq, DH), lambda b, h: (b, h)),
        out_shape=jax.ShapeDtypeStruct((batch * seq, MLSTM_INNER), BF16),
        scratch_shapes=[
            pltpu.VMEM((seq, DH), F32),
            pltpu.VMEM((seq, DH), F32),
            pltpu.VMEM((2, DH, DH), F32),
            pltpu.VMEM((2, 1, DH), F32),
            pltpu.VMEM((2, 1, LANES), F32),
        ],
        compiler_params=_params("parallel", "parallel"),
        name="mlstm_sample",
    )(q, kt, v, gcol, grow, mh_g.reshape(1, MLSTM_INNER), C0, n0, m0)


def _rope_tables(seq):
    t = jnp.arange(seq)
    row = (t // GRID_W).astype(F32)
    col = (t % GRID_W).astype(F32)
    n_freq = HEAD_DIM // 4
    inv = ROPE_THETA ** (-jnp.arange(n_freq, dtype=F32) / n_freq)
    ang = jnp.concatenate([row[:, None] * inv, col[:, None] * inv], axis=-1)
    cos, sin = jnp.cos(ang), jnp.sin(ang)
    reps = PROJ_N_TILE // HEAD_DIM
    return (jnp.tile(jnp.concatenate([cos, cos], axis=-1), (1, reps)),
            jnp.tile(jnp.concatenate([-sin, sin], axis=-1), (1, reps)))


def _block_diag(w):
    per = LANES // MLSTM_QKV_BLOCK
    w4 = w.reshape(-1, per, MLSTM_QKV_BLOCK, MLSTM_QKV_BLOCK)
    eye = jnp.eye(per, dtype=w.dtype)
    return jnp.einsum('kacd,ab->kacbd', w4, eye).reshape(-1, LANES, LANES)


def _diff_lambda_init(layer_idx):
    return 0.8 - 0.6 * math.exp(-0.3 * layer_idx)


def kernel(x_prompt, x_sample, cache_gqa_k, cache_gqa_v, cache_diff_k, cache_diff_v, state_mlstm_C, state_mlstm_n, state_mlstm_m, c, c_ctx, norm_g, w_mod, b_mod, gqa_w_in, gqa_q_norm_g, gqa_k_norm_g, gqa_w_out, diff_w_in, diff_lambda_q1, diff_lambda_k1, diff_lambda_q2, diff_lambda_k2, diff_subln_g, diff_w_out, mlstm_w_in, mlstm_conv_w, mlstm_conv_b, mlstm_w_q, mlstm_w_k, mlstm_w_v, mlstm_w_gate_f, mlstm_b_gate_f, mlstm_w_gate_b, mlstm_b_gate_b, mlstm_mh_norm_g, mlstm_skip, mlstm_w_out, final_norm_g):
    Bp, Sp, D = x_prompt.shape
    Bs, Ss, _ = x_sample.shape
    Tp, Ts = Bp * Sp, Bs * Ss
    assert Sp == MLSTM_CHUNK and Ss % TOKEN_TILE == 0 and Tp % TOKEN_TILE == 0
    H, DH = MLSTM_HEADS, MLSTM_HEAD_DIM
    past = cache_gqa_k.shape[2]

    x = (x_prompt.reshape(Tp, D), x_sample.reshape(Ts, D))
    cond = jnp.zeros((SUBLANES, D), F32).at[0].set(c_ctx).at[1:1 + Bs].set(c)
    mod4 = adaln_all(cond, w_mod, b_mod).reshape(DEPTH, SUBLANES, 1, 3 * D)
    rope_tabs = _rope_tables(Ss)
    gi = jnp.arange(PROJ_N_TILE) // HEAD_DIM
    gmat = (gi[:, None] == gi[None, :]).astype(BF16)
    tile_g = lambda g: jnp.tile(g, PROJ_N_TILE // HEAD_DIM).reshape(1, PROJ_N_TILE)
    ctx_gqa = (cache_gqa_k.reshape(Bs, -1, past, GQA_KV_W), cache_gqa_v.reshape(Bs, -1, past, GQA_KV_W))
    ctx_diff = (cache_diff_k.reshape(Bs, -1, past, DIFF_QK_W), cache_diff_v.reshape(Bs, -1, past, DIFF_BRANCH))

    gqa_k_list, gqa_v_list, diff_k_list, diff_v_list = [], [], [], []
    mC_list, mn_list, mm_list = [], [], []
    for i in range(DEPTH):
        kind, j = i % N_MIXERS, i // N_MIXERS
        common = dict(n_prompt=Tp, sample_seq=Ss)
        last = dict(final_g=final_norm_g) if i == DEPTH - 1 else {}
        if kind == 0:
            segs = ((GQA_Q_W, "q_norm", BF16), (GQA_KV_W, "k_norm", ("cache", Sp)),
                    (GQA_KV_W, "plain", ("cache", Sp)), (GQA_Q_W, "plain", BF16))
            q, kp, ks, k_cache, vp, vs, v_cache, z = proj_in(
                x, norm_g[i], mod4, i, gqa_w_in, j, segs, rope_tabs=rope_tabs,
                head_norm=(gmat, tile_g(gqa_q_norm_g[j]), tile_g(gqa_k_norm_g[j])), **common)
            scr = [(GQA_KV_HEADS, 2 * LANES)] * 2
            kern = _gqa_attn_kernel
            o_p = attention(kern, scr, q, kp, vp, Bp, Sp, 0)
            o_s = attention(kern, scr, q, ks, vs, Bs, Ss, Tp, ctx=(*ctx_gqa, j))
            x = proj_out(o_p, o_s, z, x, mod4, i, gqa_w_out, j, **common, **last)
            gqa_k_list.append(k_cache)
            gqa_v_list.append(v_cache)
        elif kind == 1:
            lam_init = _diff_lambda_init(i)
            segs = ((DIFF_QK_W, "q", BF16), (DIFF_QK_W, "k", ("cache", Sp)),
                    (DIFF_BRANCH, "plain", "split"), (DIFF_BRANCH, "plain", BF16))
            q, kp, ks, k_cache, vp, vs, z = proj_in(x, norm_g[i], mod4, i, diff_w_in, j, segs,
                                                    rope_tabs=rope_tabs, **common)
            lam_vecs = jnp.stack([diff_lambda_q1[j], diff_lambda_k1[j], diff_lambda_q2[j], diff_lambda_k2[j]])
            extra = (lam_vecs, diff_subln_g[j].reshape(1, DIFF_V_DIM))
            scr = [(DIFF_HEADS, 2 * LANES)]
            kern = functools.partial(_diff_attn_kernel, lam_init=lam_init)
            o_p = attention(kern, scr, q, kp, vp, Bp, Sp, 0, extra=extra)
            o_s = attention(kern, scr, q, ks, vs, Bs, Ss, Tp, ctx=(*ctx_diff, j), extra=extra)
            x = proj_out(o_p, o_s, z, x, mod4, i, diff_w_out, j, **common, **last)
            diff_k_list.append(k_cache)
            diff_v_list.append(vp.reshape(Bp, 1, Sp, DIFF_HEADS, DIFF_V_DIM))
        else:
            segs = ((MLSTM_INNER, "plain", F32), (MLSTM_INNER, "plain", BF16))
            xm, z = proj_in(x, norm_g[i], mod4, i, mlstm_w_in, j, segs, **common)
            bq = _block_diag(mlstm_w_q[j]).astype(BF16)
            bk = _block_diag(mlstm_w_k[j])
            bkt = (bk * (DH ** -0.5)).transpose(0, 2, 1).astype(BF16)
            bv = _block_diag(mlstm_w_v[j]).astype(BF16)
            wg = jnp.concatenate([mlstm_w_gate_f[j][:, :H], mlstm_w_gate_f[j][:, H:],
                                  mlstm_w_gate_b[j][:, :H], mlstm_w_gate_b[j][:, H:]], axis=-1)
            wg = jnp.pad(wg, ((0, 0), (0, LANES - GATE_W))).astype(BF16).reshape(3, MLSTM_INNER, LANES)
            bg = jnp.pad(jnp.concatenate([mlstm_b_gate_f[j], mlstm_b_gate_b[j]]), (0, LANES - GATE_W)).reshape(1, LANES)
            xc, q, kt, v, gcol, grow = mlstm_pre(xm, mlstm_conv_w[j], mlstm_conv_b[j], bq, bk.astype(BF16), bkt, bv,
                                                 wg, bg, **common)
            hn_p, Cp, np_, mp_ = mlstm_prompt(q, kt, v, gcol, grow, mlstm_mh_norm_g[j], Bp)
            C0 = state_mlstm_C[:, j]
            n0 = state_mlstm_n[:, j].reshape(Bs, 2, H, 1, DH)
            m0 = jnp.broadcast_to(state_mlstm_m[:, j].transpose(0, 2, 1)[..., None, None], (Bs, H, 2, 1, LANES))
            hn_s = mlstm_sample(q, kt, v, gcol, grow, mlstm_mh_norm_g[j], C0, n0, m0, Tp, Bs, Ss)
            x = proj_out(hn_p, hn_s, z, x, mod4, i, mlstm_w_out, j, xc=xc, skip=mlstm_skip[j],
                         **common, **last)
            mC_list.append(Cp)
            mn_list.append(np_.reshape(Bp, 1, 2, H, DH))
            mm_list.append(mp_[:, :, 0, 0:2].transpose(0, 2, 1)[:, None])

    cat1 = lambda parts: parts[0] if len(parts) == 1 else jnp.concatenate(parts, axis=1)
    y_prompt, y_sample = x

    def from_transposed(parts, trail):
        t = cat1(parts)
        t = t.reshape(t.shape[:2] + trail + (Sp,))
        return jnp.moveaxis(t, -1, 2)

    return (y_prompt.reshape(Bp, Sp, D), y_sample.reshape(Bs, Ss, D),
            from_transposed(gqa_k_list, (GQA_KV_HEADS, HEAD_DIM)), from_transposed(gqa_v_list, (GQA_KV_HEADS, HEAD_DIM)),
            from_transposed(diff_k_list, (DIFF_HEADS, 2, HEAD_DIM)), cat1(diff_v_list),
            cat1(mC_list), cat1(mn_list), cat1(mm_list))
```

```python
import functools
import math

import jax
import jax.numpy as jnp
from jax import lax
from jax.experimental import pallas as pl
from jax.experimental.pallas import tpu as pltpu

D_MODEL = 1024
DEPTH = 4
GRID_W = 64
N_MIXERS = 3
HEAD_DIM = 64
ROPE_THETA = 10000.0
NORM_EPS = 1e-6

GQA_HEADS = 16
GQA_KV_HEADS = 4
GQA_GROUP = GQA_HEADS // GQA_KV_HEADS
GQA_Q_W = GQA_HEADS * HEAD_DIM
GQA_KV_W = GQA_KV_HEADS * HEAD_DIM

DIFF_HEADS = 8
DIFF_QK_W = DIFF_HEADS * 2 * HEAD_DIM
DIFF_V_DIM = 2 * HEAD_DIM
DIFF_BRANCH = DIFF_HEADS * DIFF_V_DIM

MLSTM_INNER = 2 * D_MODEL
MLSTM_HEADS = 4
MLSTM_HEAD_DIM = MLSTM_INNER // MLSTM_HEADS
MLSTM_QKV_BLOCK = 4
MLSTM_CONV_K = 4

LANES = 128
SUBLANES = 8
VMEM_LIMIT_BYTES = 56 * 1024 * 1024

TOKEN_TILE = 512
PROJ_N_TILE = 256
ATTN_Q_TILE = 256
MLSTM_CHUNK = 256
HEADWISE_CHUNK = LANES
GATE_W = 4 * MLSTM_HEADS

F32 = jnp.float32
BF16 = jnp.bfloat16


def _params(*semantics):
    return pltpu.CompilerParams(dimension_semantics=semantics, vmem_limit_bytes=VMEM_LIMIT_BYTES)


def _silu(x):
    return x * (1.0 / (1.0 + jnp.exp(-x)))


def _dot(a, b):
    return jnp.dot(a, b, preferred_element_type=F32)


def _dot_nt(a, b):
    return lax.dot_general(a, b, (((1,), (1,)), ((), ())), preferred_element_type=F32)


class TokenLayout:
    def __init__(self, n_prompt, seq, tile):
        self.prompt_tiles = n_prompt // tile
        self.tiles_per_seq = seq // tile

    def is_prompt(self, m):
        return m < self.prompt_tiles

    def cond_row(self, m):
        return jnp.where(m < self.prompt_tiles, 0, 1 + (m - self.prompt_tiles) // self.tiles_per_seq)

    def prompt_block(self, m):
        return jnp.minimum(m, self.prompt_tiles - 1)

    def sample_block(self, m):
        return jnp.maximum(m - self.prompt_tiles, 0)


def _mod_kernel(cond_ref, w_ref, b_ref, o_ref):
    a = _silu(cond_ref[...]).astype(BF16)
    o_ref[...] = _dot(a, w_ref[...].astype(BF16)) + b_ref[...]


def adaln_all(cond, w_mod, b_mod):
    tn = 1024
    return pl.pallas_call(
        _mod_kernel,
        grid=(DEPTH, 3 * D_MODEL // tn),
        in_specs=[
            pl.BlockSpec((SUBLANES, D_MODEL), lambda i, n: (0, 0)),
            pl.BlockSpec((None, D_MODEL, tn), lambda i, n: (i, 0, n)),
            pl.BlockSpec((None, 1, tn), lambda i, n: (i, 0, n)),
        ],
        out_specs=pl.BlockSpec((None, SUBLANES, tn), lambda i, n: (i, 0, n)),
        out_shape=jax.ShapeDtypeStruct((DEPTH, SUBLANES, 3 * D_MODEL), F32),
        compiler_params=_params("parallel", "parallel"),
        name="adaln_mod",
    )(cond, w_mod, b_mod.reshape(DEPTH, 1, 3 * D_MODEL))


def _swap_half_heads(y):
    w = y.shape[-1]
    lane = lax.broadcasted_iota(jnp.int32, y.shape, 1)
    upper = (lane & (HEAD_DIM // 2)) != 0
    return jnp.where(upper, pltpu.roll(y, HEAD_DIM // 2, 1), pltpu.roll(y, w - HEAD_DIM // 2, 1))


def _x_specs(x, lay, tm):
    if isinstance(x, tuple):
        return [pl.BlockSpec((tm, D_MODEL), lambda m: (lay.prompt_block(m), 0)),
                pl.BlockSpec((tm, D_MODEL), lambda m: (lay.sample_block(m), 0))], list(x)
    return [pl.BlockSpec((tm, D_MODEL), lambda m: (m, 0))], [x]


def _resident_spec(stacked, layer):
    return pl.BlockSpec((None,) + stacked.shape[1:], lambda m: (layer, 0, 0), pipeline_mode=pl.Buffered(1))


def _two_paths(is_prompt, path):
    pl.when(is_prompt)(functools.partial(path, True))
    pl.when(jnp.logical_not(is_prompt))(functools.partial(path, False))


def _proj_in_kernel(*refs, segs, layout, has_norm, has_rope, x_pair):
    it = iter(refs)
    x_refs = (next(it), next(it)) if x_pair else (next(it),) * 2
    g_ref, mod_ref, w_ref = next(it), next(it), next(it)
    if has_rope:
        cos_ref, sin_ref = next(it), next(it)
    if has_norm:
        gmat_ref, qg_ref, kg_ref = next(it), next(it), next(it)
    out_refs = []
    for _, _, kind, _ in segs:
        pair = (next(it), next(it)) if kind != "full" else (next(it),) * 2
        out_refs.append(pair + ((next(it),) if kind == "cache" else ()))
    cw = PROJ_N_TILE

    def store_cache(ref, acc, col0):
        seq_len = ref.shape[2]
        for s in range(ref.shape[0]):
            ref[s, col0:col0 + cw, :] = acc[s * seq_len:(s + 1) * seq_len, :].T

    def path(prompt):
        x = x_refs[0 if prompt else 1][...]
        y = x * lax.rsqrt(jnp.mean(x * x, axis=-1, keepdims=True) + NORM_EPS) * g_ref[...]
        shift = mod_ref[:, 0:D_MODEL]
        scale = mod_ref[:, D_MODEL:2 * D_MODEL]
        h = (y * (1.0 + scale) + shift).astype(BF16)
        n_chunks = sum(s[0] for s in segs) // cw
        accs = [_dot(h, w_ref[:, i * cw:(i + 1) * cw].astype(BF16)) for i in range(n_chunks)]
        col = 0
        for (width, mode, kind, trail), o_refs in zip(segs, out_refs):
            o_ref = o_refs[0 if prompt else 1]
            for c in range(width // cw):
                acc = accs[col // cw]
                col += cw
                if mode in ("q_norm", "k_norm"):
                    ss = _dot((acc * acc).astype(BF16), gmat_ref[...])
                    gvec = qg_ref[...] if mode == "q_norm" else kg_ref[...]
                    acc = acc * lax.rsqrt(ss * (1.0 / HEAD_DIM) + NORM_EPS) * gvec
                if has_rope and mode != "plain" and not prompt:
                    acc = acc * cos_ref[...] + _swap_half_heads(acc) * sin_ref[...]
                if mode in ("q_norm", "q"):
                    acc = acc * (HEAD_DIM ** -0.5)
                o_ref[:, c * cw:(c + 1) * cw] = acc.astype(o_ref.dtype)
                if kind == "cache" and prompt:
                    store_cache(o_refs[2], acc, c * cw)

    _two_paths(layout.is_prompt(pl.program_id(0)), path)


def proj_in(x, norm_g, mod4, layer, w, w_layer, segs, n_prompt, sample_seq, rope_tabs=None, head_norm=None,
            tile=TOKEN_TILE):
    tm, cw = tile, PROJ_N_TILE
    lay = TokenLayout(n_prompt, sample_seq, tm)
    T = x[0].shape[0] + x[1].shape[0] if isinstance(x, tuple) else x.shape[0]
    N = w.shape[2]
    assert N == sum(s[0] for s in segs) and all(s[0] % cw == 0 for s in segs)

    in_specs, args = _x_specs(x, lay, tm)
    in_specs += [
        pl.BlockSpec((1, D_MODEL), lambda m: (0, 0)),
        pl.BlockSpec((None, None, 1, 3 * D_MODEL), lambda m: (layer, lay.cond_row(m), 0, 0)),
        _resident_spec(w, w_layer),
    ]
    args += [norm_g.reshape(1, D_MODEL), mod4, w]
    if rope_tabs is not None:
        def pos_block(m):
            return (lay.sample_block(m) % lay.tiles_per_seq, 0)
        in_specs += [pl.BlockSpec((tm, cw), pos_block), pl.BlockSpec((tm, cw), pos_block)]
        args += list(rope_tabs)
    if head_norm is not None:
        in_specs += [pl.BlockSpec((cw, cw), lambda m: (0, 0)),
                     pl.BlockSpec((1, cw), lambda m: (0, 0)),
                     pl.BlockSpec((1, cw), lambda m: (0, 0))]
        args += list(head_norm)

    out_specs, out_shape, kern_segs = [], [], []
    for width, mode, out in segs:
        if isinstance(out, (str, tuple)):
            cache = isinstance(out, tuple)
            kern_segs.append((width, mode, "cache" if cache else "split", None))
            out_specs += [pl.BlockSpec((tm, width), lambda m: (lay.prompt_block(m), 0)),
                          pl.BlockSpec((tm, width), lambda m: (lay.sample_block(m), 0))]
            out_shape += [jax.ShapeDtypeStruct((n_prompt, width), BF16 if cache else F32),
                          jax.ShapeDtypeStruct((T - n_prompt, width), BF16)]
            if cache:
                seq_len = out[1]
                out_specs.append(pl.BlockSpec((tm // seq_len, None, width, seq_len),
                                              lambda m: (lay.prompt_block(m), 0, 0, 0)))
                out_shape.append(jax.ShapeDtypeStruct((n_prompt // seq_len, 1, width, seq_len), F32))
        else:
            kern_segs.append((width, mode, "full", None))
            out_specs.append(pl.BlockSpec((tm, width), lambda m: (m, 0)))
            out_shape.append(jax.ShapeDtypeStruct((T, width), out))

    kern = functools.partial(_proj_in_kernel, segs=tuple(kern_segs), layout=lay, x_pair=isinstance(x, tuple),
                             has_norm=head_norm is not None, has_rope=rope_tabs is not None)
    return pl.pallas_call(
        kern,
        grid=(T // tm,),
        in_specs=in_specs,
        out_specs=out_specs,
        out_shape=out_shape,
        compiler_params=_params("arbitrary"),
        name="proj_in",
    )(*args)


def _proj_out_kernel(*refs, layout, with_skip, x_pair, final):
    it = iter(refs)
    o_refs = (next(it), next(it))
    if with_skip:
        xc_ref, skip_ref = next(it), next(it)
    z_ref = next(it)
    x_refs = (next(it), next(it)) if x_pair else (next(it),) * 2
    mod_ref, w_ref = next(it), next(it)
    if final:
        fg_ref = next(it)
    out_refs = (next(it), next(it)) if final else (next(it),) * 2

    def path(prompt):
        sel = 0 if prompt else 1
        a = o_refs[sel][...].astype(F32)
        if with_skip:
            a = a + skip_ref[...] * xc_ref[...].astype(F32)
        a = (a * _silu(z_ref[...].astype(F32))).astype(BF16)
        gate = mod_ref[:, 2 * D_MODEL:3 * D_MODEL]
        x = x_refs[sel][...] + gate * _dot(a, w_ref[...].astype(BF16))
        if final:
            x = x * lax.rsqrt(jnp.mean(x * x, axis=-1, keepdims=True) + NORM_EPS) * fg_ref[...]
        out_refs[sel][...] = x

    _two_paths(layout.is_prompt(pl.program_id(0)), path)


def proj_out(o_prompt, o_sample, z, x, mod4, layer, w_out, w_layer, n_prompt, sample_seq, xc=None, skip=None,
             final_g=None, tile=TOKEN_TILE):
    T = z.shape[0]
    K = o_prompt.shape[1]
    tm = tile
    lay = TokenLayout(n_prompt, sample_seq, tm)
    row = lambda m: (m, 0)
    prompt_row = lambda m: (lay.prompt_block(m), 0)
    sample_row = lambda m: (lay.sample_block(m), 0)
    in_specs = [pl.BlockSpec((tm, K), prompt_row), pl.BlockSpec((tm, K), sample_row)]
    args = [o_prompt, o_sample]
    if xc is not None:
        in_specs += [pl.BlockSpec((tm, K), row), pl.BlockSpec((1, K), lambda m: (0, 0))]
        args += [xc, skip.reshape(1, K)]
    in_specs.append(pl.BlockSpec((tm, K), row))
    args.append(z)
    x_specs, x_args = _x_specs(x, lay, tm)
    in_specs += x_specs + [
        pl.BlockSpec((None, None, 1, 3 * D_MODEL), lambda m: (layer, lay.cond_row(m), 0, 0)),
        _resident_spec(w_out, w_layer),
    ]
    args += x_args + [mod4, w_out]
    if final_g is not None:
        in_specs.append(pl.BlockSpec((1, D_MODEL), lambda m: (0, 0)))
        args.append(final_g.reshape(1, D_MODEL))
        out_specs = [pl.BlockSpec((tm, D_MODEL), prompt_row), pl.BlockSpec((tm, D_MODEL), sample_row)]
        out_shape = [jax.ShapeDtypeStruct((n_prompt, D_MODEL), F32),
                     jax.ShapeDtypeStruct((T - n_prompt, D_MODEL), F32)]
    else:
        out_specs = pl.BlockSpec((tm, D_MODEL), row)
        out_shape = jax.ShapeDtypeStruct((T, D_MODEL), F32)
    kern = functools.partial(_proj_out_kernel, layout=lay, with_skip=xc is not None,
                             x_pair=isinstance(x, tuple), final=final_g is not None)
    return pl.pallas_call(
        kern,
        grid=(T // tm,),
        in_specs=in_specs,
        out_specs=out_specs,
        out_shape=out_shape,
        compiler_params=_params("arbitrary"),
        name="proj_out",
    )(*args)


def _low_half(shape):
    return lax.broadcasted_iota(jnp.int32, shape, 1) < HEAD_DIM


def _attend_group(items, rows):
    scores = [[_dot_nt(q, k) for k in k_parts] for q, k_parts, _, _ in items]
    probs = []
    for s in scores:
        m = functools.reduce(jnp.maximum, [jnp.max(x, axis=-1, keepdims=True) for x in s])
        probs.append([jnp.exp(x - m).astype(BF16) for x in s])
    out = []
    for p_parts, (_, _, v_ref, head) in zip(probs, items):
        acc = None
        for p, r in zip(p_parts, rows):
            t = _dot(p, v_ref[head, r, :])
            acc = t if acc is None else acc + t
        out.append(acc)
    return out


def _group_size(n_keys):
    return 16 if n_keys <= 512 else 4


def _row_parts(k_ref, kc_ref):
    n_new = k_ref.shape[0]
    rows = [slice(0, n_new)]
    if kc_ref is not None:
        rows.append(slice(n_new, n_new + kc_ref.shape[0]))
    return rows


def _gqa_attn_kernel(*refs, has_ctx):
    if has_ctx:
        q_ref, k_ref, v_ref, kc_ref, vc_ref, o_ref, ve_ref, vo_ref = refs
    else:
        q_ref, k_ref, v_ref, o_ref, ve_ref, vo_ref = refs
        kc_ref = vc_ref = None
    rows = _row_parts(k_ref, kc_ref)

    @pl.when(pl.program_id(1) == 0)
    def _():
        for src, r in zip([v_ref, vc_ref][:len(rows)], rows):
            for c in range(GQA_KV_HEADS):
                slab = src[:, (c // 2) * LANES:(c // 2 + 1) * LANES].astype(F32)
                low = _low_half(slab.shape)
                if c % 2 == 0:
                    ve = jnp.where(low, slab, 0.0)
                    vo = pltpu.roll(ve, HEAD_DIM, 1)
                else:
                    vo = jnp.where(low, 0.0, slab)
                    ve = pltpu.roll(vo, HEAD_DIM, 1)
                ones_e = jnp.where(low, 1.0, 0.0)
                ve_ref[c, r, 0:LANES] = ve.astype(BF16)
                ve_ref[c, r, LANES:2 * LANES] = ones_e.astype(BF16)
                vo_ref[c, r, 0:LANES] = vo.astype(BF16)
                vo_ref[c, r, LANES:2 * LANES] = (1.0 - ones_e).astype(BF16)

    k_srcs = [k_ref, kc_ref][:len(rows)]
    group = _group_size(rows[-1].stop)
    n_pairs = GQA_HEADS // 2
    for p0 in range(0, n_pairs, group // 2):
        items = []
        for pair in range(p0, p0 + group // 2):
            c = pair // (GQA_GROUP // 2)
            q = q_ref[:, pair * LANES:(pair + 1) * LANES].astype(F32)
            low = _low_half(q.shape)
            q_sw = pltpu.roll(q, HEAD_DIM, 1)
            if c % 2 == 0:
                qe, qo = jnp.where(low, q, 0.0), jnp.where(low, q_sw, 0.0)
            else:
                qe, qo = jnp.where(low, 0.0, q_sw), jnp.where(low, 0.0, q)
            k_parts = [s[:, (c // 2) * LANES:(c // 2 + 1) * LANES].astype(BF16) for s in k_srcs]
            items += [(qe.astype(BF16), k_parts, ve_ref, c), (qo.astype(BF16), k_parts, vo_ref, c)]
        accs = _attend_group(items, rows)
        for i, pair in enumerate(range(p0, p0 + group // 2)):
            acc = accs[2 * i] + accs[2 * i + 1]
            o_ref[:, pair * LANES:(pair + 1) * LANES] = (acc[:, 0:LANES] / acc[:, LANES:2 * LANES]).astype(o_ref.dtype)


def _diff_attn_kernel(*refs, has_ctx, lam_init):
    if has_ctx:
        q_ref, k_ref, v_ref, kc_ref, vc_ref, lam_ref, g_ref, o_ref, vx_ref = refs
    else:
        q_ref, k_ref, v_ref, lam_ref, g_ref, o_ref, vx_ref = refs
        kc_ref = vc_ref = None
    rows = _row_parts(k_ref, kc_ref)

    @pl.when(pl.program_id(1) == 0)
    def _():
        for src, r in zip([v_ref, vc_ref][:len(rows)], rows):
            for h in range(DIFF_HEADS):
                vx_ref[h, r, 0:LANES] = src[:, h * LANES:(h + 1) * LANES].astype(BF16)
                vx_ref[h, r, LANES:2 * LANES] = jnp.ones((r.stop - r.start, LANES), BF16)

    lv = lam_ref[...]
    s1 = jnp.sum(lv[0:1] * lv[1:2], axis=-1, keepdims=True)
    s2 = jnp.sum(lv[2:3] * lv[3:4], axis=-1, keepdims=True)
    lam = jnp.exp(s1) - jnp.exp(s2) + lam_init
    k_srcs = [k_ref, kc_ref][:len(rows)]
    group = _group_size(rows[-1].stop)
    for h0 in range(0, DIFF_HEADS, group // 2):
        items = []
        for h in range(h0, h0 + group // 2):
            q = q_ref[:, h * LANES:(h + 1) * LANES].astype(F32)
            low = _low_half(q.shape)
            k_parts = [s[:, h * LANES:(h + 1) * LANES].astype(BF16) for s in k_srcs]
            items += [(jnp.where(low, q, 0.0).astype(BF16), k_parts, vx_ref, h),
                      (jnp.where(low, 0.0, q).astype(BF16), k_parts, vx_ref, h)]
        accs = _attend_group(items, rows)
        for i, h in enumerate(range(h0, h0 + group // 2)):
            a0, a1 = accs[2 * i], accs[2 * i + 1]
            o = a0[:, 0:LANES] / a0[:, LANES:2 * LANES] - lam * (a1[:, 0:LANES] / a1[:, LANES:2 * LANES])
            o = o * lax.rsqrt(jnp.mean(o * o, axis=-1, keepdims=True) + NORM_EPS) * g_ref[...]
            o_ref[:, h * LANES:(h + 1) * LANES] = (o * (1.0 - lam_init)).astype(o_ref.dtype)


def attention(kernel, v_scratch, q, k, v, batch, sq, q_row0, ctx=None, extra=()):
    tq = ATTN_Q_TILE
    nq = sq // tq
    off = q_row0 // tq
    in_specs = [
        pl.BlockSpec((tq, q.shape[1]), lambda b, i: (b * nq + i + off, 0)),
        pl.BlockSpec((sq, k.shape[1]), lambda b, i: (b, 0)),
        pl.BlockSpec((sq, v.shape[1]), lambda b, i: (b, 0)),
    ]
    args = [q, k, v]
    sk = sq
    if ctx is not None:
        kc, vc, layer = ctx
        sk += kc.shape[2]
        in_specs += [pl.BlockSpec((None, None) + kc.shape[2:], lambda b, i: (b, layer, 0, 0)),
                     pl.BlockSpec((None, None) + vc.shape[2:], lambda b, i: (b, layer, 0, 0))]
        args += [kc, vc]
    for a in extra:
        in_specs.append(pl.BlockSpec(a.shape, lambda b, i: (0, 0)))
    args += list(extra)
    return pl.pallas_call(
        functools.partial(kernel, has_ctx=ctx is not None),
        grid=(batch, nq),
        in_specs=in_specs,
        out_specs=pl.BlockSpec((tq, D_MODEL), lambda b, i: (b * nq + i, 0)),
        out_shape=jax.ShapeDtypeStruct((batch * sq, D_MODEL), BF16),
        scratch_shapes=[pltpu.VMEM((heads, sk, width), BF16) for heads, width in v_scratch],
        compiler_params=_params("parallel", "arbitrary"),
        name="attention",
    )(*args)


def _log_sigmoid(x):
    return jnp.minimum(x, 0.0) - jnp.log1p(jnp.exp(-jnp.abs(x)))


def _mlstm_in_kernel(cur_ref, prev_ref, next_ref, ng_ref, mod_ref, w_ref, cw_ref, cb_ref, bq_ref, bk_ref, bkt_ref,
                     bv_ref, wg_ref, bg_ref, xc_ref, q_ref, kt_ref, v_ref, gcol_ref, grow_ref, z_ref,
                     *, n_prompt_tiles, tiles_per_seq):
    L = MLSTM_CHUNK
    cw = PROJ_N_TILE
    t = pl.program_id(0)
    st = jnp.maximum(t - n_prompt_tiles, 0) % tiles_per_seq
    first = (t < n_prompt_tiles) | (st == 0)
    last = (t < n_prompt_tiles) | (st == tiles_per_seq - 1)
    n_ext = L + 2 * SUBLANES
    x = jnp.concatenate([prev_ref[...], cur_ref[...], next_ref[...]], axis=0)
    y = x * lax.rsqrt(jnp.mean(x * x, axis=-1, keepdims=True) + NORM_EPS) * ng_ref[...]
    h = (y * (1.0 + mod_ref[:, D_MODEL:2 * D_MODEL]) + mod_ref[:, 0:D_MODEL]).astype(BF16)
    n_xm = MLSTM_INNER // cw
    xm_chunks = [_dot(h, w_ref[:, i * cw:(i + 1) * cw].astype(BF16)) for i in range(n_xm)]
    z_chunks = [_dot(h[SUBLANES:SUBLANES + L], w_ref[:, (n_xm + i) * cw:(n_xm + i + 1) * cw].astype(BF16))
                for i in range(n_xm)]
    for i, zc in enumerate(z_chunks):
        z_ref[:, i * cw:(i + 1) * cw] = zc.astype(z_ref.dtype)
    row_e = lax.broadcasted_iota(jnp.int32, (n_ext, 1), 0)
    inside = ((row_e >= SUBLANES) | jnp.logical_not(first)) & ((row_e < SUBLANES + L) | jnp.logical_not(last))
    ext = jnp.where(inside, jnp.concatenate(xm_chunks, axis=1), 0.0)
    cur = ext[SUBLANES:SUBLANES + L]
    pad_l = MLSTM_CONV_K // 2
    acc = cb_ref[...] + cw_ref[pad_l:pad_l + 1, :] * cur
    for j in range(MLSTM_CONV_K):
        if j == pad_l:
            continue
        sh = (pad_l - j) % n_ext
        acc = acc + cw_ref[j:j + 1, :] * pltpu.roll(ext, sh, 0)[SUBLANES:SUBLANES + L]
    xc = _silu(acc)
    xc_ref[...] = xc.astype(xc_ref.dtype)

    g = jnp.zeros((L, LANES), F32)
    for c in range(MLSTM_INNER // HEADWISE_CHUNK):
        sl = slice(c * HEADWISE_CHUNK, (c + 1) * HEADWISE_CHUNK)
        xcb = xc[:, sl].astype(BF16)
        xmb = cur[:, sl].astype(BF16)
        qc = _dot(xcb, bq_ref[c]).astype(BF16)
        kc = _dot(xcb, bk_ref[c]).astype(BF16)
        vc = _dot(xmb, bv_ref[c]).astype(BF16)
        q_ref[:, sl] = qc
        v_ref[:, sl] = vc
        kt_ref[sl, :] = _dot_nt(bkt_ref[c], xcb).astype(BF16)
        g = g + _dot(qc, wg_ref[0, sl, :]) + _dot(kc, wg_ref[1, sl, :]) + _dot(vc, wg_ref[2, sl, :])
    g = g + bg_ref[...]

    ls = _log_sigmoid(g)
    row = lax.broadcasted_iota(jnp.int32, (L, LANES), 0)
    pre, suf = ls, ls
    s = 1
    while s < L:
        pre = pre + jnp.where(row >= s, pltpu.roll(pre, s, 0), 0.0)
        suf = suf + jnp.where(row < L - s, pltpu.roll(suf, L - s, 0), 0.0)
        s *= 2
    lane = lax.broadcasted_iota(jnp.int32, (L, LANES), 1)
    H = MLSTM_HEADS
    out = jnp.where((lane >= H) & (lane < 2 * H), pre, jnp.where((lane >= 3 * H) & (lane < 4 * H), suf, g))
    gcol_ref[...] = out[:, 0:GATE_W]
    grow_ref[...] = out.T[0:GATE_W, :]


def mlstm_in(x, norm_g, mod4, layer, w, w_layer, conv_w, conv_b, bq, bk, bkt, bv, wg, bg, n_prompt, sample_seq):
    T = x.shape[0]
    L = MLSTM_CHUNK
    n_tiles = T // L
    r8 = L // SUBLANES
    n_blk8 = T // SUBLANES
    C = MLSTM_INNER
    lay = TokenLayout(n_prompt, sample_seq, L)
    kern = functools.partial(_mlstm_in_kernel, n_prompt_tiles=n_prompt // L, tiles_per_seq=sample_seq // L)
    const3 = lambda t: (0, 0, 0)
    return pl.pallas_call(
        kern,
        grid=(n_tiles,),
        in_specs=[
            pl.BlockSpec((L, D_MODEL), lambda t: (t, 0)),
            pl.BlockSpec((SUBLANES, D_MODEL), lambda t: (jnp.maximum(t * r8 - 1, 0), 0)),
            pl.BlockSpec((SUBLANES, D_MODEL), lambda t: (jnp.minimum((t + 1) * r8, n_blk8 - 1), 0)),
            pl.BlockSpec((1, D_MODEL), lambda t: (0, 0)),
            pl.BlockSpec((None, None, 1, 3 * D_MODEL), lambda t: (layer, lay.cond_row(t), 0, 0)),
            _resident_spec(w, w_layer),
            pl.BlockSpec((MLSTM_CONV_K, C), lambda t: (0, 0)),
            pl.BlockSpec((1, C), lambda t: (0, 0)),
            pl.BlockSpec(bq.shape, const3),
            pl.BlockSpec(bk.shape, const3),
            pl.BlockSpec(bkt.shape, const3),
            pl.BlockSpec(bv.shape, const3),
            pl.BlockSpec(wg.shape, const3),
            pl.BlockSpec((1, LANES), lambda t: (0, 0)),
        ],
        out_specs=[
            pl.BlockSpec((L, C), lambda t: (t, 0)),
            pl.BlockSpec((L, C), lambda t: (t, 0)),
            pl.BlockSpec((None, C, L), lambda t: (t, 0, 0)),
            pl.BlockSpec((L, C), lambda t: (t, 0)),
            pl.BlockSpec((L, GATE_W), lambda t: (t, 0)),
            pl.BlockSpec((None, GATE_W, L), lambda t: (t, 0, 0)),
            pl.BlockSpec((L, C), lambda t: (t, 0)),
        ],
        out_shape=[
            jax.ShapeDtypeStruct((T, C), BF16),
            jax.ShapeDtypeStruct((T, C), BF16),
            jax.ShapeDtypeStruct((n_tiles, C, L), BF16),
            jax.ShapeDtypeStruct((T, C), BF16),
            jax.ShapeDtypeStruct((T, GATE_W), F32),
            jax.ShapeDtypeStruct((n_tiles, GATE_W, L), F32),
            jax.ShapeDtypeStruct((T, C), BF16),
        ],
        compiler_params=_params("parallel"),
        name="mlstm_in",
    )(x, x, x, norm_g.reshape(1, D_MODEL), mod4, w, conv_w, conv_b.reshape(1, C), bq, bk, bkt, bv, wg, bg)


def _gate_cols(gcol, h):
    lane = lax.broadcasted_iota(jnp.int32, gcol.shape, 1)
    pick = lambda idx: jnp.sum(jnp.where(lane == idx, gcol, 0.0), axis=-1, keepdims=True)
    return pick(MLSTM_HEADS + h), pick(3 * MLSTM_HEADS + h)


def _chunk_weights(qk, b_col, b_row, i_row, tot, causal_mask, m_prev):
    ar = i_row - b_row
    d = jnp.where(causal_mask, b_col + ar, -jnp.inf)
    inter = b_col + m_prev
    mt = jnp.maximum(jnp.max(d, axis=-1, keepdims=True), inter)
    s = qk * jnp.exp(d - mt)
    g = tot + ar
    m_new = jnp.maximum(tot + m_prev, jnp.max(g, axis=-1, keepdims=True))
    w = jnp.exp(g - m_new)
    w_hi = w.astype(BF16).astype(F32)
    rid = lax.broadcasted_iota(jnp.int32, (SUBLANES, w.shape[-1]), 0)
    w8 = jnp.where(rid == 0, w_hi, jnp.where(rid == 1, w - w_hi, 0.0)).astype(BF16)
    return dict(s=s.astype(BF16), den=jnp.sum(s, axis=-1, keepdims=True), mt=mt, inter=inter, w=w, w8=w8,
                m_new=m_new, decay_arg=tot + m_prev - m_new)


def _chunk_matmuls(wts, q, kt, v, C_prev):
    num = _dot(wts["s"], v)
    qC = None if C_prev is None else _dot(q, C_prev.astype(BF16))
    C_add = _dot((kt.astype(F32) * wts["w"]).astype(BF16), v)
    n8 = _dot_nt(wts["w8"], kt)
    return num, qC, C_add, n8[0:1] + n8[1:2]


def _chunk_finish(wts, mats, q, C_prev, n_prev):
    num, qC, C_new, n_new = mats
    den, mt = wts["den"], wts["mt"]
    if C_prev is not None:
        w_inter = jnp.exp(wts["inter"] - mt)
        num = num + w_inter * qC
        den = den + w_inter * jnp.sum(q.astype(F32) * n_prev, axis=-1, keepdims=True)
        decay = jnp.exp(wts["decay_arg"])
        C_new = decay * C_prev + C_new
        n_new = decay * n_prev + n_new
    h = num / jnp.maximum(jnp.abs(den), jnp.exp(-mt))
    return h, C_new, n_new, wts["m_new"]


def _chunk_dir(qk, q, kt, v, b_col, b_row, i_row, tot, causal_mask, m_prev, C_prev, n_prev):
    wts = _chunk_weights(qk, b_col, b_row, i_row, tot, causal_mask, m_prev)
    return _chunk_finish(wts, _chunk_matmuls(wts, q, kt, v, C_prev), q, C_prev, n_prev)


def _head_layernorm(h, g):
    mu = jnp.mean(h, axis=-1, keepdims=True)
    hc = h - mu
    var = jnp.mean(hc * hc, axis=-1, keepdims=True)
    return hc * lax.rsqrt(var + NORM_EPS) * g


def _chunk_masks(L):
    t_idx = lax.broadcasted_iota(jnp.int32, (L, L), 0)
    s_idx = lax.broadcasted_iota(jnp.int32, (L, L), 1)
    return s_idx <= t_idx, s_idx >= t_idx


def _mlstm_prompt_kernel(q_ref, kt_ref, v_ref, gcol_ref, grow_ref, mhg_ref,
                         hn_ref, C_ref, n_ref, m_ref):
    L = MLSTM_CHUNK
    H, DH = MLSTM_HEADS, MLSTM_HEAD_DIM
    fmask, bmask = _chunk_masks(L)
    zero = jnp.zeros((1, 1), F32)
    lane = lax.broadcasted_iota(jnp.int32, (1, LANES), 1)
    col = lambda idx: gcol_ref[:, idx:idx + 1]
    row = lambda idx: grow_ref[idx:idx + 1, :]
    for h in range(H):
        ch = slice(h * DH, (h + 1) * DH)
        q, kt, v = q_ref[:, ch], kt_ref[ch, :], v_ref[:, ch]
        qk = _dot(q, kt)
        bf_row, bb_row = row(H + h), row(3 * H + h)
        hf, Cf, nf, mf = _chunk_dir(qk, q, kt, v, col(H + h), bf_row, row(h), bf_row[:, L - 1:L], fmask,
                                    zero, None, None)
        hb, Cb, nb, mb = _chunk_dir(qk, q, kt, v, col(3 * H + h), bb_row, row(2 * H + h), bb_row[:, 0:1], bmask,
                                    zero, None, None)
        hn_ref[:, ch] = _head_layernorm(hf + hb, mhg_ref[:, ch]).astype(hn_ref.dtype)
        C_ref[0, h] = Cf
        C_ref[1, h] = Cb
        n_ref[0, h] = nf
        n_ref[1, h] = nb
        m_ref[h] = jnp.where(lane == 0, mf, jnp.where(lane == 1, mb, 0.0))


def mlstm_prompt(q, kt, v, gcol, grow, mh_g, batch):
    L, H, DH, C = MLSTM_CHUNK, MLSTM_HEADS, MLSTM_HEAD_DIM, MLSTM_INNER
    return pl.pallas_call(
        _mlstm_prompt_kernel,
        grid=(batch,),
        in_specs=[
            pl.BlockSpec((L, C), lambda b: (b, 0)),
            pl.BlockSpec((None, C, L), lambda b: (b, 0, 0)),
            pl.BlockSpec((L, C), lambda b: (b, 0)),
            pl.BlockSpec((L, GATE_W), lambda b: (b, 0)),
            pl.BlockSpec((None, GATE_W, L), lambda b: (b, 0, 0)),
            pl.BlockSpec((1, C), lambda b: (0, 0)),
        ],
        out_specs=[
            pl.BlockSpec((L, C), lambda b: (b, 0)),
            pl.BlockSpec((None, None, 2, H, DH, DH), lambda b: (b, 0, 0, 0, 0, 0)),
            pl.BlockSpec((None, 2, H, 1, DH), lambda b: (b, 0, 0, 0, 0)),
            pl.BlockSpec((None, H, 1, LANES), lambda b: (b, 0, 0, 0)),
        ],
        out_shape=[
            jax.ShapeDtypeStruct((batch * L, C), BF16),
            jax.ShapeDtypeStruct((batch, 1, 2, H, DH, DH), F32),
            jax.ShapeDtypeStruct((batch, 2, H, 1, DH), F32),
            jax.ShapeDtypeStruct((batch, H, 1, LANES), F32),
        ],
        compiler_params=_params("parallel"),
        name="mlstm_prompt",
    )(q, kt, v, gcol, grow, mh_g.reshape(1, C))


def _mlstm_sample_kernel(q_ref, kt_ref, v_ref, gcol_ref, grow_ref, mhg_ref, C0_ref, n0_ref, m0_ref,
                         hn_ref, hf_ref, hb_ref, C_ref, n_ref, m_ref, *, n_chunks):
    L = MLSTM_CHUNK
    H = MLSTM_HEADS
    h = pl.program_id(1)
    fmask, bmask = _chunk_masks(L)
    C_ref[...] = C0_ref[...]
    n_ref[...] = n0_ref[...]
    m_ref[...] = m0_ref[...]

    def one(c, direction):
        rows = pl.ds(pl.multiple_of(c * L, L), L)
        q, kt, v = q_ref[rows, :], kt_ref[c], v_ref[rows, :]
        qk = _dot(q, kt)
        gcol = gcol_ref[rows, :]
        b_col = _gate_cols(gcol, h)[direction]
        grow = lambda idx: grow_ref[c, pl.ds(idx, 1), :]
        b_row = grow((2 * direction + 1) * H + h)
        i_row = grow(2 * direction * H + h)
        tot = b_row[:, L - 1:L] if direction == 0 else b_row[:, 0:1]
        mask = fmask if direction == 0 else bmask
        m_prev = m_ref[direction][:, 0:1]
        hh, C_new, n_new, m_new = _chunk_dir(qk, q, kt, v, b_col, b_row, i_row, tot, mask, m_prev,
                                             C_ref[direction], n_ref[direction])
        C_ref[direction] = C_new
        n_ref[direction] = n_new
        m_ref[direction] = jnp.broadcast_to(m_new, (1, LANES))
        (hf_ref if direction == 0 else hb_ref)[rows, :] = hh

    def body(i, carry):
        one(i, 0)
        one(n_chunks - 1 - i, 1)
        return carry

    lax.fori_loop(0, n_chunks, body, 0)

    def norm(c, carry):
        rows = pl.ds(pl.multiple_of(c * L, L), L)
        hn_ref[rows, :] = _head_layernorm(hf_ref[rows, :] + hb_ref[rows, :], mhg_ref[...]).astype(hn_ref.dtype)
        return carry

    lax.fori_loop(0, n_chunks, norm, 0)


def mlstm_sample(q, kt, v, gcol, grow, mh_g, C0, n0, m0, n_prompt, batch, seq):
    L, H, DH = MLSTM_CHUNK, MLSTM_HEADS, MLSTM_HEAD_DIM
    nc = seq // L
    row_off = n_prompt // seq
    tile_off = n_prompt // L // nc
    return pl.pallas_call(
        functools.partial(_mlstm_sample_kernel, n_chunks=nc),
        grid=(batch, H),
        in_specs=[
            pl.BlockSpec((seq, DH), lambda b, h: (b + row_off, h)),
            pl.BlockSpec((nc, DH, L), lambda b, h: (b + tile_off, h, 0)),
            pl.BlockSpec((seq, DH), lambda b, h: (b + row_off, h)),
            pl.BlockSpec((seq, GATE_W), lambda b, h: (b + row_off, 0)),
            pl.BlockSpec((nc, GATE_W, L), lambda b, h: (b + tile_off, 0, 0)),
            pl.BlockSpec((1, DH), lambda b, h: (0, h)),
            pl.BlockSpec((None, 2, None, DH, DH), lambda b, h: (b, 0, h, 0, 0)),
            pl.BlockSpec((None, 2, None, 1, DH), lambda b, h: (b, 0, h, 0, 0)),
            pl.BlockSpec((None, None, 2, 1, LANES), lambda b, h: (b, h, 0, 0, 0)),
        ],
        out_specs=pl.BlockSpec((seq, DH), lambda b, h: (b, h)),
        out_shape=jax.ShapeDtypeStruct((batch * seq, MLSTM_INNER), BF16),
        scratch_shapes=[
            pltpu.VMEM((seq, DH), F32),
            pltpu.VMEM((seq, DH), F32),
            pltpu.VMEM((2, DH, DH), F32),
            pltpu.VMEM((2, 1, DH), F32),
            pltpu.VMEM((2, 1, LANES), F32),
        ],
        compiler_params=_params("parallel", "parallel"),
        name="mlstm_sample",
    )(q, kt, v, gcol, grow, mh_g.reshape(1, MLSTM_INNER), C0, n0, m0)


def _rope_tables(seq):
    t = jnp.arange(seq)
    row = (t // GRID_W).astype(F32)
    col = (t % GRID_W).astype(F32)
    n_freq = HEAD_DIM // 4
    inv = ROPE_THETA ** (-jnp.arange(n_freq, dtype=F32) / n_freq)
    ang = jnp.concatenate([row[:, None] * inv, col[:, None] * inv], axis=-1)
    cos, sin = jnp.cos(ang), jnp.sin(ang)
    reps = PROJ_N_TILE // HEAD_DIM
    return (jnp.tile(jnp.concatenate([cos, cos], axis=-1), (1, reps)),
            jnp.tile(jnp.concatenate([-sin, sin], axis=-1), (1, reps)))


def _block_diag(w):
    blk, ch = MLSTM_QKV_BLOCK, HEADWISE_CHUNK
    w4 = w.reshape(-1, ch // blk, blk, blk)
    rows = w4.transpose(0, 2, 1, 3).reshape(-1, 1, blk, ch)
    full = jnp.broadcast_to(rows, (rows.shape[0], ch // blk, blk, ch)).reshape(-1, ch, ch)
    block_of = jnp.arange(ch) // blk
    return jnp.where(block_of[:, None] == block_of[None, :], full, 0.0)


def _diff_lambda_init(layer_idx):
    return 0.8 - 0.6 * math.exp(-0.3 * layer_idx)


def kernel(x_prompt, x_sample, cache_gqa_k, cache_gqa_v, cache_diff_k, cache_diff_v, state_mlstm_C, state_mlstm_n, state_mlstm_m, c, c_ctx, norm_g, w_mod, b_mod, gqa_w_in, gqa_q_norm_g, gqa_k_norm_g, gqa_w_out, diff_w_in, diff_lambda_q1, diff_lambda_k1, diff_lambda_q2, diff_lambda_k2, diff_subln_g, diff_w_out, mlstm_w_in, mlstm_conv_w, mlstm_conv_b, mlstm_w_q, mlstm_w_k, mlstm_w_v, mlstm_w_gate_f, mlstm_b_gate_f, mlstm_w_gate_b, mlstm_b_gate_b, mlstm_mh_norm_g, mlstm_skip, mlstm_w_out, final_norm_g):
    Bp, Sp, D = x_prompt.shape
    Bs, Ss, _ = x_sample.shape
    Tp, Ts = Bp * Sp, Bs * Ss
    assert Sp == MLSTM_CHUNK and Ss % TOKEN_TILE == 0 and Tp % TOKEN_TILE == 0
    H, DH = MLSTM_HEADS, MLSTM_HEAD_DIM
    past = cache_gqa_k.shape[2]

    x = (x_prompt.reshape(Tp, D), x_sample.reshape(Ts, D))
    cond = jnp.zeros((SUBLANES, D), F32).at[0].set(c_ctx).at[1:1 + Bs].set(c)
    mod4 = adaln_all(cond, w_mod, b_mod).reshape(DEPTH, SUBLANES, 1, 3 * D)
    rope_tabs = _rope_tables(Ss)
    gi = jnp.arange(PROJ_N_TILE) // HEAD_DIM
    gmat = (gi[:, None] == gi[None, :]).astype(BF16)
    tile_g = lambda g: jnp.tile(g, PROJ_N_TILE // HEAD_DIM).reshape(1, PROJ_N_TILE)
    ctx_gqa = (cache_gqa_k.reshape(Bs, -1, past, GQA_KV_W), cache_gqa_v.reshape(Bs, -1, past, GQA_KV_W))
    ctx_diff = (cache_diff_k.reshape(Bs, -1, past, DIFF_QK_W), cache_diff_v.reshape(Bs, -1, past, DIFF_BRANCH))

    gqa_k_list, gqa_v_list, diff_k_list, diff_v_list = [], [], [], []
    mC_list, mn_list, mm_list = [], [], []
    for i in range(DEPTH):
        kind, j = i % N_MIXERS, i // N_MIXERS
        common = dict(n_prompt=Tp, sample_seq=Ss)
        last = dict(final_g=final_norm_g) if i == DEPTH - 1 else {}
        if kind == 0:
            segs = ((GQA_Q_W, "q_norm", BF16), (GQA_KV_W, "k_norm", ("cache", Sp)),
                    (GQA_KV_W, "plain", ("cache", Sp)), (GQA_Q_W, "plain", BF16))
            q, kp, ks, k_cache, vp, vs, v_cache, z = proj_in(
                x, norm_g[i], mod4, i, gqa_w_in, j, segs, rope_tabs=rope_tabs,
                head_norm=(gmat, tile_g(gqa_q_norm_g[j]), tile_g(gqa_k_norm_g[j])), **common)
            scr = [(GQA_KV_HEADS, 2 * LANES)] * 2
            kern = _gqa_attn_kernel
            o_p = attention(kern, scr, q, kp, vp, Bp, Sp, 0)
            o_s = attention(kern, scr, q, ks, vs, Bs, Ss, Tp, ctx=(*ctx_gqa, j))
            x = proj_out(o_p, o_s, z, x, mod4, i, gqa_w_out, j, **common, **last)
            gqa_k_list.append(k_cache)
            gqa_v_list.append(v_cache)
        elif kind == 1:
            lam_init = _diff_lambda_init(i)
            segs = ((DIFF_QK_W, "q", BF16), (DIFF_QK_W, "k", ("cache", Sp)),
                    (DIFF_BRANCH, "plain", "split"), (DIFF_BRANCH, "plain", BF16))
            q, kp, ks, k_cache, vp, vs, z = proj_in(x, norm_g[i], mod4, i, diff_w_in, j, segs,
                                                    rope_tabs=rope_tabs, **common)
            lam_vecs = jnp.stack([diff_lambda_q1[j], diff_lambda_k1[j], diff_lambda_q2[j], diff_lambda_k2[j]])
            extra = (lam_vecs, diff_subln_g[j].reshape(1, DIFF_V_DIM))
            scr = [(DIFF_HEADS, 2 * LANES)]
            kern = functools.partial(_diff_attn_kernel, lam_init=lam_init)
            o_p = attention(kern, scr, q, kp, vp, Bp, Sp, 0, extra=extra)
            o_s = attention(kern, scr, q, ks, vs, Bs, Ss, Tp, ctx=(*ctx_diff, j), extra=extra)
            x = proj_out(o_p, o_s, z, x, mod4, i, diff_w_out, j, **common, **last)
            diff_k_list.append(k_cache)
            diff_v_list.append(vp.reshape(Bp, 1, Sp, DIFF_HEADS, DIFF_V_DIM))
        else:
            bq = _block_diag(mlstm_w_q[j]).astype(BF16)
            bk = _block_diag(mlstm_w_k[j])
            bkt = (bk * (DH ** -0.5)).transpose(0, 2, 1).astype(BF16)
            bv = _block_diag(mlstm_w_v[j]).astype(BF16)
            wg = jnp.concatenate([mlstm_w_gate_f[j][:, :H], mlstm_w_gate_f[j][:, H:],
                                  mlstm_w_gate_b[j][:, :H], mlstm_w_gate_b[j][:, H:]], axis=-1)
            wg = jnp.pad(wg, ((0, 0), (0, LANES - GATE_W))).astype(BF16).reshape(3, MLSTM_INNER, LANES)
            bg = jnp.pad(jnp.concatenate([mlstm_b_gate_f[j], mlstm_b_gate_b[j]]), (0, LANES - GATE_W)).reshape(1, LANES)
            assert not isinstance(x, tuple)
            xc, q, kt, v, gcol, grow, z = mlstm_in(x, norm_g[i], mod4, i, mlstm_w_in, j, mlstm_conv_w[j],
                                                   mlstm_conv_b[j], bq, bk.astype(BF16), bkt, bv, wg, bg, **common)
            hn_p, Cp, np_, mp_ = mlstm_prompt(q, kt, v, gcol, grow, mlstm_mh_norm_g[j], Bp)
            C0 = state_mlstm_C[:, j]
            n0 = state_mlstm_n[:, j].reshape(Bs, 2, H, 1, DH)
            m0 = jnp.broadcast_to(state_mlstm_m[:, j].transpose(0, 2, 1)[..., None, None], (Bs, H, 2, 1, LANES))
            hn_s = mlstm_sample(q, kt, v, gcol, grow, mlstm_mh_norm_g[j], C0, n0, m0, Tp, Bs, Ss)
            x = proj_out(hn_p, hn_s, z, x, mod4, i, mlstm_w_out, j, xc=xc, skip=mlstm_skip[j],
                         **common, **last)
            mC_list.append(Cp)
            mn_list.append(np_.reshape(Bp, 1, 2, H, DH))
            mm_list.append(mp_[:, :, 0, 0:2].transpose(0, 2, 1)[:, None])

    cat1 = lambda parts: parts[0] if len(parts) == 1 else jnp.concatenate(parts, axis=1)
    y_prompt, y_sample = x

    def from_transposed(parts, trail):
        t = cat1(parts)
        t = t.reshape(t.shape[:2] + trail + (Sp,))
        return jnp.moveaxis(t, -1, 2)

    return (y_prompt.reshape(Bp, Sp, D), y_sample.reshape(Bs, Ss, D),
            from_transposed(gqa_k_list, (GQA_KV_HEADS, HEAD_DIM)), from_transposed(gqa_v_list, (GQA_KV_HEADS, HEAD_DIM)),
            from_transposed(diff_k_list, (DIFF_HEADS, 2, HEAD_DIM)), cat1(diff_v_list),
            cat1(mC_list), cat1(mn_list), cat1(mm_list))
```

```python
import functools
import math

import jax
import jax.numpy as jnp
from jax import lax
from jax.experimental import pallas as pl
from jax.experimental.pallas import tpu as pltpu

D_MODEL = 1024
DEPTH = 4
GRID_W = 64
N_MIXERS = 3
HEAD_DIM = 64
ROPE_THETA = 10000.0
NORM_EPS = 1e-6

GQA_HEADS = 16
GQA_KV_HEADS = 4
GQA_GROUP = GQA_HEADS // GQA_KV_HEADS
GQA_Q_W = GQA_HEADS * HEAD_DIM
GQA_KV_W = GQA_KV_HEADS * HEAD_DIM

DIFF_HEADS = 8
DIFF_QK_W = DIFF_HEADS * 2 * HEAD_DIM
DIFF_V_DIM = 2 * HEAD_DIM
DIFF_BRANCH = DIFF_HEADS * DIFF_V_DIM

MLSTM_INNER = 2 * D_MODEL
MLSTM_HEADS = 4
MLSTM_HEAD_DIM = MLSTM_INNER // MLSTM_HEADS
MLSTM_QKV_BLOCK = 4
MLSTM_CONV_K = 4

LANES = 128
SUBLANES = 8
VMEM_LIMIT_BYTES = 56 * 1024 * 1024

TOKEN_TILE = 512
PROJ_OUT_TILE = 1024
PROJ_N_TILE = 256
ATTN_Q_TILE = 256
ATTN_SHORT_SEQS = 2
MLSTM_CHUNK = 256
HEADWISE_CHUNK = LANES
GATE_W = 4 * MLSTM_HEADS
Q_SCALE = HEAD_DIM ** -0.5 * math.log2(math.e)

F32 = jnp.float32
BF16 = jnp.bfloat16


def _params(*semantics):
    return pltpu.CompilerParams(dimension_semantics=semantics, vmem_limit_bytes=VMEM_LIMIT_BYTES)


def _silu(x):
    return x * (1.0 / (1.0 + jnp.exp(-x)))


def _dot(a, b):
    return jnp.dot(a, b, preferred_element_type=F32)


def _dot_nt(a, b):
    return lax.dot_general(a, b, (((1,), (1,)), ((), ())), preferred_element_type=F32)


class TokenLayout:
    def __init__(self, n_prompt, seq, tile):
        self.prompt_tiles = n_prompt // tile
        self.tiles_per_seq = seq // tile

    def is_prompt(self, m):
        return m < self.prompt_tiles

    def cond_row(self, m):
        return jnp.where(m < self.prompt_tiles, 0, 1 + (m - self.prompt_tiles) // self.tiles_per_seq)

    def prompt_block(self, m):
        return jnp.minimum(m, self.prompt_tiles - 1)

    def sample_block(self, m):
        return jnp.maximum(m - self.prompt_tiles, 0)


def _mod_kernel(cond_ref, w_ref, b_ref, o_ref):
    a = _silu(cond_ref[...]).astype(BF16)
    o_ref[...] = _dot(a, w_ref[...].astype(BF16)) + b_ref[...]


def adaln_all(cond, w_mod, b_mod):
    tn = 1024
    return pl.pallas_call(
        _mod_kernel,
        grid=(DEPTH, 3 * D_MODEL // tn),
        in_specs=[
            pl.BlockSpec((SUBLANES, D_MODEL), lambda i, n: (0, 0)),
            pl.BlockSpec((None, D_MODEL, tn), lambda i, n: (i, 0, n)),
            pl.BlockSpec((None, 1, tn), lambda i, n: (i, 0, n)),
        ],
        out_specs=pl.BlockSpec((None, SUBLANES, tn), lambda i, n: (i, 0, n)),
        out_shape=jax.ShapeDtypeStruct((DEPTH, SUBLANES, 3 * D_MODEL), F32),
        compiler_params=_params("parallel", "parallel"),
        name="adaln_mod",
    )(cond, w_mod, b_mod.reshape(DEPTH, 1, 3 * D_MODEL))


def _swap_half_heads(y):
    w = y.shape[-1]
    lane = lax.broadcasted_iota(jnp.int32, y.shape, 1)
    upper = (lane & (HEAD_DIM // 2)) != 0
    return jnp.where(upper, pltpu.roll(y, HEAD_DIM // 2, 1), pltpu.roll(y, w - HEAD_DIM // 2, 1))


def _x_specs(x, lay, tm):
    if isinstance(x, tuple):
        return [pl.BlockSpec((tm, D_MODEL), lambda m: (lay.prompt_block(m), 0)),
                pl.BlockSpec((tm, D_MODEL), lambda m: (lay.sample_block(m), 0))], list(x)
    return [pl.BlockSpec((tm, D_MODEL), lambda m: (m, 0))], [x]


def _resident_spec(stacked, layer):
    return pl.BlockSpec((None,) + stacked.shape[1:], lambda m: (layer, 0, 0), pipeline_mode=pl.Buffered(1))


def _two_paths(is_prompt, path):
    pl.when(is_prompt)(functools.partial(path, True))
    pl.when(jnp.logical_not(is_prompt))(functools.partial(path, False))


def _proj_in_kernel(*refs, segs, layout, has_norm, has_rope, x_pair):
    it = iter(refs)
    x_refs = (next(it), next(it)) if x_pair else (next(it),) * 2
    g_ref, mod_ref, w_ref = next(it), next(it), next(it)
    if has_rope:
        cos_ref, sin_ref = next(it), next(it)
    if has_norm:
        gmat_ref, qg_ref, kg_ref = next(it), next(it), next(it)
    out_refs = []
    for _, _, kind in segs:
        if kind == "full":
            ref = next(it)
            out_refs.append((ref, ref, None))
        else:
            has_prompt, has_t = kind
            out_refs.append((next(it) if has_prompt else None, next(it), next(it) if has_t else None))
    cw = PROJ_N_TILE

    def store_transposed(ref, acc, col0):
        seq_len = ref.shape[2]
        for s in range(ref.shape[0]):
            ref[s, col0:col0 + cw, :] = acc[s * seq_len:(s + 1) * seq_len, :].T.astype(ref.dtype)

    def path(prompt):
        x = x_refs[0 if prompt else 1][...]
        y = x * lax.rsqrt(jnp.mean(x * x, axis=-1, keepdims=True) + NORM_EPS) * g_ref[...]
        shift = mod_ref[:, 0:D_MODEL]
        scale = mod_ref[:, D_MODEL:2 * D_MODEL]
        h = (y * (1.0 + scale) + shift).astype(BF16)
        n_chunks = sum(s[0] for s in segs) // cw
        accs = [_dot(h, w_ref[:, i * cw:(i + 1) * cw].astype(BF16)) for i in range(n_chunks)]
        col = 0
        for (width, mode, _), (p_ref, s_ref, t_ref) in zip(segs, out_refs):
            o_ref = p_ref if prompt else s_ref
            for c in range(width // cw):
                acc = accs[col // cw]
                col += cw
                if mode in ("q_norm", "k_norm"):
                    ss = _dot((acc * acc).astype(BF16), gmat_ref[...])
                    gvec = qg_ref[...] if mode == "q_norm" else kg_ref[...]
                    acc = acc * lax.rsqrt(ss * (1.0 / HEAD_DIM) + NORM_EPS) * gvec
                if has_rope and mode != "plain" and not prompt:
                    acc = acc * cos_ref[...] + _swap_half_heads(acc) * sin_ref[...]
                if mode in ("q_norm", "q"):
                    acc = acc * Q_SCALE
                if o_ref is not None:
                    o_ref[:, c * cw:(c + 1) * cw] = acc.astype(o_ref.dtype)
                if prompt and t_ref is not None:
                    store_transposed(t_ref, acc, c * cw)

    _two_paths(layout.is_prompt(pl.program_id(0)), path)


def proj_in(x, norm_g, mod4, layer, w, w_layer, segs, n_prompt, sample_seq, rope_tabs=None, head_norm=None,
            tile=TOKEN_TILE):
    tm, cw = tile, PROJ_N_TILE
    lay = TokenLayout(n_prompt, sample_seq, tm)
    T = x[0].shape[0] + x[1].shape[0] if isinstance(x, tuple) else x.shape[0]
    N = w.shape[2]
    assert N == sum(s[0] for s in segs) and all(s[0] % cw == 0 for s in segs)

    in_specs, args = _x_specs(x, lay, tm)
    in_specs += [
        pl.BlockSpec((1, D_MODEL), lambda m: (0, 0)),
        pl.BlockSpec((None, None, 1, 3 * D_MODEL), lambda m: (layer, lay.cond_row(m), 0, 0)),
        _resident_spec(w, w_layer),
    ]
    args += [norm_g.reshape(1, D_MODEL), mod4, w]
    if rope_tabs is not None:
        def pos_block(m):
            return (lay.sample_block(m) % lay.tiles_per_seq, 0)
        in_specs += [pl.BlockSpec((tm, cw), pos_block), pl.BlockSpec((tm, cw), pos_block)]
        args += list(rope_tabs)
    if head_norm is not None:
        in_specs += [pl.BlockSpec((cw, cw), lambda m: (0, 0)),
                     pl.BlockSpec((1, cw), lambda m: (0, 0)),
                     pl.BlockSpec((1, cw), lambda m: (0, 0))]
        args += list(head_norm)

    out_specs, out_shape, kern_segs = [], [], []
    for width, mode, out in segs:
        if isinstance(out, tuple):
            prompt_dt, t_dt, seq_len = out
            kern_segs.append((width, mode, (prompt_dt is not None, t_dt is not None)))
            if prompt_dt is not None:
                out_specs.append(pl.BlockSpec((tm, width), lambda m: (lay.prompt_block(m), 0)))
                out_shape.append(jax.ShapeDtypeStruct((n_prompt, width), prompt_dt))
            out_specs.append(pl.BlockSpec((tm, width), lambda m: (lay.sample_block(m), 0)))
            out_shape.append(jax.ShapeDtypeStruct((T - n_prompt, width), BF16))
            if t_dt is not None:
                out_specs.append(pl.BlockSpec((tm // seq_len, None, width, seq_len),
                                              lambda m: (lay.prompt_block(m), 0, 0, 0)))
                out_shape.append(jax.ShapeDtypeStruct((n_prompt // seq_len, 1, width, seq_len), t_dt))
        else:
            kern_segs.append((width, mode, "full"))
            out_specs.append(pl.BlockSpec((tm, width), lambda m: (m, 0)))
            out_shape.append(jax.ShapeDtypeStruct((T, width), out))

    kern = functools.partial(_proj_in_kernel, segs=tuple(kern_segs), layout=lay, x_pair=isinstance(x, tuple),
                             has_norm=head_norm is not None, has_rope=rope_tabs is not None)
    return pl.pallas_call(
        kern,
        grid=(T // tm,),
        in_specs=in_specs,
        out_specs=out_specs,
        out_shape=out_shape,
        compiler_params=_params("arbitrary"),
        name="proj_in",
    )(*args)


def _proj_out_kernel(*refs, layout, with_skip, x_pair, final):
    it = iter(refs)
    o_refs = (next(it), next(it))
    if with_skip:
        xc_ref, skip_ref = next(it), next(it)
    z_ref = next(it)
    x_refs = (next(it), next(it)) if x_pair else (next(it),) * 2
    mod_ref, w_ref = next(it), next(it)
    if final:
        fg_ref = next(it)
    out_refs = (next(it), next(it)) if final else (next(it),) * 2

    def path(prompt):
        sel = 0 if prompt else 1
        a = o_refs[sel][...].astype(F32)
        if with_skip:
            a = a + skip_ref[...] * xc_ref[...].astype(F32)
        a = (a * _silu(z_ref[...].astype(F32))).astype(BF16)
        gate = mod_ref[:, 2 * D_MODEL:3 * D_MODEL]
        x = x_refs[sel][...] + gate * _dot(a, w_ref[...].astype(BF16))
        if final:
            x = x * lax.rsqrt(jnp.mean(x * x, axis=-1, keepdims=True) + NORM_EPS) * fg_ref[...]
        out_refs[sel][...] = x

    _two_paths(layout.is_prompt(pl.program_id(0)), path)


def proj_out(o_prompt, o_sample, z, x, mod4, layer, w_out, w_layer, n_prompt, sample_seq, xc=None, skip=None,
             final_g=None, tile=TOKEN_TILE):
    T = z.shape[0]
    K = o_prompt.shape[1]
    tm = tile
    lay = TokenLayout(n_prompt, sample_seq, tm)
    row = lambda m: (m, 0)
    prompt_row = lambda m: (lay.prompt_block(m), 0)
    sample_row = lambda m: (lay.sample_block(m), 0)
    in_specs = [pl.BlockSpec((tm, K), prompt_row), pl.BlockSpec((tm, K), sample_row)]
    args = [o_prompt, o_sample]
    if xc is not None:
        in_specs += [pl.BlockSpec((tm, K), row), pl.BlockSpec((1, K), lambda m: (0, 0))]
        args += [xc, skip.reshape(1, K)]
    in_specs.append(pl.BlockSpec((tm, K), row))
    args.append(z)
    x_specs, x_args = _x_specs(x, lay, tm)
    in_specs += x_specs + [
        pl.BlockSpec((None, None, 1, 3 * D_MODEL), lambda m: (layer, lay.cond_row(m), 0, 0)),
        _resident_spec(w_out, w_layer),
    ]
    args += x_args + [mod4, w_out]
    if final_g is not None:
        in_specs.append(pl.BlockSpec((1, D_MODEL), lambda m: (0, 0)))
        args.append(final_g.reshape(1, D_MODEL))
        out_specs = [pl.BlockSpec((tm, D_MODEL), prompt_row), pl.BlockSpec((tm, D_MODEL), sample_row)]
        out_shape = [jax.ShapeDtypeStruct((n_prompt, D_MODEL), F32),
                     jax.ShapeDtypeStruct((T - n_prompt, D_MODEL), F32)]
    else:
        out_specs = pl.BlockSpec((tm, D_MODEL), row)
        out_shape = jax.ShapeDtypeStruct((T, D_MODEL), F32)
    kern = functools.partial(_proj_out_kernel, layout=lay, with_skip=xc is not None,
                             x_pair=isinstance(x, tuple), final=final_g is not None)
    return pl.pallas_call(
        kern,
        grid=(T // tm,),
        in_specs=in_specs,
        out_specs=out_specs,
        out_shape=out_shape,
        compiler_params=_params("arbitrary"),
        name="proj_out",
    )(*args)


def _low_half(shape):
    return lax.broadcasted_iota(jnp.int32, shape, 1) < HEAD_DIM


def _attend_group(items, rows):
    scores = [[_dot_nt(q, k) for k in k_parts] for q, k_parts, _, _ in items]
    probs = []
    for s in scores:
        m = functools.reduce(jnp.maximum, [jnp.max(x, axis=-1, keepdims=True) for x in s])
        probs.append([jnp.exp2(x - m).astype(BF16) for x in s])
    out = []
    for p_parts, (_, _, v_ref, head) in zip(probs, items):
        acc = None
        for p, r in zip(p_parts, rows):
            t = _dot(p, v_ref[head, r, :])
            acc = t if acc is None else acc + t
        out.append(acc)
    return out


def _group_size(n_keys):
    return 16 if n_keys <= 512 else 4


def _row_parts(k_ref, kc_ref):
    n_new = k_ref.shape[0]
    rows = [slice(0, n_new)]
    if kc_ref is not None:
        rows.append(slice(n_new, n_new + kc_ref.shape[0]))
    return rows


def _per_sequence(body, seqs, row_refs, shared, scratch):
    for s in range(seqs):
        def view(ref):
            n = ref.shape[0] // seqs
            return ref if seqs == 1 else ref.at[pl.ds(s * n, n)]
        body(*[view(r) for r in row_refs], *shared, *[r.at[s] for r in scratch])


def _gqa_attn_kernel(*refs, has_ctx, seqs):
    if has_ctx:
        q_ref, k_ref, v_ref, kc_ref, vc_ref, o_ref, ve_ref, vo_ref = refs
    else:
        q_ref, k_ref, v_ref, o_ref, ve_ref, vo_ref = refs
        kc_ref = vc_ref = None
    _per_sequence(_gqa_attn_sequence, seqs, (q_ref, k_ref, v_ref, o_ref), (kc_ref, vc_ref), (ve_ref, vo_ref))


def _gqa_attn_sequence(q_ref, k_ref, v_ref, o_ref, kc_ref, vc_ref, ve_ref, vo_ref):
    rows = _row_parts(k_ref, kc_ref)

    @pl.when(pl.program_id(1) == 0)
    def _():
        for src, r in zip([v_ref, vc_ref][:len(rows)], rows):
            for c in range(GQA_KV_HEADS):
                slab = src[:, (c // 2) * LANES:(c // 2 + 1) * LANES].astype(F32)
                low = _low_half(slab.shape)
                if c % 2 == 0:
                    ve = jnp.where(low, slab, 0.0)
                    vo = pltpu.roll(ve, HEAD_DIM, 1)
                else:
                    vo = jnp.where(low, 0.0, slab)
                    ve = pltpu.roll(vo, HEAD_DIM, 1)
                ones_e = jnp.where(low, 1.0, 0.0)
                ve_ref[c, r, 0:LANES] = ve.astype(BF16)
                ve_ref[c, r, LANES:2 * LANES] = ones_e.astype(BF16)
                vo_ref[c, r, 0:LANES] = vo.astype(BF16)
                vo_ref[c, r, LANES:2 * LANES] = (1.0 - ones_e).astype(BF16)

    k_srcs = [k_ref, kc_ref][:len(rows)]
    group = _group_size(rows[-1].stop)
    n_pairs = GQA_HEADS // 2
    for p0 in range(0, n_pairs, group // 2):
        items = []
        for pair in range(p0, p0 + group // 2):
            c = pair // (GQA_GROUP // 2)
            q = q_ref[:, pair * LANES:(pair + 1) * LANES].astype(F32)
            low = _low_half(q.shape)
            q_sw = pltpu.roll(q, HEAD_DIM, 1)
            if c % 2 == 0:
                qe, qo = jnp.where(low, q, 0.0), jnp.where(low, q_sw, 0.0)
            else:
                qe, qo = jnp.where(low, 0.0, q_sw), jnp.where(low, 0.0, q)
            k_parts = [s[:, (c // 2) * LANES:(c // 2 + 1) * LANES].astype(BF16) for s in k_srcs]
            items += [(qe.astype(BF16), k_parts, ve_ref, c), (qo.astype(BF16), k_parts, vo_ref, c)]
        accs = _attend_group(items, rows)
        for i, pair in enumerate(range(p0, p0 + group // 2)):
            acc = accs[2 * i] + accs[2 * i + 1]
            o_ref[:, pair * LANES:(pair + 1) * LANES] = (acc[:, 0:LANES] / acc[:, LANES:2 * LANES]).astype(o_ref.dtype)


def _diff_attn_kernel(*refs, has_ctx, seqs, lam_init):
    if has_ctx:
        q_ref, k_ref, v_ref, kc_ref, vc_ref, lam_ref, g_ref, o_ref, vx_ref = refs
    else:
        q_ref, k_ref, v_ref, lam_ref, g_ref, o_ref, vx_ref = refs
        kc_ref = vc_ref = None
    _per_sequence(functools.partial(_diff_attn_sequence, lam_init=lam_init), seqs, (q_ref, k_ref, v_ref, o_ref),
                  (kc_ref, vc_ref, lam_ref, g_ref), (vx_ref,))


def _diff_attn_sequence(q_ref, k_ref, v_ref, o_ref, kc_ref, vc_ref, lam_ref, g_ref, vx_ref, *, lam_init):
    rows = _row_parts(k_ref, kc_ref)

    @pl.when(pl.program_id(1) == 0)
    def _():
        for src, r in zip([v_ref, vc_ref][:len(rows)], rows):
            for h in range(DIFF_HEADS):
                vx_ref[h, r, 0:LANES] = src[:, h * LANES:(h + 1) * LANES].astype(BF16)
                vx_ref[h, r, LANES:2 * LANES] = jnp.ones((r.stop - r.start, LANES), BF16)

    lv = lam_ref[...]
    s1 = jnp.sum(lv[0:1] * lv[1:2], axis=-1, keepdims=True)
    s2 = jnp.sum(lv[2:3] * lv[3:4], axis=-1, keepdims=True)
    lam = jnp.exp(s1) - jnp.exp(s2) + lam_init
    k_srcs = [k_ref, kc_ref][:len(rows)]
    group = _group_size(rows[-1].stop)
    for h0 in range(0, DIFF_HEADS, group // 2):
        items = []
        for h in range(h0, h0 + group // 2):
            q = q_ref[:, h * LANES:(h + 1) * LANES].astype(F32)
            low = _low_half(q.shape)
            k_parts = [s[:, h * LANES:(h + 1) * LANES].astype(BF16) for s in k_srcs]
            items += [(jnp.where(low, q, 0.0).astype(BF16), k_parts, vx_ref, h),
                      (jnp.where(low, 0.0, q).astype(BF16), k_parts, vx_ref, h)]
        accs = _attend_group(items, rows)
        for i, h in enumerate(range(h0, h0 + group // 2)):
            a0, a1 = accs[2 * i], accs[2 * i + 1]
            o = a0[:, 0:LANES] / a0[:, LANES:2 * LANES] - lam * (a1[:, 0:LANES] / a1[:, LANES:2 * LANES])
            o = o * lax.rsqrt(jnp.mean(o * o, axis=-1, keepdims=True) + NORM_EPS) * g_ref[...]
            o_ref[:, h * LANES:(h + 1) * LANES] = (o * (1.0 - lam_init)).astype(o_ref.dtype)


def attention(kernel, v_scratch, q, k, v, batch, sq, q_row0, ctx=None, extra=()):
    tq = ATTN_Q_TILE
    nq = sq // tq
    seqs = ATTN_SHORT_SEQS if nq == 1 and ctx is None and batch % ATTN_SHORT_SEQS == 0 else 1
    batch, sq, tq = batch // seqs, sq * seqs, tq * seqs
    off = q_row0 // tq
    in_specs = [
        pl.BlockSpec((tq, q.shape[1]), lambda b, i: (b * nq + i + off, 0)),
        pl.BlockSpec((sq, k.shape[1]), lambda b, i: (b, 0)),
        pl.BlockSpec((sq, v.shape[1]), lambda b, i: (b, 0)),
    ]
    args = [q, k, v]
    sk = sq
    if ctx is not None:
        kc, vc, layer = ctx
        sk += kc.shape[2]
        in_specs += [pl.BlockSpec((None, None) + kc.shape[2:], lambda b, i: (b, layer, 0, 0)),
                     pl.BlockSpec((None, None) + vc.shape[2:], lambda b, i: (b, layer, 0, 0))]
        args += [kc, vc]
    for a in extra:
        in_specs.append(pl.BlockSpec(a.shape, lambda b, i: (0, 0)))
    args += list(extra)
    return pl.pallas_call(
        functools.partial(kernel, has_ctx=ctx is not None, seqs=seqs),
        grid=(batch, nq),
        in_specs=in_specs,
        out_specs=pl.BlockSpec((tq, D_MODEL), lambda b, i: (b * nq + i, 0)),
        out_shape=jax.ShapeDtypeStruct((batch * sq, D_MODEL), BF16),
        scratch_shapes=[pltpu.VMEM((seqs, heads, sk // seqs, width), BF16) for heads, width in v_scratch],
        compiler_params=_params("parallel", "arbitrary"),
        name="attention",
    )(*args)


def _log_sigmoid(x):
    return jnp.minimum(x, 0.0) - jnp.log1p(jnp.exp(-jnp.abs(x)))


def _mlstm_in_kernel(cur_ref, prev_ref, next_ref, ng_ref, mod_ref, w_ref, cw_ref, cb_ref, bq_ref, bk_ref, bkt_ref,
                     bv_ref, wg_ref, bg_ref, xc_ref, q_ref, kt_ref, v_ref, gcol_ref, grow_ref, z_ref,
                     *, n_prompt_tiles, tiles_per_seq):
    L = MLSTM_CHUNK
    cw = PROJ_N_TILE
    t = pl.program_id(0)
    st = jnp.maximum(t - n_prompt_tiles, 0) % tiles_per_seq
    first = (t < n_prompt_tiles) | (st == 0)
    last = (t < n_prompt_tiles) | (st == tiles_per_seq - 1)
    n_ext = L + 2 * SUBLANES
    x = jnp.concatenate([prev_ref[...], cur_ref[...], next_ref[...]], axis=0)
    y = x * lax.rsqrt(jnp.mean(x * x, axis=-1, keepdims=True) + NORM_EPS) * ng_ref[...]
    h = (y * (1.0 + mod_ref[:, D_MODEL:2 * D_MODEL]) + mod_ref[:, 0:D_MODEL]).astype(BF16)
    n_xm = MLSTM_INNER // cw
    xm_chunks = [_dot(h, w_ref[:, i * cw:(i + 1) * cw].astype(BF16)) for i in range(n_xm)]
    z_chunks = [_dot(h[SUBLANES:SUBLANES + L], w_ref[:, (n_xm + i) * cw:(n_xm + i + 1) * cw].astype(BF16))
                for i in range(n_xm)]
    for i, zc in enumerate(z_chunks):
        z_ref[:, i * cw:(i + 1) * cw] = zc.astype(z_ref.dtype)
    row_e = lax.broadcasted_iota(jnp.int32, (n_ext, 1), 0)
    inside = ((row_e >= SUBLANES) | jnp.logical_not(first)) & ((row_e < SUBLANES + L) | jnp.logical_not(last))
    ext = jnp.where(inside, jnp.concatenate(xm_chunks, axis=1), 0.0)
    cur = ext[SUBLANES:SUBLANES + L]
    pad_l = MLSTM_CONV_K // 2
    acc = cb_ref[...] + cw_ref[pad_l:pad_l + 1, :] * cur
    for j in range(MLSTM_CONV_K):
        if j == pad_l:
            continue
        sh = (pad_l - j) % n_ext
        acc = acc + cw_ref[j:j + 1, :] * pltpu.roll(ext, sh, 0)[SUBLANES:SUBLANES + L]
    xc = _silu(acc)
    xc_ref[...] = xc.astype(xc_ref.dtype)

    g = jnp.zeros((L, LANES), F32)
    for c in range(MLSTM_INNER // HEADWISE_CHUNK):
        sl = slice(c * HEADWISE_CHUNK, (c + 1) * HEADWISE_CHUNK)
        xcb = xc[:, sl].astype(BF16)
        xmb = cur[:, sl].astype(BF16)
        qc = _dot(xcb, bq_ref[c]).astype(BF16)
        kc = _dot(xcb, bk_ref[c]).astype(BF16)
        vc = _dot(xmb, bv_ref[c]).astype(BF16)
        q_ref[:, sl] = qc
        v_ref[:, sl] = vc
        kt_ref[sl, :] = _dot_nt(bkt_ref[c], xcb).astype(BF16)
        g = g + _dot(qc, wg_ref[0, sl, :]) + _dot(kc, wg_ref[1, sl, :]) + _dot(vc, wg_ref[2, sl, :])
    g = g + bg_ref[...]

    ls = _log_sigmoid(g)
    row = lax.broadcasted_iota(jnp.int32, (L, LANES), 0)
    pre, suf = ls, ls
    s = 1
    while s < L:
        pre = pre + jnp.where(row >= s, pltpu.roll(pre, s, 0), 0.0)
        suf = suf + jnp.where(row < L - s, pltpu.roll(suf, L - s, 0), 0.0)
        s *= 2
    lane = lax.broadcasted_iota(jnp.int32, (L, LANES), 1)
    H = MLSTM_HEADS
    out = jnp.where((lane >= H) & (lane < 2 * H), pre, jnp.where((lane >= 3 * H) & (lane < 4 * H), suf, g))
    gcol_ref[...] = out[:, 0:GATE_W]
    grow_ref[...] = out.T[0:GATE_W, :]


def mlstm_in(x, norm_g, mod4, layer, w, w_layer, conv_w, conv_b, bq, bk, bkt, bv, wg, bg, n_prompt, sample_seq):
    T = x.shape[0]
    L = MLSTM_CHUNK
    n_tiles = T // L
    r8 = L // SUBLANES
    n_blk8 = T // SUBLANES
    C = MLSTM_INNER
    lay = TokenLayout(n_prompt, sample_seq, L)
    kern = functools.partial(_mlstm_in_kernel, n_prompt_tiles=n_prompt // L, tiles_per_seq=sample_seq // L)
    const3 = lambda t: (0, 0, 0)
    return pl.pallas_call(
        kern,
        grid=(n_tiles,),
        in_specs=[
            pl.BlockSpec((L, D_MODEL), lambda t: (t, 0)),
            pl.BlockSpec((SUBLANES, D_MODEL), lambda t: (jnp.maximum(t * r8 - 1, 0), 0)),
            pl.BlockSpec((SUBLANES, D_MODEL), lambda t: (jnp.minimum((t + 1) * r8, n_blk8 - 1), 0)),
            pl.BlockSpec((1, D_MODEL), lambda t: (0, 0)),
            pl.BlockSpec((None, None, 1, 3 * D_MODEL), lambda t: (layer, lay.cond_row(t), 0, 0)),
            _resident_spec(w, w_layer),
            pl.BlockSpec((MLSTM_CONV_K, C), lambda t: (0, 0)),
            pl.BlockSpec((1, C), lambda t: (0, 0)),
            pl.BlockSpec(bq.shape, const3),
            pl.BlockSpec(bk.shape, const3),
            pl.BlockSpec(bkt.shape, const3),
            pl.BlockSpec(bv.shape, const3),
            pl.BlockSpec(wg.shape, const3),
            pl.BlockSpec((1, LANES), lambda t: (0, 0)),
        ],
        out_specs=[
            pl.BlockSpec((L, C), lambda t: (t, 0)),
            pl.BlockSpec((L, C), lambda t: (t, 0)),
            pl.BlockSpec((None, C, L), lambda t: (t, 0, 0)),
            pl.BlockSpec((L, C), lambda t: (t, 0)),
            pl.BlockSpec((L, GATE_W), lambda t: (t, 0)),
            pl.BlockSpec((None, GATE_W, L), lambda t: (t, 0, 0)),
            pl.BlockSpec((L, C), lambda t: (t, 0)),
        ],
        out_shape=[
            jax.ShapeDtypeStruct((T, C), BF16),
            jax.ShapeDtypeStruct((T, C), BF16),
            jax.ShapeDtypeStruct((n_tiles, C, L), BF16),
            jax.ShapeDtypeStruct((T, C), BF16),
            jax.ShapeDtypeStruct((T, GATE_W), F32),
            jax.ShapeDtypeStruct((n_tiles, GATE_W, L), F32),
            jax.ShapeDtypeStruct((T, C), BF16),
        ],
        compiler_params=_params("parallel"),
        name="mlstm_in",
    )(x, x, x, norm_g.reshape(1, D_MODEL), mod4, w, conv_w, conv_b.reshape(1, C), bq, bk, bkt, bv, wg, bg)


def _gate_cols(gcol, h):
    lane = lax.broadcasted_iota(jnp.int32, gcol.shape, 1)
    pick = lambda idx: jnp.sum(jnp.where(lane == idx, gcol, 0.0), axis=-1, keepdims=True)
    return pick(MLSTM_HEADS + h), pick(3 * MLSTM_HEADS + h)


def _chunk_weights(qk, b_col, b_row, i_row, tot, causal_mask, m_prev):
    ar = i_row - b_row
    d = jnp.where(causal_mask, b_col + ar, -jnp.inf)
    inter = b_col + m_prev
    mt = jnp.maximum(jnp.max(d, axis=-1, keepdims=True), inter)
    s = qk * jnp.exp(d - mt)
    g = tot + ar
    m_new = jnp.maximum(tot + m_prev, jnp.max(g, axis=-1, keepdims=True))
    w = jnp.exp(g - m_new)
    w_hi = w.astype(BF16).astype(F32)
    rid = lax.broadcasted_iota(jnp.int32, (SUBLANES, w.shape[-1]), 0)
    w8 = jnp.where(rid == 0, w_hi, jnp.where(rid == 1, w - w_hi, 0.0)).astype(BF16)
    return dict(s=s.astype(BF16), den=jnp.sum(s, axis=-1, keepdims=True), mt=mt, inter=inter, w=w, w8=w8,
                m_new=m_new, decay_arg=tot + m_prev - m_new)


def _chunk_matmuls(wts, q, kt, v, C_prev):
    num = _dot(wts["s"], v)
    qC = None if C_prev is None else _dot(q, C_prev.astype(BF16))
    C_add = _dot((kt.astype(F32) * wts["w"]).astype(BF16), v)
    n8 = _dot_nt(wts["w8"], kt)
    return num, qC, C_add, n8[0:1] + n8[1:2]


def _chunk_finish(wts, mats, q, C_prev, n_prev):
    num, qC, C_new, n_new = mats
    den, mt = wts["den"], wts["mt"]
    if C_prev is not None:
        w_inter = jnp.exp(wts["inter"] - mt)
        num = num + w_inter * qC
        den = den + w_inter * jnp.sum(q.astype(F32) * n_prev, axis=-1, keepdims=True)
        decay = jnp.exp(wts["decay_arg"])
        C_new = decay * C_prev + C_new
        n_new = decay * n_prev + n_new
    h = num / jnp.maximum(jnp.abs(den), jnp.exp(-mt))
    return h, C_new, n_new, wts["m_new"]


def _chunk_dir(qk, q, kt, v, b_col, b_row, i_row, tot, causal_mask, m_prev, C_prev, n_prev):
    wts = _chunk_weights(qk, b_col, b_row, i_row, tot, causal_mask, m_prev)
    return _chunk_finish(wts, _chunk_matmuls(wts, q, kt, v, C_prev), q, C_prev, n_prev)


def _head_layernorm(h, g):
    mu = jnp.mean(h, axis=-1, keepdims=True)
    hc = h - mu
    var = jnp.mean(hc * hc, axis=-1, keepdims=True)
    return hc * lax.rsqrt(var + NORM_EPS) * g


def _chunk_masks(L):
    t_idx = lax.broadcasted_iota(jnp.int32, (L, L), 0)
    s_idx = lax.broadcasted_iota(jnp.int32, (L, L), 1)
    return s_idx <= t_idx, s_idx >= t_idx


def _mlstm_prompt_kernel(q_ref, kt_ref, v_ref, gcol_ref, grow_ref, mhg_ref,
                         hn_ref, C_ref, n_ref, m_ref):
    L = MLSTM_CHUNK
    H, DH = MLSTM_HEADS, MLSTM_HEAD_DIM
    fmask, bmask = _chunk_masks(L)
    zero = jnp.zeros((1, 1), F32)
    lane = lax.broadcasted_iota(jnp.int32, (1, LANES), 1)
    col = lambda idx: gcol_ref[:, idx:idx + 1]
    row = lambda idx: grow_ref[idx:idx + 1, :]
    for h in range(H):
        ch = slice(h * DH, (h + 1) * DH)
        q, kt, v = q_ref[:, ch], kt_ref[ch, :], v_ref[:, ch]
        qk = _dot(q, kt)
        bf_row, bb_row = row(H + h), row(3 * H + h)
        hf, Cf, nf, mf = _chunk_dir(qk, q, kt, v, col(H + h), bf_row, row(h), bf_row[:, L - 1:L], fmask,
                                    zero, None, None)
        hb, Cb, nb, mb = _chunk_dir(qk, q, kt, v, col(3 * H + h), bb_row, row(2 * H + h), bb_row[:, 0:1], bmask,
                                    zero, None, None)
        hn_ref[:, ch] = _head_layernorm(hf + hb, mhg_ref[:, ch]).astype(hn_ref.dtype)
        C_ref[0, h] = Cf
        C_ref[1, h] = Cb
        n_ref[0, h] = nf
        n_ref[1, h] = nb
        m_ref[h] = jnp.where(lane == 0, mf, jnp.where(lane == 1, mb, 0.0))


def mlstm_prompt(q, kt, v, gcol, grow, mh_g, batch):
    L, H, DH, C = MLSTM_CHUNK, MLSTM_HEADS, MLSTM_HEAD_DIM, MLSTM_INNER
    return pl.pallas_call(
        _mlstm_prompt_kernel,
        grid=(batch,),
        in_specs=[
            pl.BlockSpec((L, C), lambda b: (b, 0)),
            pl.BlockSpec((None, C, L), lambda b: (b, 0, 0)),
            pl.BlockSpec((L, C), lambda b: (b, 0)),
            pl.BlockSpec((L, GATE_W), lambda b: (b, 0)),
            pl.BlockSpec((None, GATE_W, L), lambda b: (b, 0, 0)),
            pl.BlockSpec((1, C), lambda b: (0, 0)),
        ],
        out_specs=[
            pl.BlockSpec((L, C), lambda b: (b, 0)),
            pl.BlockSpec((None, None, 2, H, DH, DH), lambda b: (b, 0, 0, 0, 0, 0)),
            pl.BlockSpec((None, 2, H, 1, DH), lambda b: (b, 0, 0, 0, 0)),
            pl.BlockSpec((None, H, 1, LANES), lambda b: (b, 0, 0, 0)),
        ],
        out_shape=[
            jax.ShapeDtypeStruct((batch * L, C), BF16),
            jax.ShapeDtypeStruct((batch, 1, 2, H, DH, DH), F32),
            jax.ShapeDtypeStruct((batch, 2, H, 1, DH), F32),
            jax.ShapeDtypeStruct((batch, H, 1, LANES), F32),
        ],
        compiler_params=_params("parallel"),
        name="mlstm_prompt",
    )(q, kt, v, gcol, grow, mh_g.reshape(1, C))


def _mlstm_sample_kernel(q_ref, kt_ref, v_ref, gcol_ref, grow_ref, mhg_ref, C0_ref, n0_ref, m0_ref,
                         hn_ref, hf_ref, hb_ref, C_ref, n_ref, m_ref, *, n_chunks):
    L = MLSTM_CHUNK
    H = MLSTM_HEADS
    h = pl.program_id(1)
    fmask, bmask = _chunk_masks(L)
    C_ref[...] = C0_ref[...]
    n_ref[...] = n0_ref[...]
    m_ref[...] = m0_ref[...]

    def one(c, direction):
        rows = pl.ds(pl.multiple_of(c * L, L), L)
        q, kt, v = q_ref[rows, :], kt_ref[c], v_ref[rows, :]
        qk = _dot(q, kt)
        gcol = gcol_ref[rows, :]
        b_col = _gate_cols(gcol, h)[direction]
        grow = lambda idx: grow_ref[c, pl.ds(idx, 1), :]
        b_row = grow((2 * direction + 1) * H + h)
        i_row = grow(2 * direction * H + h)
        tot = b_row[:, L - 1:L] if direction == 0 else b_row[:, 0:1]
        mask = fmask if direction == 0 else bmask
        m_prev = m_ref[direction][:, 0:1]
        hh, C_new, n_new, m_new = _chunk_dir(qk, q, kt, v, b_col, b_row, i_row, tot, mask, m_prev,
                                             C_ref[direction], n_ref[direction])
        C_ref[direction] = C_new
        n_ref[direction] = n_new
        m_ref[direction] = jnp.broadcast_to(m_new, (1, LANES))
        (hf_ref if direction == 0 else hb_ref)[rows, :] = hh

    def body(i, carry):
        one(i, 0)
        one(n_chunks - 1 - i, 1)
        return carry

    lax.fori_loop(0, n_chunks, body, 0)

    def norm(c, carry):
        rows = pl.ds(pl.multiple_of(c * L, L), L)
        hn_ref[rows, :] = _head_layernorm(hf_ref[rows, :] + hb_ref[rows, :], mhg_ref[...]).astype(hn_ref.dtype)
        return carry

    lax.fori_loop(0, n_chunks, norm, 0)


def mlstm_sample(q, kt, v, gcol, grow, mh_g, C0, n0, m0, n_prompt, batch, seq):
    L, H, DH = MLSTM_CHUNK, MLSTM_HEADS, MLSTM_HEAD_DIM
    nc = seq // L
    row_off = n_prompt // seq
    tile_off = n_prompt // L // nc
    return pl.pallas_call(
        functools.partial(_mlstm_sample_kernel, n_chunks=nc),
        grid=(batch, H),
        in_specs=[
            pl.BlockSpec((seq, DH), lambda b, h: (b + row_off, h)),
            pl.BlockSpec((nc, DH, L), lambda b, h: (b + tile_off, h, 0)),
            pl.BlockSpec((seq, DH), lambda b, h: (b + row_off, h)),
            pl.BlockSpec((seq, GATE_W), lambda b, h: (b + row_off, 0)),
            pl.BlockSpec((nc, GATE_W, L), lambda b, h: (b + tile_off, 0, 0)),
            pl.BlockSpec((1, DH), lambda b, h: (0, h)),
            pl.BlockSpec((None, 2, None, DH, DH), lambda b, h: (b, 0, h, 0, 0)),
            pl.BlockSpec((None, 2, None, 1, DH), lambda b, h: (b, 0, h, 0, 0)),
            pl.BlockSpec((None, None, 2, 1, LANES), lambda b, h: (b, h, 0, 0, 0)),
        ],
        out_specs=pl.BlockSpec((seq, DH), lambda b, h: (b, h)),
        out_shape=jax.ShapeDtypeStruct((batch * seq, MLSTM_INNER), BF16),
        scratch_shapes=[
            pltpu.VMEM((seq, DH), F32),
            pltpu.VMEM((seq, DH), F32),
            pltpu.VMEM((2, DH, DH), F32),
            pltpu.VMEM((2, 1, DH), F32),
            pltpu.VMEM((2, 1, LANES), F32),
        ],
        compiler_params=_params("parallel", "parallel"),
        name="mlstm_sample",
    )(q, kt, v, gcol, grow, mh_g.reshape(1, MLSTM_INNER), C0, n0, m0)


def _rope_tables(seq):
    t = jnp.arange(seq)
    row = (t // GRID_W).astype(F32)
    col = (t % GRID_W).astype(F32)
    n_freq = HEAD_DIM // 4
    inv = ROPE_THETA ** (-jnp.arange(n_freq, dtype=F32) / n_freq)
    ang = jnp.concatenate([row[:, None] * inv, col[:, None] * inv], axis=-1)
    cos, sin = jnp.cos(ang), jnp.sin(ang)
    reps = PROJ_N_TILE // HEAD_DIM
    return (jnp.tile(jnp.concatenate([cos, cos], axis=-1), (1, reps)),
            jnp.tile(jnp.concatenate([-sin, sin], axis=-1), (1, reps)))


def _block_diag(w):
    blk, ch = MLSTM_QKV_BLOCK, HEADWISE_CHUNK
    w4 = w.reshape(-1, ch // blk, blk, blk)
    rows = w4.transpose(0, 2, 1, 3).reshape(-1, 1, blk, ch)
    full = jnp.broadcast_to(rows, (rows.shape[0], ch // blk, blk, ch)).reshape(-1, ch, ch)
    block_of = jnp.arange(ch) // blk
    return jnp.where(block_of[:, None] == block_of[None, :], full, 0.0)


def _diff_lambda_init(layer_idx):
    return 0.8 - 0.6 * math.exp(-0.3 * layer_idx)


def kernel(x_prompt, x_sample, cache_gqa_k, cache_gqa_v, cache_diff_k, cache_diff_v, state_mlstm_C, state_mlstm_n, state_mlstm_m, c, c_ctx, norm_g, w_mod, b_mod, gqa_w_in, gqa_q_norm_g, gqa_k_norm_g, gqa_w_out, diff_w_in, diff_lambda_q1, diff_lambda_k1, diff_lambda_q2, diff_lambda_k2, diff_subln_g, diff_w_out, mlstm_w_in, mlstm_conv_w, mlstm_conv_b, mlstm_w_q, mlstm_w_k, mlstm_w_v, mlstm_w_gate_f, mlstm_b_gate_f, mlstm_w_gate_b, mlstm_b_gate_b, mlstm_mh_norm_g, mlstm_skip, mlstm_w_out, final_norm_g):
    Bp, Sp, D = x_prompt.shape
    Bs, Ss, _ = x_sample.shape
    Tp, Ts = Bp * Sp, Bs * Ss
    assert Sp == MLSTM_CHUNK and Ss % TOKEN_TILE == 0 and Tp % TOKEN_TILE == 0
    H, DH = MLSTM_HEADS, MLSTM_HEAD_DIM
    past = cache_gqa_k.shape[2]

    x = (x_prompt.reshape(Tp, D), x_sample.reshape(Ts, D))
    cond = jnp.zeros((SUBLANES, D), F32).at[0].set(c_ctx).at[1:1 + Bs].set(c)
    mod4 = adaln_all(cond, w_mod, b_mod).reshape(DEPTH, SUBLANES, 1, 3 * D)
    rope_tabs = _rope_tables(Ss)
    gi = jnp.arange(PROJ_N_TILE) // HEAD_DIM
    gmat = (gi[:, None] == gi[None, :]).astype(BF16)
    tile_g = lambda g: jnp.tile(g, PROJ_N_TILE // HEAD_DIM).reshape(1, PROJ_N_TILE)
    ctx_gqa = (cache_gqa_k.reshape(Bs, -1, past, GQA_KV_W), cache_gqa_v.reshape(Bs, -1, past, GQA_KV_W))
    ctx_diff = (cache_diff_k.reshape(Bs, -1, past, DIFF_QK_W), cache_diff_v.reshape(Bs, -1, past, DIFF_BRANCH))

    gqa_k_list, gqa_v_list, diff_k_list, diff_v_list = [], [], [], []
    mC_list, mn_list, mm_list = [], [], []
    for i in range(DEPTH):
        kind, j = i % N_MIXERS, i // N_MIXERS
        common = dict(n_prompt=Tp, sample_seq=Ss)
        last = dict(final_g=final_norm_g) if i == DEPTH - 1 else {}
        if kind == 0:
            segs = ((GQA_Q_W, "q_norm", BF16), (GQA_KV_W, "k_norm", (BF16, F32, Sp)),
                    (GQA_KV_W, "plain", (BF16, F32, Sp)), (GQA_Q_W, "plain", BF16))
            q, kp, ks, k_cache, vp, vs, v_cache, z = proj_in(
                x, norm_g[i], mod4, i, gqa_w_in, j, segs, rope_tabs=rope_tabs,
                head_norm=(gmat, tile_g(gqa_q_norm_g[j]), tile_g(gqa_k_norm_g[j])), **common)
            scr = [(GQA_KV_HEADS, 2 * LANES)] * 2
            kern = _gqa_attn_kernel
            o_p = attention(kern, scr, q, kp, vp, Bp, Sp, 0)
            o_s = attention(kern, scr, q, ks, vs, Bs, Ss, Tp, ctx=(*ctx_gqa, j))
            x = proj_out(o_p, o_s, z, x, mod4, i, gqa_w_out, j, tile=PROJ_OUT_TILE, **common, **last)
            gqa_k_list.append(k_cache)
            gqa_v_list.append(v_cache)
        elif kind == 1:
            lam_init = _diff_lambda_init(i)
            segs = ((DIFF_QK_W, "q", BF16), (DIFF_QK_W, "k", (BF16, F32, Sp)),
                    (DIFF_BRANCH, "plain", (F32, None, Sp)), (DIFF_BRANCH, "plain", BF16))
            q, kp, ks, k_cache, vp, vs, z = proj_in(x, norm_g[i], mod4, i, diff_w_in, j, segs,
                                                    rope_tabs=rope_tabs, **common)
            lam_vecs = jnp.stack([diff_lambda_q1[j], diff_lambda_k1[j], diff_lambda_q2[j], diff_lambda_k2[j]])
            extra = (lam_vecs, diff_subln_g[j].reshape(1, DIFF_V_DIM))
            scr = [(DIFF_HEADS, 2 * LANES)]
            kern = functools.partial(_diff_attn_kernel, lam_init=lam_init)
            o_p = attention(kern, scr, q, kp, vp, Bp, Sp, 0, extra=extra)
            o_s = attention(kern, scr, q, ks, vs, Bs, Ss, Tp, ctx=(*ctx_diff, j), extra=extra)
            x = proj_out(o_p, o_s, z, x, mod4, i, diff_w_out, j, tile=PROJ_OUT_TILE, **common, **last)
            diff_k_list.append(k_cache)
            diff_v_list.append(vp.reshape(Bp, 1, Sp, DIFF_HEADS, DIFF_V_DIM))
        else:
            bq = _block_diag(mlstm_w_q[j]).astype(BF16)
            bk = _block_diag(mlstm_w_k[j])
            bkt = (bk * (DH ** -0.5)).transpose(0, 2, 1).astype(BF16)
            bv = _block_diag(mlstm_w_v[j]).astype(BF16)
            wg = jnp.concatenate([mlstm_w_gate_f[j][:, :H], mlstm_w_gate_f[j][:, H:],
                                  mlstm_w_gate_b[j][:, :H], mlstm_w_gate_b[j][:, H:]], axis=-1)
            wg = jnp.pad(wg, ((0, 0), (0, LANES - GATE_W))).astype(BF16).reshape(3, MLSTM_INNER, LANES)
            bg = jnp.pad(jnp.concatenate([mlstm_b_gate_f[j], mlstm_b_gate_b[j]]), (0, LANES - GATE_W)).reshape(1, LANES)
            assert not isinstance(x, tuple)
            xc, q, kt, v, gcol, grow, z = mlstm_in(x, norm_g[i], mod4, i, mlstm_w_in, j, mlstm_conv_w[j],
                                                   mlstm_conv_b[j], bq, bk.astype(BF16), bkt, bv, wg, bg, **common)
            hn_p, Cp, np_, mp_ = mlstm_prompt(q, kt, v, gcol, grow, mlstm_mh_norm_g[j], Bp)
            C0 = state_mlstm_C[:, j]
            n0 = state_mlstm_n[:, j].reshape(Bs, 2, H, 1, DH)
            m0 = jnp.broadcast_to(state_mlstm_m[:, j].transpose(0, 2, 1)[..., None, None], (Bs, H, 2, 1, LANES))
            hn_s = mlstm_sample(q, kt, v, gcol, grow, mlstm_mh_norm_g[j], C0, n0, m0, Tp, Bs, Ss)
            x = proj_out(hn_p, hn_s, z, x, mod4, i, mlstm_w_out, j, xc=xc, skip=mlstm_skip[j],
                         **common, **last)
            mC_list.append(Cp)
            mn_list.append(np_.reshape(Bp, 1, 2, H, DH))
            mm_list.append(mp_[:, :, 0, 0:2].transpose(0, 2, 1)[:, None])

    cat1 = lambda parts: parts[0] if len(parts) == 1 else jnp.concatenate(parts, axis=1)
    y_prompt, y_sample = x

    def from_transposed(parts, trail):
        t = cat1(parts)
        t = t.reshape(t.shape[:2] + trail + (Sp,))
        return jnp.moveaxis(t, -1, 2)

    return (y_prompt.reshape(Bp, Sp, D), y_sample.reshape(Bs, Ss, D),
            from_transposed(gqa_k_list, (GQA_KV_HEADS, HEAD_DIM)), from_transposed(gqa_v_list, (GQA_KV_HEADS, HEAD_DIM)),
            from_transposed(diff_k_list, (DIFF_HEADS, 2, HEAD_DIM)), cat1(diff_v_list),
            cat1(mC_list), cat1(mn_list), cat1(mm_list))
```

```python
import functools
import math

import jax
import jax.numpy as jnp
import numpy as np
from jax import lax
from jax.experimental import pallas as pl
from jax.experimental.pallas import tpu as pltpu

D_MODEL = 1024
DEPTH = 4
GRID_W = 64
N_MIXERS = 3
HEAD_DIM = 64
ROPE_THETA = 10000.0
NORM_EPS = 1e-6

GQA_HEADS = 16
GQA_KV_HEADS = 4
GQA_GROUP = GQA_HEADS // GQA_KV_HEADS
GQA_Q_W = GQA_HEADS * HEAD_DIM
GQA_KV_W = GQA_KV_HEADS * HEAD_DIM

DIFF_HEADS = 8
DIFF_QK_W = DIFF_HEADS * 2 * HEAD_DIM
DIFF_V_DIM = 2 * HEAD_DIM
DIFF_BRANCH = DIFF_HEADS * DIFF_V_DIM

MLSTM_INNER = 2 * D_MODEL
MLSTM_HEADS = 4
MLSTM_HEAD_DIM = MLSTM_INNER // MLSTM_HEADS
MLSTM_QKV_BLOCK = 4
MLSTM_CONV_K = 4

LANES = 128
SUBLANES = 8
VMEM_LIMIT_BYTES = 56 * 1024 * 1024

TOKEN_TILE = 512
PROJ_OUT_TILE = 1024
PROJ_N_TILE = 256
ATTN_Q_TILE = 256
ATTN_SHORT_SEQS = 2
MLSTM_CHUNK = 256
HEADWISE_CHUNK = LANES
GATE_W = 4 * MLSTM_HEADS
Q_SCALE = HEAD_DIM ** -0.5 * math.log2(math.e)

F32 = jnp.float32
BF16 = jnp.bfloat16


def _params(*semantics):
    return pltpu.CompilerParams(dimension_semantics=semantics, vmem_limit_bytes=VMEM_LIMIT_BYTES)


def _silu(x):
    return x * (1.0 / (1.0 + jnp.exp(-x)))


def _dot(a, b):
    return jnp.dot(a, b, preferred_element_type=F32)


def _dot_nt(a, b):
    return lax.dot_general(a, b, (((1,), (1,)), ((), ())), preferred_element_type=F32)


class TokenLayout:
    def __init__(self, n_prompt, seq, tile):
        self.prompt_tiles = n_prompt // tile
        self.tiles_per_seq = seq // tile

    def is_prompt(self, m):
        return m < self.prompt_tiles

    def cond_row(self, m):
        return jnp.where(m < self.prompt_tiles, 0, 1 + (m - self.prompt_tiles) // self.tiles_per_seq)

    def prompt_block(self, m):
        return jnp.minimum(m, self.prompt_tiles - 1)

    def sample_block(self, m):
        return jnp.maximum(m - self.prompt_tiles, 0)


def _mod_kernel(cond_ref, w_ref, b_ref, o_ref):
    a = _silu(cond_ref[...]).astype(BF16)
    o_ref[...] = _dot(a, w_ref[...].astype(BF16)) + b_ref[...]


def adaln_all(cond, w_mod, b_mod):
    tn = 1024
    return pl.pallas_call(
        _mod_kernel,
        grid=(DEPTH, 3 * D_MODEL // tn),
        in_specs=[
            pl.BlockSpec((SUBLANES, D_MODEL), lambda i, n: (0, 0)),
            pl.BlockSpec((None, D_MODEL, tn), lambda i, n: (i, 0, n)),
            pl.BlockSpec((None, 1, tn), lambda i, n: (i, 0, n)),
        ],
        out_specs=pl.BlockSpec((None, SUBLANES, tn), lambda i, n: (i, 0, n)),
        out_shape=jax.ShapeDtypeStruct((DEPTH, SUBLANES, 3 * D_MODEL), F32),
        compiler_params=_params("parallel", "parallel"),
        name="adaln_mod",
    )(cond, w_mod, b_mod.reshape(DEPTH, 1, 3 * D_MODEL))


def _swap_half_heads(y):
    w = y.shape[-1]
    lane = lax.broadcasted_iota(jnp.int32, y.shape, 1)
    upper = (lane & (HEAD_DIM // 2)) != 0
    return jnp.where(upper, pltpu.roll(y, HEAD_DIM // 2, 1), pltpu.roll(y, w - HEAD_DIM // 2, 1))


def _x_specs(x, lay, tm):
    if isinstance(x, tuple):
        return [pl.BlockSpec((tm, D_MODEL), lambda m: (lay.prompt_block(m), 0)),
                pl.BlockSpec((tm, D_MODEL), lambda m: (lay.sample_block(m), 0))], list(x)
    return [pl.BlockSpec((tm, D_MODEL), lambda m: (m, 0))], [x]


def _resident_spec(stacked, layer):
    return pl.BlockSpec((None,) + stacked.shape[1:], lambda m: (layer, 0, 0), pipeline_mode=pl.Buffered(1))


def _two_paths(is_prompt, path):
    pl.when(is_prompt)(functools.partial(path, True))
    pl.when(jnp.logical_not(is_prompt))(functools.partial(path, False))


def _proj_in_kernel(*refs, segs, layout, has_norm, has_rope, x_pair):
    it = iter(refs)
    x_refs = (next(it), next(it)) if x_pair else (next(it),) * 2
    g_ref, mod_ref, w_ref = next(it), next(it), next(it)
    if has_rope:
        cos_ref, sin_ref = next(it), next(it)
    if has_norm:
        gmat_ref, qg_ref, kg_ref = next(it), next(it), next(it)
    prev_refs = [next(it) if kind != "full" and kind[2] else None for _, _, kind in segs]
    out_refs = []
    for _, _, kind in segs:
        if kind == "full":
            ref = next(it)
            out_refs.append((ref, ref, None))
        else:
            has_prompt, has_t, _ = kind
            out_refs.append((next(it) if has_prompt else None, next(it), next(it) if has_t else None))
    cw = PROJ_N_TILE

    def store_transposed(ref, acc, col0):
        seq_len = ref.shape[3]
        for s in range(ref.shape[0]):
            ref[s, ref.shape[1] - 1, col0:col0 + cw, :] = acc[s * seq_len:(s + 1) * seq_len, :].T.astype(ref.dtype)

    def path(prompt):
        x = x_refs[0 if prompt else 1][...]
        y = x * lax.rsqrt(jnp.mean(x * x, axis=-1, keepdims=True) + NORM_EPS) * g_ref[...]
        shift = mod_ref[:, 0:D_MODEL]
        scale = mod_ref[:, D_MODEL:2 * D_MODEL]
        h = (y * (1.0 + scale) + shift).astype(BF16)
        n_chunks = sum(s[0] for s in segs) // cw
        accs = [_dot(h, w_ref[:, i * cw:(i + 1) * cw].astype(BF16)) for i in range(n_chunks)]
        col = 0
        for (width, mode, _), (p_ref, s_ref, t_ref), prev_ref in zip(segs, out_refs, prev_refs):
            o_ref = p_ref if prompt else s_ref
            if prompt and prev_ref is not None:
                t_ref[:, 0:prev_ref.shape[1]] = prev_ref[...]
            for c in range(width // cw):
                acc = accs[col // cw]
                col += cw
                if mode in ("q_norm", "k_norm"):
                    ss = _dot((acc * acc).astype(BF16), gmat_ref[...])
                    gvec = qg_ref[...] if mode == "q_norm" else kg_ref[...]
                    acc = acc * lax.rsqrt(ss * (1.0 / HEAD_DIM) + NORM_EPS) * gvec
                if has_rope and mode != "plain" and not prompt:
                    acc = acc * cos_ref[...] + _swap_half_heads(acc) * sin_ref[...]
                if mode in ("q_norm", "q"):
                    acc = acc * Q_SCALE
                if o_ref is not None:
                    o_ref[:, c * cw:(c + 1) * cw] = acc.astype(o_ref.dtype)
                if prompt and t_ref is not None:
                    store_transposed(t_ref, acc, c * cw)

    _two_paths(layout.is_prompt(pl.program_id(0)), path)


def proj_in(x, norm_g, mod4, layer, w, w_layer, segs, n_prompt, sample_seq, rope_tabs=None, head_norm=None,
            tile=TOKEN_TILE):
    tm, cw = tile, PROJ_N_TILE
    lay = TokenLayout(n_prompt, sample_seq, tm)
    T = x[0].shape[0] + x[1].shape[0] if isinstance(x, tuple) else x.shape[0]
    N = w.shape[2]
    assert N == sum(s[0] for s in segs) and all(s[0] % cw == 0 for s in segs)

    in_specs, args = _x_specs(x, lay, tm)
    in_specs += [
        pl.BlockSpec((1, D_MODEL), lambda m: (0, 0)),
        pl.BlockSpec((None, None, 1, 3 * D_MODEL), lambda m: (layer, lay.cond_row(m), 0, 0)),
        _resident_spec(w, w_layer),
    ]
    args += [norm_g.reshape(1, D_MODEL), mod4, w]
    if rope_tabs is not None:
        def pos_block(m):
            return (lay.sample_block(m) % lay.tiles_per_seq, 0)
        in_specs += [pl.BlockSpec((tm, cw), pos_block), pl.BlockSpec((tm, cw), pos_block)]
        args += list(rope_tabs)
    if head_norm is not None:
        in_specs += [pl.BlockSpec((cw, cw), lambda m: (0, 0)),
                     pl.BlockSpec((1, cw), lambda m: (0, 0)),
                     pl.BlockSpec((1, cw), lambda m: (0, 0))]
        args += list(head_norm)

    out_specs, out_shape, kern_segs = [], [], []
    t_block = lambda m: (lay.prompt_block(m), 0, 0, 0)
    for width, mode, out in segs:
        if isinstance(out, tuple):
            prompt_dt, t_dt, seq_len, prev = out
            kern_segs.append((width, mode, (prompt_dt is not None, t_dt is not None, prev is not None)))
            if prev is not None:
                in_specs.append(pl.BlockSpec((tm // seq_len,) + prev.shape[1:], t_block))
                args.append(prev)
            if prompt_dt is not None:
                out_specs.append(pl.BlockSpec((tm, width), lambda m: (lay.prompt_block(m), 0)))
                out_shape.append(jax.ShapeDtypeStruct((n_prompt, width), prompt_dt))
            out_specs.append(pl.BlockSpec((tm, width), lambda m: (lay.sample_block(m), 0)))
            out_shape.append(jax.ShapeDtypeStruct((T - n_prompt, width), BF16))
            if t_dt is not None:
                layers = 1 if prev is None else prev.shape[1] + 1
                out_specs.append(pl.BlockSpec((tm // seq_len, layers, width, seq_len), t_block))
                out_shape.append(jax.ShapeDtypeStruct((n_prompt // seq_len, layers, width, seq_len), t_dt))
        else:
            kern_segs.append((width, mode, "full"))
            out_specs.append(pl.BlockSpec((tm, width), lambda m: (m, 0)))
            out_shape.append(jax.ShapeDtypeStruct((T, width), out))

    kern = functools.partial(_proj_in_kernel, segs=tuple(kern_segs), layout=lay, x_pair=isinstance(x, tuple),
                             has_norm=head_norm is not None, has_rope=rope_tabs is not None)
    return pl.pallas_call(
        kern,
        grid=(T // tm,),
        in_specs=in_specs,
        out_specs=out_specs,
        out_shape=out_shape,
        compiler_params=_params("arbitrary"),
        name="proj_in",
    )(*args)


def _proj_out_kernel(*refs, layout, with_skip, x_pair, final):
    it = iter(refs)
    o_refs = (next(it), next(it))
    if with_skip:
        xc_ref, skip_ref = next(it), next(it)
    z_ref = next(it)
    x_refs = (next(it), next(it)) if x_pair else (next(it),) * 2
    mod_ref, w_ref = next(it), next(it)
    if final:
        fg_ref = next(it)
    out_refs = (next(it), next(it)) if final else (next(it),) * 2

    def path(prompt):
        sel = 0 if prompt else 1
        a = o_refs[sel][...].astype(F32)
        if with_skip:
            a = a + skip_ref[...] * xc_ref[...].astype(F32)
        a = (a * _silu(z_ref[...].astype(F32))).astype(BF16)
        gate = mod_ref[:, 2 * D_MODEL:3 * D_MODEL]
        x = x_refs[sel][...] + gate * _dot(a, w_ref[...].astype(BF16))
        if final:
            x = x * lax.rsqrt(jnp.mean(x * x, axis=-1, keepdims=True) + NORM_EPS) * fg_ref[...]
        out_refs[sel][...] = x

    _two_paths(layout.is_prompt(pl.program_id(0)), path)


def proj_out(o_prompt, o_sample, z, x, mod4, layer, w_out, w_layer, n_prompt, sample_seq, xc=None, skip=None,
             final_g=None, tile=TOKEN_TILE):
    T = z.shape[0]
    K = o_prompt.shape[1]
    tm = tile
    lay = TokenLayout(n_prompt, sample_seq, tm)
    row = lambda m: (m, 0)
    prompt_row = lambda m: (lay.prompt_block(m), 0)
    sample_row = lambda m: (lay.sample_block(m), 0)
    in_specs = [pl.BlockSpec((tm, K), prompt_row), pl.BlockSpec((tm, K), sample_row)]
    args = [o_prompt, o_sample]
    if xc is not None:
        in_specs += [pl.BlockSpec((tm, K), row), pl.BlockSpec((1, K), lambda m: (0, 0))]
        args += [xc, skip.reshape(1, K)]
    in_specs.append(pl.BlockSpec((tm, K), row))
    args.append(z)
    x_specs, x_args = _x_specs(x, lay, tm)
    in_specs += x_specs + [
        pl.BlockSpec((None, None, 1, 3 * D_MODEL), lambda m: (layer, lay.cond_row(m), 0, 0)),
        _resident_spec(w_out, w_layer),
    ]
    args += x_args + [mod4, w_out]
    if final_g is not None:
        in_specs.append(pl.BlockSpec((1, D_MODEL), lambda m: (0, 0)))
        args.append(final_g.reshape(1, D_MODEL))
        out_specs = [pl.BlockSpec((tm, D_MODEL), prompt_row), pl.BlockSpec((tm, D_MODEL), sample_row)]
        out_shape = [jax.ShapeDtypeStruct((n_prompt, D_MODEL), F32),
                     jax.ShapeDtypeStruct((T - n_prompt, D_MODEL), F32)]
    else:
        out_specs = pl.BlockSpec((tm, D_MODEL), row)
        out_shape = jax.ShapeDtypeStruct((T, D_MODEL), F32)
    kern = functools.partial(_proj_out_kernel, layout=lay, with_skip=xc is not None,
                             x_pair=isinstance(x, tuple), final=final_g is not None)
    return pl.pallas_call(
        kern,
        grid=(T // tm,),
        in_specs=in_specs,
        out_specs=out_specs,
        out_shape=out_shape,
        compiler_params=_params("arbitrary"),
        name="proj_out",
    )(*args)


ONES_ROWS = 16


def _low_half(shape):
    return lax.broadcasted_iota(jnp.int32, shape, 1) < HEAD_DIM


def _attend_group(items, rows, values_transposed=False):
    scores = [[_dot_nt(q, k) for k in k_parts] for q, k_parts, _, _ in items]
    probs = []
    for s in scores:
        m = functools.reduce(jnp.maximum, [jnp.max(x, axis=-1, keepdims=True) for x in s])
        probs.append([jnp.exp2(x - m).astype(BF16) for x in s])
    out = []
    for p_parts, (_, _, v_ref, head) in zip(probs, items):
        acc = None
        for p, r in zip(p_parts, rows):
            t = _dot_nt(v_ref[head, :, r], p) if values_transposed else _dot(p, v_ref[head, r, :])
            acc = t if acc is None else acc + t
        out.append(acc)
    return out


def _group_size(n_keys):
    return 16 if n_keys <= 512 else 4


def _row_parts(k_ref, kc_ref):
    n_new = k_ref.shape[0]
    rows = [slice(0, n_new)]
    if kc_ref is not None:
        rows.append(slice(n_new, n_new + kc_ref.shape[0]))
    return rows


def _per_sequence(body, seqs, row_refs, shared, scratch):
    for s in range(seqs):
        def view(ref):
            n = ref.shape[0] // seqs
            return ref if seqs == 1 else ref.at[pl.ds(s * n, n)]
        body(*[view(r) for r in row_refs], *shared, *[r.at[s] for r in scratch])


def _gqa_attn_kernel(*refs, has_ctx, seqs):
    if has_ctx:
        q_ref, k_ref, v_ref, kc_ref, vc_ref, o_ref, vt_ref = refs
    else:
        q_ref, k_ref, v_ref, o_ref, vt_ref = refs
        kc_ref = vc_ref = None
    _per_sequence(_gqa_attn_sequence, seqs, (q_ref, k_ref, v_ref, o_ref), (kc_ref, vc_ref), (vt_ref,))


def _gqa_attn_sequence(q_ref, k_ref, v_ref, o_ref, kc_ref, vc_ref, vt_ref):
    rows = _row_parts(k_ref, kc_ref)

    @pl.when(pl.program_id(1) == 0)
    def _():
        for src, r in zip([v_ref, vc_ref][:len(rows)], rows):
            first = lax.broadcasted_iota(jnp.int32, (ONES_ROWS, r.stop - r.start), 0) == 0
            for slab in range(GQA_KV_W // LANES):
                t = src[:, slab * LANES:(slab + 1) * LANES].astype(F32).T
                for half in range(LANES // HEAD_DIM):
                    c = slab * (LANES // HEAD_DIM) + half
                    vt_ref[c, 0:HEAD_DIM, r] = t[half * HEAD_DIM:(half + 1) * HEAD_DIM, :].astype(BF16)
                    vt_ref[c, HEAD_DIM:HEAD_DIM + ONES_ROWS, r] = jnp.where(first, 1.0, 0.0).astype(BF16)

    k_srcs = [k_ref, kc_ref][:len(rows)]
    group = _group_size(rows[-1].stop)
    n_pairs = GQA_HEADS // 2
    for p0 in range(0, n_pairs, group // 2):
        items = []
        for pair in range(p0, p0 + group // 2):
            c = pair // (GQA_GROUP // 2)
            q = q_ref[:, pair * LANES:(pair + 1) * LANES].astype(F32)
            low = _low_half(q.shape)
            q_sw = pltpu.roll(q, HEAD_DIM, 1)
            if c % 2 == 0:
                qe, qo = jnp.where(low, q, 0.0), jnp.where(low, q_sw, 0.0)
            else:
                qe, qo = jnp.where(low, 0.0, q_sw), jnp.where(low, 0.0, q)
            k_parts = [s[:, (c // 2) * LANES:(c // 2 + 1) * LANES].astype(BF16) for s in k_srcs]
            items += [(qe.astype(BF16), k_parts, vt_ref, c), (qo.astype(BF16), k_parts, vt_ref, c)]
        accs = _attend_group(items, rows, values_transposed=True)
        for i, pair in enumerate(range(p0, p0 + group // 2)):
            o_t = [a[0:HEAD_DIM] / a[HEAD_DIM:HEAD_DIM + 1] for a in accs[2 * i:2 * i + 2]]
            o_ref[:, pair * LANES:(pair + 1) * LANES] = jnp.concatenate(o_t, axis=0).T.astype(o_ref.dtype)


def _diff_attn_kernel(*refs, has_ctx, seqs, lam_init):
    if has_ctx:
        q_ref, k_ref, v_ref, kc_ref, vc_ref, lam_ref, g_ref, o_ref, vx_ref = refs
    else:
        q_ref, k_ref, v_ref, lam_ref, g_ref, o_ref, vx_ref = refs
        kc_ref = vc_ref = None
    _per_sequence(functools.partial(_diff_attn_sequence, lam_init=lam_init), seqs, (q_ref, k_ref, v_ref, o_ref),
                  (kc_ref, vc_ref, lam_ref, g_ref), (vx_ref,))


def _diff_attn_sequence(q_ref, k_ref, v_ref, o_ref, kc_ref, vc_ref, lam_ref, g_ref, vx_ref, *, lam_init):
    rows = _row_parts(k_ref, kc_ref)

    @pl.when(pl.program_id(1) == 0)
    def _():
        for src, r in zip([v_ref, vc_ref][:len(rows)], rows):
            for h in range(DIFF_HEADS):
                vx_ref[h, r, 0:LANES] = src[:, h * LANES:(h + 1) * LANES].astype(BF16)
                vx_ref[h, r, LANES:2 * LANES] = jnp.ones((r.stop - r.start, LANES), BF16)

    lv = lam_ref[...]
    s1 = jnp.sum(lv[0:1] * lv[1:2], axis=-1, keepdims=True)
    s2 = jnp.sum(lv[2:3] * lv[3:4], axis=-1, keepdims=True)
    lam = jnp.exp(s1) - jnp.exp(s2) + lam_init
    k_srcs = [k_ref, kc_ref][:len(rows)]
    group = _group_size(rows[-1].stop)
    for h0 in range(0, DIFF_HEADS, group // 2):
        items = []
        for h in range(h0, h0 + group // 2):
            q = q_ref[:, h * LANES:(h + 1) * LANES].astype(F32)
            low = _low_half(q.shape)
            k_parts = [s[:, h * LANES:(h + 1) * LANES].astype(BF16) for s in k_srcs]
            items += [(jnp.where(low, q, 0.0).astype(BF16), k_parts, vx_ref, h),
                      (jnp.where(low, 0.0, q).astype(BF16), k_parts, vx_ref, h)]
        accs = _attend_group(items, rows)
        for i, h in enumerate(range(h0, h0 + group // 2)):
            a0, a1 = accs[2 * i], accs[2 * i + 1]
            o = a0[:, 0:LANES] / a0[:, LANES:2 * LANES] - lam * (a1[:, 0:LANES] / a1[:, LANES:2 * LANES])
            o = o * lax.rsqrt(jnp.mean(o * o, axis=-1, keepdims=True) + NORM_EPS) * g_ref[...]
            o_ref[:, h * LANES:(h + 1) * LANES] = (o * (1.0 - lam_init)).astype(o_ref.dtype)


def attention(kernel, v_scratch, q, k, v, batch, sq, q_row0, ctx=None, extra=()):
    tq = ATTN_Q_TILE
    nq = sq // tq
    seqs = ATTN_SHORT_SEQS if nq == 1 and ctx is None and batch % ATTN_SHORT_SEQS == 0 else 1
    batch, sq, tq = batch // seqs, sq * seqs, tq * seqs
    off = q_row0 // tq
    in_specs = [
        pl.BlockSpec((tq, q.shape[1]), lambda b, i: (b * nq + i + off, 0)),
        pl.BlockSpec((sq, k.shape[1]), lambda b, i: (b, 0)),
        pl.BlockSpec((sq, v.shape[1]), lambda b, i: (b, 0)),
    ]
    args = [q, k, v]
    sk = sq
    if ctx is not None:
        kc, vc, layer = ctx
        sk += kc.shape[2]
        in_specs += [pl.BlockSpec((None, None) + kc.shape[2:], lambda b, i: (b, layer, 0, 0)),
                     pl.BlockSpec((None, None) + vc.shape[2:], lambda b, i: (b, layer, 0, 0))]
        args += [kc, vc]
    for a in extra:
        in_specs.append(pl.BlockSpec(a.shape, lambda b, i: (0, 0)))
    args += list(extra)
    return pl.pallas_call(
        functools.partial(kernel, has_ctx=ctx is not None, seqs=seqs),
        grid=(batch, nq),
        in_specs=in_specs,
        out_specs=pl.BlockSpec((tq, D_MODEL), lambda b, i: (b * nq + i, 0)),
        out_shape=jax.ShapeDtypeStruct((batch * sq, D_MODEL), BF16),
        scratch_shapes=[pltpu.VMEM((seqs,) + tuple(sk // seqs if d == "keys" else d for d in shape), BF16)
                        for shape in v_scratch],
        compiler_params=_params("parallel", "arbitrary"),
        name="attention",
    )(*args)


def _log_sigmoid(x):
    return jnp.minimum(x, 0.0) - jnp.log1p(jnp.exp(-jnp.abs(x)))


def _mlstm_in_kernel(cur_ref, prev_ref, next_ref, ng_ref, mod_ref, w_ref, cw_ref, cb_ref, bq_ref, bk_ref, bkt_ref,
                     bv_ref, wg_ref, bg_ref, xc_ref, q_ref, kt_ref, v_ref, gcol_ref, grow_ref, z_ref,
                     *, n_prompt_tiles, tiles_per_seq):
    L = MLSTM_CHUNK
    cw = PROJ_N_TILE
    t = pl.program_id(0)
    st = jnp.maximum(t - n_prompt_tiles, 0) % tiles_per_seq
    first = (t < n_prompt_tiles) | (st == 0)
    last = (t < n_prompt_tiles) | (st == tiles_per_seq - 1)
    n_ext = L + 2 * SUBLANES
    x = jnp.concatenate([prev_ref[...], cur_ref[...], next_ref[...]], axis=0)
    y = x * lax.rsqrt(jnp.mean(x * x, axis=-1, keepdims=True) + NORM_EPS) * ng_ref[...]
    h = (y * (1.0 + mod_ref[:, D_MODEL:2 * D_MODEL]) + mod_ref[:, 0:D_MODEL]).astype(BF16)
    n_xm = MLSTM_INNER // cw
    xm_chunks = [_dot(h, w_ref[:, i * cw:(i + 1) * cw].astype(BF16)) for i in range(n_xm)]
    z_chunks = [_dot(h[SUBLANES:SUBLANES + L], w_ref[:, (n_xm + i) * cw:(n_xm + i + 1) * cw].astype(BF16))
                for i in range(n_xm)]
    for i, zc in enumerate(z_chunks):
        z_ref[:, i * cw:(i + 1) * cw] = zc.astype(z_ref.dtype)
    row_e = lax.broadcasted_iota(jnp.int32, (n_ext, 1), 0)
    inside = ((row_e >= SUBLANES) | jnp.logical_not(first)) & ((row_e < SUBLANES + L) | jnp.logical_not(last))
    ext = jnp.where(inside, jnp.concatenate(xm_chunks, axis=1), 0.0)
    cur = ext[SUBLANES:SUBLANES + L]
    pad_l = MLSTM_CONV_K // 2
    acc = cb_ref[...] + cw_ref[pad_l:pad_l + 1, :] * cur
    for j in range(MLSTM_CONV_K):
        if j == pad_l:
            continue
        sh = (pad_l - j) % n_ext
        acc = acc + cw_ref[j:j + 1, :] * pltpu.roll(ext, sh, 0)[SUBLANES:SUBLANES + L]
    xc = _silu(acc)
    xc_ref[...] = xc.astype(xc_ref.dtype)

    g = jnp.zeros((L, LANES), F32)
    for c in range(MLSTM_INNER // HEADWISE_CHUNK):
        sl = slice(c * HEADWISE_CHUNK, (c + 1) * HEADWISE_CHUNK)
        xcb = xc[:, sl].astype(BF16)
        xmb = cur[:, sl].astype(BF16)
        qc = _dot(xcb, bq_ref[c]).astype(BF16)
        kc = _dot(xcb, bk_ref[c]).astype(BF16)
        vc = _dot(xmb, bv_ref[c]).astype(BF16)
        q_ref[:, sl] = qc
        v_ref[:, sl] = vc
        kt_ref[sl, :] = _dot_nt(bkt_ref[c], xcb).astype(BF16)
        g = g + _dot(qc, wg_ref[0, sl, :]) + _dot(kc, wg_ref[1, sl, :]) + _dot(vc, wg_ref[2, sl, :])
    g = g + bg_ref[...]

    ls = _log_sigmoid(g)
    row = lax.broadcasted_iota(jnp.int32, (L, LANES), 0)
    pre, suf = ls, ls
    s = 1
    while s < L:
        pre = pre + jnp.where(row >= s, pltpu.roll(pre, s, 0), 0.0)
        suf = suf + jnp.where(row < L - s, pltpu.roll(suf, L - s, 0), 0.0)
        s *= 2
    lane = lax.broadcasted_iota(jnp.int32, (L, LANES), 1)
    H = MLSTM_HEADS
    out = jnp.where((lane >= H) & (lane < 2 * H), pre, jnp.where((lane >= 3 * H) & (lane < 4 * H), suf, g))
    gcol_ref[...] = out[:, 0:GATE_W]
    grow_ref[...] = out.T[0:GATE_W, :]


def mlstm_in(x, norm_g, mod4, layer, w, w_layer, conv_w, conv_b, bq, bk, bkt, bv, wg, bg, n_prompt, sample_seq):
    T = x.shape[0]
    L = MLSTM_CHUNK
    n_tiles = T // L
    r8 = L // SUBLANES
    n_blk8 = T // SUBLANES
    C = MLSTM_INNER
    lay = TokenLayout(n_prompt, sample_seq, L)
    kern = functools.partial(_mlstm_in_kernel, n_prompt_tiles=n_prompt // L, tiles_per_seq=sample_seq // L)
    const3 = lambda t: (0, 0, 0)
    return pl.pallas_call(
        kern,
        grid=(n_tiles,),
        in_specs=[
            pl.BlockSpec((L, D_MODEL), lambda t: (t, 0)),
            pl.BlockSpec((SUBLANES, D_MODEL), lambda t: (jnp.maximum(t * r8 - 1, 0), 0)),
            pl.BlockSpec((SUBLANES, D_MODEL), lambda t: (jnp.minimum((t + 1) * r8, n_blk8 - 1), 0)),
            pl.BlockSpec((1, D_MODEL), lambda t: (0, 0)),
            pl.BlockSpec((None, None, 1, 3 * D_MODEL), lambda t: (layer, lay.cond_row(t), 0, 0)),
            _resident_spec(w, w_layer),
            pl.BlockSpec((MLSTM_CONV_K, C), lambda t: (0, 0)),
            pl.BlockSpec((1, C), lambda t: (0, 0)),
            pl.BlockSpec(bq.shape, const3),
            pl.BlockSpec(bk.shape, const3),
            pl.BlockSpec(bkt.shape, const3),
            pl.BlockSpec(bv.shape, const3),
            pl.BlockSpec(wg.shape, const3),
            pl.BlockSpec((1, LANES), lambda t: (0, 0)),
        ],
        out_specs=[
            pl.BlockSpec((L, C), lambda t: (t, 0)),
            pl.BlockSpec((L, C), lambda t: (t, 0)),
            pl.BlockSpec((None, C, L), lambda t: (t, 0, 0)),
            pl.BlockSpec((L, C), lambda t: (t, 0)),
            pl.BlockSpec((L, GATE_W), lambda t: (t, 0)),
            pl.BlockSpec((None, GATE_W, L), lambda t: (t, 0, 0)),
            pl.BlockSpec((L, C), lambda t: (t, 0)),
        ],
        out_shape=[
            jax.ShapeDtypeStruct((T, C), BF16),
            jax.ShapeDtypeStruct((T, C), BF16),
            jax.ShapeDtypeStruct((n_tiles, C, L), BF16),
            jax.ShapeDtypeStruct((T, C), BF16),
            jax.ShapeDtypeStruct((T, GATE_W), F32),
            jax.ShapeDtypeStruct((n_tiles, GATE_W, L), F32),
            jax.ShapeDtypeStruct((T, C), BF16),
        ],
        compiler_params=_params("parallel"),
        name="mlstm_in",
    )(x, x, x, norm_g.reshape(1, D_MODEL), mod4, w, conv_w, conv_b.reshape(1, C), bq, bk, bkt, bv, wg, bg)


def _gate_cols(gcol, h):
    lane = lax.broadcasted_iota(jnp.int32, gcol.shape, 1)
    pick = lambda idx: jnp.sum(jnp.where(lane == idx, gcol, 0.0), axis=-1, keepdims=True)
    return pick(MLSTM_HEADS + h), pick(3 * MLSTM_HEADS + h)


def _chunk_weights(qk, b_col, b_row, i_row, tot, causal_mask, m_prev):
    ar = i_row - b_row
    d = jnp.where(causal_mask, b_col + ar, -jnp.inf)
    inter = b_col + m_prev
    mt = jnp.maximum(jnp.max(d, axis=-1, keepdims=True), inter)
    s = qk * jnp.exp(d - mt)
    g = tot + ar
    m_new = jnp.maximum(tot + m_prev, jnp.max(g, axis=-1, keepdims=True))
    w = jnp.exp(g - m_new)
    w_hi = w.astype(BF16).astype(F32)
    rid = lax.broadcasted_iota(jnp.int32, (SUBLANES, w.shape[-1]), 0)
    w8 = jnp.where(rid == 0, w_hi, jnp.where(rid == 1, w - w_hi, 0.0)).astype(BF16)
    return dict(s=s.astype(BF16), den=jnp.sum(s, axis=-1, keepdims=True), mt=mt, inter=inter, w=w, w8=w8,
                m_new=m_new, decay_arg=tot + m_prev - m_new)


def _chunk_matmuls(wts, q, kt, v, C_prev):
    num = _dot(wts["s"], v)
    qC = None if C_prev is None else _dot(q, C_prev.astype(BF16))
    C_add = _dot((kt.astype(F32) * wts["w"]).astype(BF16), v)
    n8 = _dot_nt(wts["w8"], kt)
    return num, qC, C_add, n8[0:1] + n8[1:2]


def _chunk_finish(wts, mats, q, C_prev, n_prev):
    num, qC, C_new, n_new = mats
    den, mt = wts["den"], wts["mt"]
    if C_prev is not None:
        w_inter = jnp.exp(wts["inter"] - mt)
        num = num + w_inter * qC
        den = den + w_inter * jnp.sum(q.astype(F32) * n_prev, axis=-1, keepdims=True)
        decay = jnp.exp(wts["decay_arg"])
        C_new = decay * C_prev + C_new
        n_new = decay * n_prev + n_new
    h = num / jnp.maximum(jnp.abs(den), jnp.exp(-mt))
    return h, C_new, n_new, wts["m_new"]


def _chunk_dir(qk, q, kt, v, b_col, b_row, i_row, tot, causal_mask, m_prev, C_prev, n_prev):
    wts = _chunk_weights(qk, b_col, b_row, i_row, tot, causal_mask, m_prev)
    return _chunk_finish(wts, _chunk_matmuls(wts, q, kt, v, C_prev), q, C_prev, n_prev)


def _head_layernorm(h, g):
    mu = jnp.mean(h, axis=-1, keepdims=True)
    hc = h - mu
    var = jnp.mean(hc * hc, axis=-1, keepdims=True)
    return hc * lax.rsqrt(var + NORM_EPS) * g


def _chunk_masks(L):
    t_idx = lax.broadcasted_iota(jnp.int32, (L, L), 0)
    s_idx = lax.broadcasted_iota(jnp.int32, (L, L), 1)
    return s_idx <= t_idx, s_idx >= t_idx


def _mlstm_prompt_kernel(q_ref, kt_ref, v_ref, gcol_ref, grow_ref, mhg_ref,
                         hn_ref, C_ref, n_ref, m_ref):
    L = MLSTM_CHUNK
    H, DH = MLSTM_HEADS, MLSTM_HEAD_DIM
    fmask, bmask = _chunk_masks(L)
    zero = jnp.zeros((1, 1), F32)
    lane = lax.broadcasted_iota(jnp.int32, (1, LANES), 1)
    col = lambda idx: gcol_ref[:, idx:idx + 1]
    row = lambda idx: grow_ref[idx:idx + 1, :]
    for h in range(H):
        ch = slice(h * DH, (h + 1) * DH)
        q, kt, v = q_ref[:, ch], kt_ref[ch, :], v_ref[:, ch]
        qk = _dot(q, kt)
        bf_row, bb_row = row(H + h), row(3 * H + h)
        hf, Cf, nf, mf = _chunk_dir(qk, q, kt, v, col(H + h), bf_row, row(h), bf_row[:, L - 1:L], fmask,
                                    zero, None, None)
        hb, Cb, nb, mb = _chunk_dir(qk, q, kt, v, col(3 * H + h), bb_row, row(2 * H + h), bb_row[:, 0:1], bmask,
                                    zero, None, None)
        hn_ref[:, ch] = _head_layernorm(hf + hb, mhg_ref[:, ch]).astype(hn_ref.dtype)
        C_ref[0, h] = Cf
        C_ref[1, h] = Cb
        n_ref[0, h] = nf
        n_ref[1, h] = nb
        m_ref[h] = jnp.where(lane == 0, mf, jnp.where(lane == 1, mb, 0.0))


def mlstm_prompt(q, kt, v, gcol, grow, mh_g, batch):
    L, H, DH, C = MLSTM_CHUNK, MLSTM_HEADS, MLSTM_HEAD_DIM, MLSTM_INNER
    return pl.pallas_call(
        _mlstm_prompt_kernel,
        grid=(batch,),
        in_specs=[
            pl.BlockSpec((L, C), lambda b: (b, 0)),
            pl.BlockSpec((None, C, L), lambda b: (b, 0, 0)),
            pl.BlockSpec((L, C), lambda b: (b, 0)),
            pl.BlockSpec((L, GATE_W), lambda b: (b, 0)),
            pl.BlockSpec((None, GATE_W, L), lambda b: (b, 0, 0)),
            pl.BlockSpec((1, C), lambda b: (0, 0)),
        ],
        out_specs=[
            pl.BlockSpec((L, C), lambda b: (b, 0)),
            pl.BlockSpec((None, None, 2, H, DH, DH), lambda b: (b, 0, 0, 0, 0, 0)),
            pl.BlockSpec((None, 2, H, 1, DH), lambda b: (b, 0, 0, 0, 0)),
            pl.BlockSpec((None, H, 1, LANES), lambda b: (b, 0, 0, 0)),
        ],
        out_shape=[
            jax.ShapeDtypeStruct((batch * L, C), BF16),
            jax.ShapeDtypeStruct((batch, 1, 2, H, DH, DH), F32),
            jax.ShapeDtypeStruct((batch, 2, H, 1, DH), F32),
            jax.ShapeDtypeStruct((batch, H, 1, LANES), F32),
        ],
        compiler_params=_params("parallel"),
        name="mlstm_prompt",
    )(q, kt, v, gcol, grow, mh_g.reshape(1, C))


def _mlstm_sample_kernel(q_ref, kt_ref, v_ref, gcol_ref, grow_ref, mhg_ref, C0_ref, n0_ref, m0_ref,
                         hn_ref, hf_ref, hb_ref, C_ref, n_ref, m_ref, *, n_chunks):
    L = MLSTM_CHUNK
    H = MLSTM_HEADS
    h = pl.program_id(1)
    fmask, bmask = _chunk_masks(L)
    C_ref[...] = C0_ref[...]
    n_ref[...] = n0_ref[...]
    m_ref[...] = m0_ref[...]

    def one(c, direction):
        rows = pl.ds(pl.multiple_of(c * L, L), L)
        q, kt, v = q_ref[rows, :], kt_ref[c], v_ref[rows, :]
        qk = _dot(q, kt)
        gcol = gcol_ref[rows, :]
        b_col = _gate_cols(gcol, h)[direction]
        grow = lambda idx: grow_ref[c, pl.ds(idx, 1), :]
        b_row = grow((2 * direction + 1) * H + h)
        i_row = grow(2 * direction * H + h)
        tot = b_row[:, L - 1:L] if direction == 0 else b_row[:, 0:1]
        mask = fmask if direction == 0 else bmask
        m_prev = m_ref[direction][:, 0:1]
        hh, C_new, n_new, m_new = _chunk_dir(qk, q, kt, v, b_col, b_row, i_row, tot, mask, m_prev,
                                             C_ref[direction], n_ref[direction])
        C_ref[direction] = C_new
        n_ref[direction] = n_new
        m_ref[direction] = jnp.broadcast_to(m_new, (1, LANES))
        (hf_ref if direction == 0 else hb_ref)[rows, :] = hh

    def body(i, carry):
        one(i, 0)
        one(n_chunks - 1 - i, 1)
        return carry

    lax.fori_loop(0, n_chunks, body, 0)

    def norm(c, carry):
        rows = pl.ds(pl.multiple_of(c * L, L), L)
        hn_ref[rows, :] = _head_layernorm(hf_ref[rows, :] + hb_ref[rows, :], mhg_ref[...]).astype(hn_ref.dtype)
        return carry

    lax.fori_loop(0, n_chunks, norm, 0)


def mlstm_sample(q, kt, v, gcol, grow, mh_g, C0, n0, m0, n_prompt, batch, seq):
    L, H, DH = MLSTM_CHUNK, MLSTM_HEADS, MLSTM_HEAD_DIM
    nc = seq // L
    row_off = n_prompt // seq
    tile_off = n_prompt // L // nc
    return pl.pallas_call(
        functools.partial(_mlstm_sample_kernel, n_chunks=nc),
        grid=(batch, H),
        in_specs=[
            pl.BlockSpec((seq, DH), lambda b, h: (b + row_off, h)),
            pl.BlockSpec((nc, DH, L), lambda b, h: (b + tile_off, h, 0)),
            pl.BlockSpec((seq, DH), lambda b, h: (b + row_off, h)),
            pl.BlockSpec((seq, GATE_W), lambda b, h: (b + row_off, 0)),
            pl.BlockSpec((nc, GATE_W, L), lambda b, h: (b + tile_off, 0, 0)),
            pl.BlockSpec((1, DH), lambda b, h: (0, h)),
            pl.BlockSpec((None, 2, None, DH, DH), lambda b, h: (b, 0, h, 0, 0)),
            pl.BlockSpec((None, 2, None, 1, DH), lambda b, h: (b, 0, h, 0, 0)),
            pl.BlockSpec((None, None, 2, 1, LANES), lambda b, h: (b, h, 0, 0, 0)),
        ],
        out_specs=pl.BlockSpec((seq, DH), lambda b, h: (b, h)),
        out_shape=jax.ShapeDtypeStruct((batch * seq, MLSTM_INNER), BF16),
        scratch_shapes=[
            pltpu.VMEM((seq, DH), F32),
            pltpu.VMEM((seq, DH), F32),
            pltpu.VMEM((2, DH, DH), F32),
            pltpu.VMEM((2, 1, DH), F32),
            pltpu.VMEM((2, 1, LANES), F32),
        ],
        compiler_params=_params("parallel", "parallel"),
        name="mlstm_sample",
    )(q, kt, v, gcol, grow, mh_g.reshape(1, MLSTM_INNER), C0, n0, m0)


def _rope_tables(seq):
    t = np.arange(seq)
    row = (t // GRID_W).astype(np.float64)
    col = (t % GRID_W).astype(np.float64)
    n_freq = HEAD_DIM // 4
    inv = ROPE_THETA ** (-np.arange(n_freq, dtype=np.float64) / n_freq)
    ang = np.concatenate([row[:, None] * inv, col[:, None] * inv], axis=-1)
    cos, sin = np.cos(ang), np.sin(ang)
    reps = PROJ_N_TILE // HEAD_DIM
    return (jnp.asarray(np.tile(np.concatenate([cos, cos], axis=-1), (1, reps)), F32),
            jnp.asarray(np.tile(np.concatenate([-sin, sin], axis=-1), (1, reps)), F32))


def _block_diag(w):
    blk, ch = MLSTM_QKV_BLOCK, HEADWISE_CHUNK
    w4 = w.reshape(-1, ch // blk, blk, blk)
    rows = w4.transpose(0, 2, 1, 3).reshape(-1, 1, blk, ch)
    full = jnp.broadcast_to(rows, (rows.shape[0], ch // blk, blk, ch)).reshape(-1, ch, ch)
    block_of = jnp.arange(ch) // blk
    return jnp.where(block_of[:, None] == block_of[None, :], full, 0.0)


def _diff_lambda_init(layer_idx):
    return 0.8 - 0.6 * math.exp(-0.3 * layer_idx)


def kernel(x_prompt, x_sample, cache_gqa_k, cache_gqa_v, cache_diff_k, cache_diff_v, state_mlstm_C, state_mlstm_n, state_mlstm_m, c, c_ctx, norm_g, w_mod, b_mod, gqa_w_in, gqa_q_norm_g, gqa_k_norm_g, gqa_w_out, diff_w_in, diff_lambda_q1, diff_lambda_k1, diff_lambda_q2, diff_lambda_k2, diff_subln_g, diff_w_out, mlstm_w_in, mlstm_conv_w, mlstm_conv_b, mlstm_w_q, mlstm_w_k, mlstm_w_v, mlstm_w_gate_f, mlstm_b_gate_f, mlstm_w_gate_b, mlstm_b_gate_b, mlstm_mh_norm_g, mlstm_skip, mlstm_w_out, final_norm_g):
    Bp, Sp, D = x_prompt.shape
    Bs, Ss, _ = x_sample.shape
    Tp, Ts = Bp * Sp, Bs * Ss
    assert Sp == MLSTM_CHUNK and Ss % TOKEN_TILE == 0 and Tp % TOKEN_TILE == 0
    H, DH = MLSTM_HEADS, MLSTM_HEAD_DIM
    past = cache_gqa_k.shape[2]

    x = (x_prompt.reshape(Tp, D), x_sample.reshape(Ts, D))
    cond = jnp.zeros((SUBLANES, D), F32).at[0].set(c_ctx).at[1:1 + Bs].set(c)
    mod4 = adaln_all(cond, w_mod, b_mod).reshape(DEPTH, SUBLANES, 1, 3 * D)
    rope_tabs = _rope_tables(Ss)
    gi = np.arange(PROJ_N_TILE) // HEAD_DIM
    gmat = jnp.asarray(gi[:, None] == gi[None, :], BF16)
    tile_g = lambda g: jnp.tile(g, PROJ_N_TILE // HEAD_DIM).reshape(1, PROJ_N_TILE)
    ctx_gqa = (cache_gqa_k.reshape(Bs, -1, past, GQA_KV_W), cache_gqa_v.reshape(Bs, -1, past, GQA_KV_W))
    ctx_diff = (cache_diff_k.reshape(Bs, -1, past, DIFF_QK_W), cache_diff_v.reshape(Bs, -1, past, DIFF_BRANCH))

    gqa_k_t = gqa_v_t = diff_k_t = None
    diff_v_list, mC_list, mn_list, mm_list = [], [], [], []
    for i in range(DEPTH):
        kind, j = i % N_MIXERS, i // N_MIXERS
        common = dict(n_prompt=Tp, sample_seq=Ss)
        last = dict(final_g=final_norm_g) if i == DEPTH - 1 else {}
        if kind == 0:
            segs = ((GQA_Q_W, "q_norm", BF16), (GQA_KV_W, "k_norm", (BF16, F32, Sp, gqa_k_t)),
                    (GQA_KV_W, "plain", (BF16, F32, Sp, gqa_v_t)), (GQA_Q_W, "plain", BF16))
            q, kp, ks, gqa_k_t, vp, vs, gqa_v_t, z = proj_in(
                x, norm_g[i], mod4, i, gqa_w_in, j, segs, rope_tabs=rope_tabs,
                head_norm=(gmat, tile_g(gqa_q_norm_g[j]), tile_g(gqa_k_norm_g[j])), **common)
            scr = [(GQA_KV_HEADS, HEAD_DIM + ONES_ROWS, "keys")]
            kern = _gqa_attn_kernel
            o_p = attention(kern, scr, q, kp, vp, Bp, Sp, 0)
            o_s = attention(kern, scr, q, ks, vs, Bs, Ss, Tp, ctx=(*ctx_gqa, j))
            x = proj_out(o_p, o_s, z, x, mod4, i, gqa_w_out, j, tile=PROJ_OUT_TILE, **common, **last)
        elif kind == 1:
            lam_init = _diff_lambda_init(i)
            segs = ((DIFF_QK_W, "q", BF16), (DIFF_QK_W, "k", (BF16, F32, Sp, diff_k_t)),
                    (DIFF_BRANCH, "plain", (F32, None, Sp, None)), (DIFF_BRANCH, "plain", BF16))
            q, kp, ks, diff_k_t, vp, vs, z = proj_in(x, norm_g[i], mod4, i, diff_w_in, j, segs,
                                                    rope_tabs=rope_tabs, **common)
            lam_vecs = jnp.stack([diff_lambda_q1[j], diff_lambda_k1[j], diff_lambda_q2[j], diff_lambda_k2[j]])
            extra = (lam_vecs, diff_subln_g[j].reshape(1, DIFF_V_DIM))
            scr = [(DIFF_HEADS, "keys", 2 * LANES)]
            kern = functools.partial(_diff_attn_kernel, lam_init=lam_init)
            o_p = attention(kern, scr, q, kp, vp, Bp, Sp, 0, extra=extra)
            o_s = attention(kern, scr, q, ks, vs, Bs, Ss, Tp, ctx=(*ctx_diff, j), extra=extra)
            x = proj_out(o_p, o_s, z, x, mod4, i, diff_w_out, j, tile=PROJ_OUT_TILE, **common, **last)
            diff_v_list.append(vp.reshape(Bp, 1, Sp, DIFF_HEADS, DIFF_V_DIM))
        else:
            bq = _block_diag(mlstm_w_q[j]).astype(BF16)
            bk = _block_diag(mlstm_w_k[j])
            bkt = (bk * (DH ** -0.5)).transpose(0, 2, 1).astype(BF16)
            bv = _block_diag(mlstm_w_v[j]).astype(BF16)
            wg = jnp.concatenate([mlstm_w_gate_f[j][:, :H], mlstm_w_gate_f[j][:, H:],
                                  mlstm_w_gate_b[j][:, :H], mlstm_w_gate_b[j][:, H:]], axis=-1)
            wg = jnp.pad(wg, ((0, 0), (0, LANES - GATE_W))).astype(BF16).reshape(3, MLSTM_INNER, LANES)
            bg = jnp.pad(jnp.concatenate([mlstm_b_gate_f[j], mlstm_b_gate_b[j]]), (0, LANES - GATE_W)).reshape(1, LANES)
            assert not isinstance(x, tuple)
            xc, q, kt, v, gcol, grow, z = mlstm_in(x, norm_g[i], mod4, i, mlstm_w_in, j, mlstm_conv_w[j],
                                                   mlstm_conv_b[j], bq, bk.astype(BF16), bkt, bv, wg, bg, **common)
            hn_p, Cp, np_, mp_ = mlstm_prompt(q, kt, v, gcol, grow, mlstm_mh_norm_g[j], Bp)
            C0 = state_mlstm_C[:, j]
            n0 = state_mlstm_n[:, j].reshape(Bs, 2, H, 1, DH)
            m0 = jnp.broadcast_to(state_mlstm_m[:, j].transpose(0, 2, 1)[..., None, None], (Bs, H, 2, 1, LANES))
            hn_s = mlstm_sample(q, kt, v, gcol, grow, mlstm_mh_norm_g[j], C0, n0, m0, Tp, Bs, Ss)
            x = proj_out(hn_p, hn_s, z, x, mod4, i, mlstm_w_out, j, xc=xc, skip=mlstm_skip[j],
                         **common, **last)
            mC_list.append(Cp)
            mn_list.append(np_.reshape(Bp, 1, 2, H, DH))
            mm_list.append(mp_[:, :, 0, 0:2].transpose(0, 2, 1)[:, None])

    cat1 = lambda parts: parts[0] if len(parts) == 1 else jnp.concatenate(parts, axis=1)
    y_prompt, y_sample = x

    def from_transposed(t, trail):
        t = t.reshape(t.shape[:2] + trail + (Sp,))
        return jnp.moveaxis(t, -1, 2)

    return (y_prompt.reshape(Bp, Sp, D), y_sample.reshape(Bs, Ss, D),
            from_transposed(gqa_k_t, (GQA_KV_HEADS, HEAD_DIM)), from_transposed(gqa_v_t, (GQA_KV_HEADS, HEAD_DIM)),
            from_transposed(diff_k_t, (DIFF_HEADS, 2, HEAD_DIM)), cat1(diff_v_list),
            cat1(mC_list), cat1(mn_list), cat1(mm_list))
```

```python
import functools
import math

import jax
import jax.numpy as jnp
import numpy as np
from jax import lax
from jax.experimental import pallas as pl
from jax.experimental.pallas import tpu as pltpu

D_MODEL = 1024
DEPTH = 4
GRID_W = 64
N_MIXERS = 3
HEAD_DIM = 64
ROPE_THETA = 10000.0
NORM_EPS = 1e-6

GQA_HEADS = 16
GQA_KV_HEADS = 4
GQA_GROUP = GQA_HEADS // GQA_KV_HEADS
GQA_Q_W = GQA_HEADS * HEAD_DIM
GQA_KV_W = GQA_KV_HEADS * HEAD_DIM

DIFF_HEADS = 8
DIFF_QK_W = DIFF_HEADS * 2 * HEAD_DIM
DIFF_V_DIM = 2 * HEAD_DIM
DIFF_BRANCH = DIFF_HEADS * DIFF_V_DIM

MLSTM_INNER = 2 * D_MODEL
MLSTM_HEADS = 4
MLSTM_HEAD_DIM = MLSTM_INNER // MLSTM_HEADS
MLSTM_QKV_BLOCK = 4
MLSTM_CONV_K = 4

LANES = 128
SUBLANES = 8
VMEM_LIMIT_BYTES = 56 * 1024 * 1024

TOKEN_TILE = 512
PROJ_OUT_TILE = 1024
PROJ_N_TILE = 256
ATTN_Q_TILE = 256
ATTN_SHORT_SEQS = 2
MLSTM_CHUNK = 256
HEADWISE_CHUNK = LANES
GATE_W = 4 * MLSTM_HEADS
Q_SCALE = HEAD_DIM ** -0.5 * math.log2(math.e)

F32 = jnp.float32
BF16 = jnp.bfloat16


def _params(*semantics):
    return pltpu.CompilerParams(dimension_semantics=semantics, vmem_limit_bytes=VMEM_LIMIT_BYTES)


def _silu(x):
    return x * (1.0 / (1.0 + jnp.exp(-x)))


def _dot(a, b):
    return jnp.dot(a, b, preferred_element_type=F32)


def _dot_nt(a, b):
    return lax.dot_general(a, b, (((1,), (1,)), ((), ())), preferred_element_type=F32)


class TokenLayout:
    def __init__(self, n_prompt, seq, tile):
        self.prompt_tiles = n_prompt // tile
        self.tiles_per_seq = seq // tile

    def is_prompt(self, m):
        return m < self.prompt_tiles

    def cond_row(self, m):
        return jnp.where(m < self.prompt_tiles, 0, 1 + (m - self.prompt_tiles) // self.tiles_per_seq)

    def prompt_block(self, m):
        return jnp.minimum(m, self.prompt_tiles - 1)

    def sample_block(self, m):
        return jnp.maximum(m - self.prompt_tiles, 0)


def _mod_kernel(cond_ref, w_ref, b_ref, o_ref):
    a = _silu(cond_ref[...]).astype(BF16)
    o_ref[...] = _dot(a, w_ref[...].astype(BF16)) + b_ref[...]


def adaln_all(cond, w_mod, b_mod):
    tn = 1024
    return pl.pallas_call(
        _mod_kernel,
        grid=(DEPTH, 3 * D_MODEL // tn),
        in_specs=[
            pl.BlockSpec((SUBLANES, D_MODEL), lambda i, n: (0, 0)),
            pl.BlockSpec((None, D_MODEL, tn), lambda i, n: (i, 0, n)),
            pl.BlockSpec((None, 1, tn), lambda i, n: (i, 0, n)),
        ],
        out_specs=pl.BlockSpec((None, SUBLANES, tn), lambda i, n: (i, 0, n)),
        out_shape=jax.ShapeDtypeStruct((DEPTH, SUBLANES, 3 * D_MODEL), F32),
        compiler_params=_params("parallel", "parallel"),
        name="adaln_mod",
    )(cond, w_mod, b_mod.reshape(DEPTH, 1, 3 * D_MODEL))


def _swap_half_heads(y):
    w = y.shape[-1]
    lane = lax.broadcasted_iota(jnp.int32, y.shape, 1)
    upper = (lane & (HEAD_DIM // 2)) != 0
    return jnp.where(upper, pltpu.roll(y, HEAD_DIM // 2, 1), pltpu.roll(y, w - HEAD_DIM // 2, 1))


def _x_specs(x, lay, tm):
    if isinstance(x, tuple):
        return [pl.BlockSpec((tm, D_MODEL), lambda m: (lay.prompt_block(m), 0)),
                pl.BlockSpec((tm, D_MODEL), lambda m: (lay.sample_block(m), 0))], list(x)
    return [pl.BlockSpec((tm, D_MODEL), lambda m: (m, 0))], [x]


def _resident_spec(stacked, layer):
    return pl.BlockSpec((None,) + stacked.shape[1:], lambda m: (layer, 0, 0), pipeline_mode=pl.Buffered(1))


def _two_paths(is_prompt, path):
    pl.when(is_prompt)(functools.partial(path, True))
    pl.when(jnp.logical_not(is_prompt))(functools.partial(path, False))


def _proj_in_kernel(*refs, segs, layout, has_norm, has_rope, x_pair):
    it = iter(refs)
    x_refs = (next(it), next(it)) if x_pair else (next(it),) * 2
    g_ref, mod_ref, w_ref = next(it), next(it), next(it)
    if has_rope:
        cos_ref, sin_ref = next(it), next(it)
    if has_norm:
        gmat_ref, qg_ref, kg_ref = next(it), next(it), next(it)
    prev_refs = [next(it) if kind != "full" and kind[2] else None for _, _, kind in segs]
    out_refs = []
    for _, _, kind in segs:
        if kind == "full":
            ref = next(it)
            out_refs.append((ref, ref, None))
        else:
            has_prompt, has_t, _ = kind
            out_refs.append((next(it) if has_prompt else None, next(it), next(it) if has_t else None))
    cw = PROJ_N_TILE

    def store_transposed(ref, acc, col0):
        seq_len = ref.shape[3]
        for s in range(ref.shape[0]):
            ref[s, ref.shape[1] - 1, col0:col0 + cw, :] = acc[s * seq_len:(s + 1) * seq_len, :].T.astype(ref.dtype)

    def path(prompt):
        x = x_refs[0 if prompt else 1][...]
        y = x * lax.rsqrt(jnp.mean(x * x, axis=-1, keepdims=True) + NORM_EPS) * g_ref[...]
        shift = mod_ref[:, 0:D_MODEL]
        scale = mod_ref[:, D_MODEL:2 * D_MODEL]
        h = (y * (1.0 + scale) + shift).astype(BF16)
        n_chunks = sum(s[0] for s in segs) // cw
        accs = [_dot(h, w_ref[:, i * cw:(i + 1) * cw].astype(BF16)) for i in range(n_chunks)]
        col = 0
        for (width, mode, _), (p_ref, s_ref, t_ref), prev_ref in zip(segs, out_refs, prev_refs):
            o_ref = p_ref if prompt else s_ref
            if prompt and prev_ref is not None:
                t_ref[:, 0:prev_ref.shape[1]] = prev_ref[...]
            for c in range(width // cw):
                acc = accs[col // cw]
                col += cw
                if mode in ("q_norm", "k_norm"):
                    ss = _dot((acc * acc).astype(BF16), gmat_ref[...])
                    gvec = qg_ref[...] if mode == "q_norm" else kg_ref[...]
                    acc = acc * lax.rsqrt(ss * (1.0 / HEAD_DIM) + NORM_EPS) * gvec
                if has_rope and mode != "plain" and not prompt:
                    acc = acc * cos_ref[...] + _swap_half_heads(acc) * sin_ref[...]
                if mode in ("q_norm", "q"):
                    acc = acc * Q_SCALE
                if o_ref is not None:
                    o_ref[:, c * cw:(c + 1) * cw] = acc.astype(o_ref.dtype)
                if prompt and t_ref is not None:
                    store_transposed(t_ref, acc, c * cw)

    _two_paths(layout.is_prompt(pl.program_id(0)), path)


def proj_in(x, norm_g, mod4, layer, w, w_layer, segs, n_prompt, sample_seq, rope_tabs=None, head_norm=None,
            tile=TOKEN_TILE):
    tm, cw = tile, PROJ_N_TILE
    lay = TokenLayout(n_prompt, sample_seq, tm)
    T = x[0].shape[0] + x[1].shape[0] if isinstance(x, tuple) else x.shape[0]
    N = w.shape[2]
    assert N == sum(s[0] for s in segs) and all(s[0] % cw == 0 for s in segs)

    in_specs, args = _x_specs(x, lay, tm)
    in_specs += [
        pl.BlockSpec((1, D_MODEL), lambda m: (0, 0)),
        pl.BlockSpec((None, None, 1, 3 * D_MODEL), lambda m: (layer, lay.cond_row(m), 0, 0)),
        _resident_spec(w, w_layer),
    ]
    args += [norm_g.reshape(1, D_MODEL), mod4, w]
    if rope_tabs is not None:
        def pos_block(m):
            return (lay.sample_block(m) % lay.tiles_per_seq, 0)
        in_specs += [pl.BlockSpec((tm, cw), pos_block), pl.BlockSpec((tm, cw), pos_block)]
        args += list(rope_tabs)
    if head_norm is not None:
        in_specs += [pl.BlockSpec((cw, cw), lambda m: (0, 0)),
                     pl.BlockSpec((1, cw), lambda m: (0, 0)),
                     pl.BlockSpec((1, cw), lambda m: (0, 0))]
        args += list(head_norm)

    out_specs, out_shape, kern_segs = [], [], []
    t_block = lambda m: (lay.prompt_block(m), 0, 0, 0)
    for width, mode, out in segs:
        if isinstance(out, tuple):
            prompt_dt, t_dt, seq_len, prev = out
            kern_segs.append((width, mode, (prompt_dt is not None, t_dt is not None, prev is not None)))
            if prev is not None:
                in_specs.append(pl.BlockSpec((tm // seq_len,) + prev.shape[1:], t_block))
                args.append(prev)
            if prompt_dt is not None:
                out_specs.append(pl.BlockSpec((tm, width), lambda m: (lay.prompt_block(m), 0)))
                out_shape.append(jax.ShapeDtypeStruct((n_prompt, width), prompt_dt))
            out_specs.append(pl.BlockSpec((tm, width), lambda m: (lay.sample_block(m), 0)))
            out_shape.append(jax.ShapeDtypeStruct((T - n_prompt, width), BF16))
            if t_dt is not None:
                layers = 1 if prev is None else prev.shape[1] + 1
                out_specs.append(pl.BlockSpec((tm // seq_len, layers, width, seq_len), t_block))
                out_shape.append(jax.ShapeDtypeStruct((n_prompt // seq_len, layers, width, seq_len), t_dt))
        else:
            kern_segs.append((width, mode, "full"))
            out_specs.append(pl.BlockSpec((tm, width), lambda m: (m, 0)))
            out_shape.append(jax.ShapeDtypeStruct((T, width), out))

    kern = functools.partial(_proj_in_kernel, segs=tuple(kern_segs), layout=lay, x_pair=isinstance(x, tuple),
                             has_norm=head_norm is not None, has_rope=rope_tabs is not None)
    return pl.pallas_call(
        kern,
        grid=(T // tm,),
        in_specs=in_specs,
        out_specs=out_specs,
        out_shape=out_shape,
        compiler_params=_params("arbitrary"),
        name="proj_in",
    )(*args)


def _proj_out_kernel(*refs, layout, with_skip, x_pair, final):
    it = iter(refs)
    o_refs = (next(it), next(it))
    if with_skip:
        xc_ref, skip_ref = next(it), next(it)
    z_ref = next(it)
    x_refs = (next(it), next(it)) if x_pair else (next(it),) * 2
    mod_ref, w_ref = next(it), next(it)
    if final:
        fg_ref = next(it)
    out_refs = (next(it), next(it)) if final else (next(it),) * 2

    def path(prompt):
        sel = 0 if prompt else 1
        a = o_refs[sel][...].astype(F32)
        if with_skip:
            a = a + skip_ref[...] * xc_ref[...].astype(F32)
        a = (a * _silu(z_ref[...].astype(F32))).astype(BF16)
        gate = mod_ref[:, 2 * D_MODEL:3 * D_MODEL]
        x = x_refs[sel][...] + gate * _dot(a, w_ref[...].astype(BF16))
        if final:
            x = x * lax.rsqrt(jnp.mean(x * x, axis=-1, keepdims=True) + NORM_EPS) * fg_ref[...]
        out_refs[sel][...] = x

    _two_paths(layout.is_prompt(pl.program_id(0)), path)


def proj_out(o_prompt, o_sample, z, x, mod4, layer, w_out, w_layer, n_prompt, sample_seq, xc=None, skip=None,
             final_g=None, tile=TOKEN_TILE):
    T = z.shape[0]
    K = o_prompt.shape[1]
    tm = tile
    lay = TokenLayout(n_prompt, sample_seq, tm)
    row = lambda m: (m, 0)
    prompt_row = lambda m: (lay.prompt_block(m), 0)
    sample_row = lambda m: (lay.sample_block(m), 0)
    in_specs = [pl.BlockSpec((tm, K), prompt_row), pl.BlockSpec((tm, K), sample_row)]
    args = [o_prompt, o_sample]
    if xc is not None:
        in_specs += [pl.BlockSpec((tm, K), row), pl.BlockSpec((1, K), lambda m: (0, 0))]
        args += [xc, skip.reshape(1, K)]
    in_specs.append(pl.BlockSpec((tm, K), row))
    args.append(z)
    x_specs, x_args = _x_specs(x, lay, tm)
    in_specs += x_specs + [
        pl.BlockSpec((None, None, 1, 3 * D_MODEL), lambda m: (layer, lay.cond_row(m), 0, 0)),
        _resident_spec(w_out, w_layer),
    ]
    args += x_args + [mod4, w_out]
    if final_g is not None:
        in_specs.append(pl.BlockSpec((1, D_MODEL), lambda m: (0, 0)))
        args.append(final_g.reshape(1, D_MODEL))
        out_specs = [pl.BlockSpec((tm, D_MODEL), prompt_row), pl.BlockSpec((tm, D_MODEL), sample_row)]
        out_shape = [jax.ShapeDtypeStruct((n_prompt, D_MODEL), F32),
                     jax.ShapeDtypeStruct((T - n_prompt, D_MODEL), F32)]
    else:
        out_specs = pl.BlockSpec((tm, D_MODEL), row)
        out_shape = jax.ShapeDtypeStruct((T, D_MODEL), F32)
    kern = functools.partial(_proj_out_kernel, layout=lay, with_skip=xc is not None,
                             x_pair=isinstance(x, tuple), final=final_g is not None)
    return pl.pallas_call(
        kern,
        grid=(T // tm,),
        in_specs=in_specs,
        out_specs=out_specs,
        out_shape=out_shape,
        compiler_params=_params("arbitrary"),
        name="proj_out",
    )(*args)


ONES_ROWS = 16


def _low_half(shape):
    return lax.broadcasted_iota(jnp.int32, shape, 1) < HEAD_DIM


def _attend_group(items, rows, values_transposed=False):
    scores = [[_dot_nt(q, k) for k in k_parts] for q, k_parts, _, _ in items]
    probs = []
    for s in scores:
        m = functools.reduce(jnp.maximum, [jnp.max(x, axis=-1, keepdims=True) for x in s])
        probs.append([jnp.exp2(x - m).astype(BF16) for x in s])
    out = []
    for p_parts, (_, _, v_ref, head) in zip(probs, items):
        acc = None
        for p, r in zip(p_parts, rows):
            t = _dot_nt(v_ref[head, :, r], p) if values_transposed else _dot(p, v_ref[head, r, :])
            acc = t if acc is None else acc + t
        out.append(acc)
    return out


def _group_size(n_keys):
    return 16 if n_keys <= 512 else 4


def _row_parts(k_ref, kc_ref):
    n_new = k_ref.shape[0]
    rows = [slice(0, n_new)]
    if kc_ref is not None:
        rows.append(slice(n_new, n_new + kc_ref.shape[0]))
    return rows


def _per_sequence(body, seqs, row_refs, shared, scratch):
    for s in range(seqs):
        def view(ref):
            n = ref.shape[0] // seqs
            return ref if seqs == 1 else ref.at[pl.ds(s * n, n)]
        body(*[view(r) for r in row_refs], *shared, *[r.at[s] for r in scratch])


def _gqa_attn_kernel(*refs, has_ctx, seqs):
    if has_ctx:
        q_ref, k_ref, v_ref, kc_ref, vc_ref, o_ref, *tables = refs
    else:
        q_ref, k_ref, v_ref, o_ref, *tables = refs
        kc_ref = vc_ref = None
    _per_sequence(_gqa_attn_sequence, seqs, (q_ref, k_ref, v_ref, o_ref), (kc_ref, vc_ref), tables)


def _gqa_attn_sequence(q_ref, k_ref, v_ref, o_ref, kc_ref, vc_ref, *tables):
    rows = _row_parts(k_ref, kc_ref)
    transposed = len(tables) == 1

    @pl.when(pl.program_id(1) == 0)
    def _():
        for src, r in zip([v_ref, vc_ref][:len(rows)], rows):
            if transposed:
                vt_ref, = tables
                first = lax.broadcasted_iota(jnp.int32, (ONES_ROWS, r.stop - r.start), 0) == 0
                for slab in range(GQA_KV_W // LANES):
                    t = src[:, slab * LANES:(slab + 1) * LANES].astype(F32).T
                    for half in range(LANES // HEAD_DIM):
                        c = slab * (LANES // HEAD_DIM) + half
                        vt_ref[c, 0:HEAD_DIM, r] = t[half * HEAD_DIM:(half + 1) * HEAD_DIM, :].astype(BF16)
                        vt_ref[c, HEAD_DIM:HEAD_DIM + ONES_ROWS, r] = jnp.where(first, 1.0, 0.0).astype(BF16)
            else:
                ve_ref, vo_ref = tables
                for c in range(GQA_KV_HEADS):
                    slab = src[:, (c // 2) * LANES:(c // 2 + 1) * LANES].astype(F32)
                    low = _low_half(slab.shape)
                    if c % 2 == 0:
                        ve = jnp.where(low, slab, 0.0)
                        vo = pltpu.roll(ve, HEAD_DIM, 1)
                    else:
                        vo = jnp.where(low, 0.0, slab)
                        ve = pltpu.roll(vo, HEAD_DIM, 1)
                    ones_e = jnp.where(low, 1.0, 0.0)
                    ve_ref[c, r, 0:LANES] = ve.astype(BF16)
                    ve_ref[c, r, LANES:2 * LANES] = ones_e.astype(BF16)
                    vo_ref[c, r, 0:LANES] = vo.astype(BF16)
                    vo_ref[c, r, LANES:2 * LANES] = (1.0 - ones_e).astype(BF16)

    k_srcs = [k_ref, kc_ref][:len(rows)]
    group = _group_size(rows[-1].stop)
    n_pairs = GQA_HEADS // 2
    for p0 in range(0, n_pairs, group // 2):
        items = []
        for pair in range(p0, p0 + group // 2):
            c = pair // (GQA_GROUP // 2)
            q = q_ref[:, pair * LANES:(pair + 1) * LANES].astype(F32)
            low = _low_half(q.shape)
            q_sw = pltpu.roll(q, HEAD_DIM, 1)
            if c % 2 == 0:
                qe, qo = jnp.where(low, q, 0.0), jnp.where(low, q_sw, 0.0)
            else:
                qe, qo = jnp.where(low, 0.0, q_sw), jnp.where(low, 0.0, q)
            k_parts = [s[:, (c // 2) * LANES:(c // 2 + 1) * LANES].astype(BF16) for s in k_srcs]
            items += [(qe.astype(BF16), k_parts, tables[0], c), (qo.astype(BF16), k_parts, tables[-1], c)]
        accs = _attend_group(items, rows, values_transposed=transposed)
        for i, pair in enumerate(range(p0, p0 + group // 2)):
            if transposed:
                o_t = [a[0:HEAD_DIM] / a[HEAD_DIM:HEAD_DIM + 1] for a in accs[2 * i:2 * i + 2]]
                o = jnp.concatenate(o_t, axis=0).T
            else:
                acc = accs[2 * i] + accs[2 * i + 1]
                o = acc[:, 0:LANES] / acc[:, LANES:2 * LANES]
            o_ref[:, pair * LANES:(pair + 1) * LANES] = o.astype(o_ref.dtype)


def _diff_attn_kernel(*refs, has_ctx, seqs, lam_init):
    if has_ctx:
        q_ref, k_ref, v_ref, kc_ref, vc_ref, lam_ref, g_ref, o_ref, vx_ref = refs
    else:
        q_ref, k_ref, v_ref, lam_ref, g_ref, o_ref, vx_ref = refs
        kc_ref = vc_ref = None
    _per_sequence(functools.partial(_diff_attn_sequence, lam_init=lam_init), seqs, (q_ref, k_ref, v_ref, o_ref),
                  (kc_ref, vc_ref, lam_ref, g_ref), (vx_ref,))


def _diff_attn_sequence(q_ref, k_ref, v_ref, o_ref, kc_ref, vc_ref, lam_ref, g_ref, vx_ref, *, lam_init):
    rows = _row_parts(k_ref, kc_ref)

    @pl.when(pl.program_id(1) == 0)
    def _():
        for src, r in zip([v_ref, vc_ref][:len(rows)], rows):
            for h in range(DIFF_HEADS):
                vx_ref[h, r, 0:LANES] = src[:, h * LANES:(h + 1) * LANES].astype(BF16)
                vx_ref[h, r, LANES:2 * LANES] = jnp.ones((r.stop - r.start, LANES), BF16)

    lv = lam_ref[...]
    s1 = jnp.sum(lv[0:1] * lv[1:2], axis=-1, keepdims=True)
    s2 = jnp.sum(lv[2:3] * lv[3:4], axis=-1, keepdims=True)
    lam = jnp.exp(s1) - jnp.exp(s2) + lam_init
    k_srcs = [k_ref, kc_ref][:len(rows)]
    group = _group_size(rows[-1].stop)
    for h0 in range(0, DIFF_HEADS, group // 2):
        items = []
        for h in range(h0, h0 + group // 2):
            q = q_ref[:, h * LANES:(h + 1) * LANES].astype(F32)
            low = _low_half(q.shape)
            k_parts = [s[:, h * LANES:(h + 1) * LANES].astype(BF16) for s in k_srcs]
            items += [(jnp.where(low, q, 0.0).astype(BF16), k_parts, vx_ref, h),
                      (jnp.where(low, 0.0, q).astype(BF16), k_parts, vx_ref, h)]
        accs = _attend_group(items, rows)
        for i, h in enumerate(range(h0, h0 + group // 2)):
            a0, a1 = accs[2 * i], accs[2 * i + 1]
            o = a0[:, 0:LANES] / a0[:, LANES:2 * LANES] - lam * (a1[:, 0:LANES] / a1[:, LANES:2 * LANES])
            o = o * lax.rsqrt(jnp.mean(o * o, axis=-1, keepdims=True) + NORM_EPS) * g_ref[...]
            o_ref[:, h * LANES:(h + 1) * LANES] = (o * (1.0 - lam_init)).astype(o_ref.dtype)


def attention(kernel, v_scratch, q, k, v, batch, sq, q_row0, ctx=None, extra=()):
    tq = ATTN_Q_TILE
    nq = sq // tq
    seqs = ATTN_SHORT_SEQS if nq == 1 and ctx is None and batch % ATTN_SHORT_SEQS == 0 else 1
    batch, sq, tq = batch // seqs, sq * seqs, tq * seqs
    off = q_row0 // tq
    in_specs = [
        pl.BlockSpec((tq, q.shape[1]), lambda b, i: (b * nq + i + off, 0)),
        pl.BlockSpec((sq, k.shape[1]), lambda b, i: (b, 0)),
        pl.BlockSpec((sq, v.shape[1]), lambda b, i: (b, 0)),
    ]
    args = [q, k, v]
    sk = sq
    if ctx is not None:
        kc, vc, layer = ctx
        sk += kc.shape[2]
        in_specs += [pl.BlockSpec((None, None) + kc.shape[2:], lambda b, i: (b, layer, 0, 0)),
                     pl.BlockSpec((None, None) + vc.shape[2:], lambda b, i: (b, layer, 0, 0))]
        args += [kc, vc]
    for a in extra:
        in_specs.append(pl.BlockSpec(a.shape, lambda b, i: (0, 0)))
    args += list(extra)
    return pl.pallas_call(
        functools.partial(kernel, has_ctx=ctx is not None, seqs=seqs),
        grid=(batch, nq),
        in_specs=in_specs,
        out_specs=pl.BlockSpec((tq, D_MODEL), lambda b, i: (b * nq + i, 0)),
        out_shape=jax.ShapeDtypeStruct((batch * sq, D_MODEL), BF16),
        scratch_shapes=[pltpu.VMEM((seqs,) + tuple(sk // seqs if d == "keys" else d for d in shape), BF16)
                        for shape in v_scratch],
        compiler_params=_params("parallel", "arbitrary"),
        name="attention",
    )(*args)


def _log_sigmoid(x):
    return jnp.minimum(x, 0.0) - jnp.log1p(jnp.exp(-jnp.abs(x)))


def _mlstm_in_kernel(cur_ref, prev_ref, next_ref, ng_ref, mod_ref, w_ref, cw_ref, cb_ref, bq_ref, bk_ref, bkt_ref,
                     bv_ref, wg_ref, bg_ref, xc_ref, q_ref, kt_ref, v_ref, gcol_ref, grow_ref, z_ref,
                     *, n_prompt_tiles, tiles_per_seq):
    L = MLSTM_CHUNK
    cw = PROJ_N_TILE
    t = pl.program_id(0)
    st = jnp.maximum(t - n_prompt_tiles, 0) % tiles_per_seq
    first = (t < n_prompt_tiles) | (st == 0)
    last = (t < n_prompt_tiles) | (st == tiles_per_seq - 1)
    n_ext = L + 2 * SUBLANES
    x = jnp.concatenate([prev_ref[...], cur_ref[...], next_ref[...]], axis=0)
    y = x * lax.rsqrt(jnp.mean(x * x, axis=-1, keepdims=True) + NORM_EPS) * ng_ref[...]
    h = (y * (1.0 + mod_ref[:, D_MODEL:2 * D_MODEL]) + mod_ref[:, 0:D_MODEL]).astype(BF16)
    n_xm = MLSTM_INNER // cw
    xm_chunks = [_dot(h, w_ref[:, i * cw:(i + 1) * cw].astype(BF16)) for i in range(n_xm)]
    z_chunks = [_dot(h[SUBLANES:SUBLANES + L], w_ref[:, (n_xm + i) * cw:(n_xm + i + 1) * cw].astype(BF16))
                for i in range(n_xm)]
    for i, zc in enumerate(z_chunks):
        z_ref[:, i * cw:(i + 1) * cw] = zc.astype(z_ref.dtype)
    row_e = lax.broadcasted_iota(jnp.int32, (n_ext, 1), 0)
    inside = ((row_e >= SUBLANES) | jnp.logical_not(first)) & ((row_e < SUBLANES + L) | jnp.logical_not(last))
    ext = jnp.where(inside, jnp.concatenate(xm_chunks, axis=1), 0.0)
    cur = ext[SUBLANES:SUBLANES + L]
    pad_l = MLSTM_CONV_K // 2
    acc = cb_ref[...] + cw_ref[pad_l:pad_l + 1, :] * cur
    for j in range(MLSTM_CONV_K):
        if j == pad_l:
            continue
        sh = (pad_l - j) % n_ext
        acc = acc + cw_ref[j:j + 1, :] * pltpu.roll(ext, sh, 0)[SUBLANES:SUBLANES + L]
    xc = _silu(acc)
    xc_ref[...] = xc.astype(xc_ref.dtype)

    g = jnp.zeros((L, LANES), F32)
    for c in range(MLSTM_INNER // HEADWISE_CHUNK):
        sl = slice(c * HEADWISE_CHUNK, (c + 1) * HEADWISE_CHUNK)
        xcb = xc[:, sl].astype(BF16)
        xmb = cur[:, sl].astype(BF16)
        qc = _dot(xcb, bq_ref[c]).astype(BF16)
        kc = _dot(xcb, bk_ref[c]).astype(BF16)
        vc = _dot(xmb, bv_ref[c]).astype(BF16)
        q_ref[:, sl] = qc
        v_ref[:, sl] = vc
        kt_ref[sl, :] = _dot_nt(bkt_ref[c], xcb).astype(BF16)
        g = g + _dot(qc, wg_ref[0, sl, :]) + _dot(kc, wg_ref[1, sl, :]) + _dot(vc, wg_ref[2, sl, :])
    g = g + bg_ref[...]

    ls = _log_sigmoid(g)
    row = lax.broadcasted_iota(jnp.int32, (L, LANES), 0)
    pre, suf = ls, ls
    s = 1
    while s < L:
        pre = pre + jnp.where(row >= s, pltpu.roll(pre, s, 0), 0.0)
        suf = suf + jnp.where(row < L - s, pltpu.roll(suf, L - s, 0), 0.0)
        s *= 2
    lane = lax.broadcasted_iota(jnp.int32, (L, LANES), 1)
    H = MLSTM_HEADS
    out = jnp.where((lane >= H) & (lane < 2 * H), pre, jnp.where((lane >= 3 * H) & (lane < 4 * H), suf, g))
    gcol_ref[...] = out[:, 0:GATE_W]
    grow_ref[...] = out.T[0:GATE_W, :]


def mlstm_in(x, norm_g, mod4, layer, w, w_layer, conv_w, conv_b, bq, bk, bkt, bv, wg, bg, n_prompt, sample_seq):
    T = x.shape[0]
    L = MLSTM_CHUNK
    n_tiles = T // L
    r8 = L // SUBLANES
    n_blk8 = T // SUBLANES
    C = MLSTM_INNER
    lay = TokenLayout(n_prompt, sample_seq, L)
    kern = functools.partial(_mlstm_in_kernel, n_prompt_tiles=n_prompt // L, tiles_per_seq=sample_seq // L)
    const3 = lambda t: (0, 0, 0)
    return pl.pallas_call(
        kern,
        grid=(n_tiles,),
        in_specs=[
            pl.BlockSpec((L, D_MODEL), lambda t: (t, 0)),
            pl.BlockSpec((SUBLANES, D_MODEL), lambda t: (jnp.maximum(t * r8 - 1, 0), 0)),
            pl.BlockSpec((SUBLANES, D_MODEL), lambda t: (jnp.minimum((t + 1) * r8, n_blk8 - 1), 0)),
            pl.BlockSpec((1, D_MODEL), lambda t: (0, 0)),
            pl.BlockSpec((None, None, 1, 3 * D_MODEL), lambda t: (layer, lay.cond_row(t), 0, 0)),
            _resident_spec(w, w_layer),
            pl.BlockSpec((MLSTM_CONV_K, C), lambda t: (0, 0)),
            pl.BlockSpec((1, C), lambda t: (0, 0)),
            pl.BlockSpec(bq.shape, const3),
            pl.BlockSpec(bk.shape, const3),
            pl.BlockSpec(bkt.shape, const3),
            pl.BlockSpec(bv.shape, const3),
            pl.BlockSpec(wg.shape, const3),
            pl.BlockSpec((1, LANES), lambda t: (0, 0)),
        ],
        out_specs=[
            pl.BlockSpec((L, C), lambda t: (t, 0)),
            pl.BlockSpec((L, C), lambda t: (t, 0)),
            pl.BlockSpec((None, C, L), lambda t: (t, 0, 0)),
            pl.BlockSpec((L, C), lambda t: (t, 0)),
            pl.BlockSpec((L, GATE_W), lambda t: (t, 0)),
            pl.BlockSpec((None, GATE_W, L), lambda t: (t, 0, 0)),
            pl.BlockSpec((L, C), lambda t: (t, 0)),
        ],
        out_shape=[
            jax.ShapeDtypeStruct((T, C), BF16),
            jax.ShapeDtypeStruct((T, C), BF16),
            jax.ShapeDtypeStruct((n_tiles, C, L), BF16),
            jax.ShapeDtypeStruct((T, C), BF16),
            jax.ShapeDtypeStruct((T, GATE_W), F32),
            jax.ShapeDtypeStruct((n_tiles, GATE_W, L), F32),
            jax.ShapeDtypeStruct((T, C), BF16),
        ],
        compiler_params=_params("parallel"),
        name="mlstm_in",
    )(x, x, x, norm_g.reshape(1, D_MODEL), mod4, w, conv_w, conv_b.reshape(1, C), bq, bk, bkt, bv, wg, bg)


def _gate_cols(gcol, h):
    lane = lax.broadcasted_iota(jnp.int32, gcol.shape, 1)
    pick = lambda idx: jnp.sum(jnp.where(lane == idx, gcol, 0.0), axis=-1, keepdims=True)
    return pick(MLSTM_HEADS + h), pick(3 * MLSTM_HEADS + h)


def _chunk_weights(qk, b_col, b_row, i_row, tot, causal_mask, m_prev):
    ar = i_row - b_row
    d = jnp.where(causal_mask, b_col + ar, -jnp.inf)
    inter = b_col + m_prev
    mt = jnp.maximum(jnp.max(d, axis=-1, keepdims=True), inter)
    s = qk * jnp.exp(d - mt)
    g = tot + ar
    m_new = jnp.maximum(tot + m_prev, jnp.max(g, axis=-1, keepdims=True))
    w = jnp.exp(g - m_new)
    w_hi = w.astype(BF16).astype(F32)
    rid = lax.broadcasted_iota(jnp.int32, (SUBLANES, w.shape[-1]), 0)
    w8 = jnp.where(rid == 0, w_hi, jnp.where(rid == 1, w - w_hi, 0.0)).astype(BF16)
    return dict(s=s.astype(BF16), den=jnp.sum(s, axis=-1, keepdims=True), mt=mt, inter=inter, w=w, w8=w8,
                m_new=m_new, decay_arg=tot + m_prev - m_new)


def _chunk_matmuls(wts, q, kt, v, C_prev):
    num = _dot(wts["s"], v)
    qC = None if C_prev is None else _dot(q, C_prev.astype(BF16))
    C_add = _dot((kt.astype(F32) * wts["w"]).astype(BF16), v)
    n8 = _dot_nt(wts["w8"], kt)
    return num, qC, C_add, n8[0:1] + n8[1:2]


def _chunk_finish(wts, mats, q, C_prev, n_prev):
    num, qC, C_new, n_new = mats
    den, mt = wts["den"], wts["mt"]
    if C_prev is not None:
        w_inter = jnp.exp(wts["inter"] - mt)
        num = num + w_inter * qC
        den = den + w_inter * jnp.sum(q.astype(F32) * n_prev, axis=-1, keepdims=True)
        decay = jnp.exp(wts["decay_arg"])
        C_new = decay * C_prev + C_new
        n_new = decay * n_prev + n_new
    h = num / jnp.maximum(jnp.abs(den), jnp.exp(-mt))
    return h, C_new, n_new, wts["m_new"]


def _chunk_dir(qk, q, kt, v, b_col, b_row, i_row, tot, causal_mask, m_prev, C_prev, n_prev):
    wts = _chunk_weights(qk, b_col, b_row, i_row, tot, causal_mask, m_prev)
    return _chunk_finish(wts, _chunk_matmuls(wts, q, kt, v, C_prev), q, C_prev, n_prev)


def _head_layernorm(h, g):
    mu = jnp.mean(h, axis=-1, keepdims=True)
    hc = h - mu
    var = jnp.mean(hc * hc, axis=-1, keepdims=True)
    return hc * lax.rsqrt(var + NORM_EPS) * g


def _chunk_masks(L):
    t_idx = lax.broadcasted_iota(jnp.int32, (L, L), 0)
    s_idx = lax.broadcasted_iota(jnp.int32, (L, L), 1)
    return s_idx <= t_idx, s_idx >= t_idx


def _mlstm_prompt_kernel(q_ref, kt_ref, v_ref, gcol_ref, grow_ref, mhg_ref,
                         hn_ref, C_ref, n_ref, m_ref):
    L = MLSTM_CHUNK
    H, DH = MLSTM_HEADS, MLSTM_HEAD_DIM
    fmask, bmask = _chunk_masks(L)
    zero = jnp.zeros((1, 1), F32)
    lane = lax.broadcasted_iota(jnp.int32, (1, LANES), 1)
    col = lambda idx: gcol_ref[:, idx:idx + 1]
    row = lambda idx: grow_ref[idx:idx + 1, :]
    for h in range(H):
        ch = slice(h * DH, (h + 1) * DH)
        q, kt, v = q_ref[:, ch], kt_ref[ch, :], v_ref[:, ch]
        qk = _dot(q, kt)
        bf_row, bb_row = row(H + h), row(3 * H + h)
        hf, Cf, nf, mf = _chunk_dir(qk, q, kt, v, col(H + h), bf_row, row(h), bf_row[:, L - 1:L], fmask,
                                    zero, None, None)
        hb, Cb, nb, mb = _chunk_dir(qk, q, kt, v, col(3 * H + h), bb_row, row(2 * H + h), bb_row[:, 0:1], bmask,
                                    zero, None, None)
        hn_ref[:, ch] = _head_layernorm(hf + hb, mhg_ref[:, ch]).astype(hn_ref.dtype)
        C_ref[0, h] = Cf
        C_ref[1, h] = Cb
        n_ref[0, h] = nf
        n_ref[1, h] = nb
        m_ref[h] = jnp.where(lane == 0, mf, jnp.where(lane == 1, mb, 0.0))


def mlstm_prompt(q, kt, v, gcol, grow, mh_g, batch):
    L, H, DH, C = MLSTM_CHUNK, MLSTM_HEADS, MLSTM_HEAD_DIM, MLSTM_INNER
    return pl.pallas_call(
        _mlstm_prompt_kernel,
        grid=(batch,),
        in_specs=[
            pl.BlockSpec((L, C), lambda b: (b, 0)),
            pl.BlockSpec((None, C, L), lambda b: (b, 0, 0)),
            pl.BlockSpec((L, C), lambda b: (b, 0)),
            pl.BlockSpec((L, GATE_W), lambda b: (b, 0)),
            pl.BlockSpec((None, GATE_W, L), lambda b: (b, 0, 0)),
            pl.BlockSpec((1, C), lambda b: (0, 0)),
        ],
        out_specs=[
            pl.BlockSpec((L, C), lambda b: (b, 0)),
            pl.BlockSpec((None, None, 2, H, DH, DH), lambda b: (b, 0, 0, 0, 0, 0)),
            pl.BlockSpec((None, 2, H, 1, DH), lambda b: (b, 0, 0, 0, 0)),
            pl.BlockSpec((None, H, 1, LANES), lambda b: (b, 0, 0, 0)),
        ],
        out_shape=[
            jax.ShapeDtypeStruct((batch * L, C), BF16),
            jax.ShapeDtypeStruct((batch, 1, 2, H, DH, DH), F32),
            jax.ShapeDtypeStruct((batch, 2, H, 1, DH), F32),
            jax.ShapeDtypeStruct((batch, H, 1, LANES), F32),
        ],
        compiler_params=_params("parallel"),
        name="mlstm_prompt",
    )(q, kt, v, gcol, grow, mh_g.reshape(1, C))


def _mlstm_sample_kernel(q_ref, kt_ref, v_ref, gcol_ref, grow_ref, mhg_ref, C0_ref, n0_ref, m0_ref,
                         hn_ref, hf_ref, hb_ref, C_ref, n_ref, m_ref, *, n_chunks):
    L = MLSTM_CHUNK
    H = MLSTM_HEADS
    h = pl.program_id(1)
    fmask, bmask = _chunk_masks(L)
    C_ref[...] = C0_ref[...]
    n_ref[...] = n0_ref[...]
    m_ref[...] = m0_ref[...]

    def one(c, direction):
        rows = pl.ds(pl.multiple_of(c * L, L), L)
        q, kt, v = q_ref[rows, :], kt_ref[c], v_ref[rows, :]
        qk = _dot(q, kt)
        gcol = gcol_ref[rows, :]
        b_col = _gate_cols(gcol, h)[direction]
        grow = lambda idx: grow_ref[c, pl.ds(idx, 1), :]
        b_row = grow((2 * direction + 1) * H + h)
        i_row = grow(2 * direction * H + h)
        tot = b_row[:, L - 1:L] if direction == 0 else b_row[:, 0:1]
        mask = fmask if direction == 0 else bmask
        m_prev = m_ref[direction][:, 0:1]
        hh, C_new, n_new, m_new = _chunk_dir(qk, q, kt, v, b_col, b_row, i_row, tot, mask, m_prev,
                                             C_ref[direction], n_ref[direction])
        C_ref[direction] = C_new
        n_ref[direction] = n_new
        m_ref[direction] = jnp.broadcast_to(m_new, (1, LANES))
        (hf_ref if direction == 0 else hb_ref)[rows, :] = hh

    def body(i, carry):
        one(i, 0)
        one(n_chunks - 1 - i, 1)
        return carry

    lax.fori_loop(0, n_chunks, body, 0)

    def norm(c, carry):
        rows = pl.ds(pl.multiple_of(c * L, L), L)
        hn_ref[rows, :] = _head_layernorm(hf_ref[rows, :] + hb_ref[rows, :], mhg_ref[...]).astype(hn_ref.dtype)
        return carry

    lax.fori_loop(0, n_chunks, norm, 0)


def mlstm_sample(q, kt, v, gcol, grow, mh_g, C0, n0, m0, n_prompt, batch, seq):
    L, H, DH = MLSTM_CHUNK, MLSTM_HEADS, MLSTM_HEAD_DIM
    nc = seq // L
    row_off = n_prompt // seq
    tile_off = n_prompt // L // nc
    return pl.pallas_call(
        functools.partial(_mlstm_sample_kernel, n_chunks=nc),
        grid=(batch, H),
        in_specs=[
            pl.BlockSpec((seq, DH), lambda b, h: (b + row_off, h)),
            pl.BlockSpec((nc, DH, L), lambda b, h: (b + tile_off, h, 0)),
            pl.BlockSpec((seq, DH), lambda b, h: (b + row_off, h)),
            pl.BlockSpec((seq, GATE_W), lambda b, h: (b + row_off, 0)),
            pl.BlockSpec((nc, GATE_W, L), lambda b, h: (b + tile_off, 0, 0)),
            pl.BlockSpec((1, DH), lambda b, h: (0, h)),
            pl.BlockSpec((None, 2, None, DH, DH), lambda b, h: (b, 0, h, 0, 0)),
            pl.BlockSpec((None, 2, None, 1, DH), lambda b, h: (b, 0, h, 0, 0)),
            pl.BlockSpec((None, None, 2, 1, LANES), lambda b, h: (b, h, 0, 0, 0)),
        ],
        out_specs=pl.BlockSpec((seq, DH), lambda b, h: (b, h)),
        out_shape=jax.ShapeDtypeStruct((batch * seq, MLSTM_INNER), BF16),
        scratch_shapes=[
            pltpu.VMEM((seq, DH), F32),
            pltpu.VMEM((seq, DH), F32),
            pltpu.VMEM((2, DH, DH), F32),
            pltpu.VMEM((2, 1, DH), F32),
            pltpu.VMEM((2, 1, LANES), F32),
        ],
        compiler_params=_params("parallel", "parallel"),
        name="mlstm_sample",
    )(q, kt, v, gcol, grow, mh_g.reshape(1, MLSTM_INNER), C0, n0, m0)


def _rope_tables(seq):
    t = np.arange(seq)
    row = (t // GRID_W).astype(np.float64)
    col = (t % GRID_W).astype(np.float64)
    n_freq = HEAD_DIM // 4
    inv = ROPE_THETA ** (-np.arange(n_freq, dtype=np.float64) / n_freq)
    ang = np.concatenate([row[:, None] * inv, col[:, None] * inv], axis=-1)
    cos, sin = np.cos(ang), np.sin(ang)
    reps = PROJ_N_TILE // HEAD_DIM
    return (jnp.asarray(np.tile(np.concatenate([cos, cos], axis=-1), (1, reps)), F32),
            jnp.asarray(np.tile(np.concatenate([-sin, sin], axis=-1), (1, reps)), F32))


def _block_diag(w):
    blk, ch = MLSTM_QKV_BLOCK, HEADWISE_CHUNK
    w4 = w.reshape(-1, ch // blk, blk, blk)
    rows = w4.transpose(0, 2, 1, 3).reshape(-1, 1, blk, ch)
    full = jnp.broadcast_to(rows, (rows.shape[0], ch // blk, blk, ch)).reshape(-1, ch, ch)
    block_of = jnp.arange(ch) // blk
    return jnp.where(block_of[:, None] == block_of[None, :], full, 0.0)


def _diff_lambda_init(layer_idx):
    return 0.8 - 0.6 * math.exp(-0.3 * layer_idx)


def kernel(x_prompt, x_sample, cache_gqa_k, cache_gqa_v, cache_diff_k, cache_diff_v, state_mlstm_C, state_mlstm_n, state_mlstm_m, c, c_ctx, norm_g, w_mod, b_mod, gqa_w_in, gqa_q_norm_g, gqa_k_norm_g, gqa_w_out, diff_w_in, diff_lambda_q1, diff_lambda_k1, diff_lambda_q2, diff_lambda_k2, diff_subln_g, diff_w_out, mlstm_w_in, mlstm_conv_w, mlstm_conv_b, mlstm_w_q, mlstm_w_k, mlstm_w_v, mlstm_w_gate_f, mlstm_b_gate_f, mlstm_w_gate_b, mlstm_b_gate_b, mlstm_mh_norm_g, mlstm_skip, mlstm_w_out, final_norm_g):
    Bp, Sp, D = x_prompt.shape
    Bs, Ss, _ = x_sample.shape
    Tp, Ts = Bp * Sp, Bs * Ss
    assert Sp == MLSTM_CHUNK and Ss % TOKEN_TILE == 0 and Tp % TOKEN_TILE == 0
    H, DH = MLSTM_HEADS, MLSTM_HEAD_DIM
    past = cache_gqa_k.shape[2]

    x = (x_prompt.reshape(Tp, D), x_sample.reshape(Ts, D))
    cond = jnp.zeros((SUBLANES, D), F32).at[0].set(c_ctx).at[1:1 + Bs].set(c)
    mod4 = adaln_all(cond, w_mod, b_mod).reshape(DEPTH, SUBLANES, 1, 3 * D)
    rope_tabs = _rope_tables(Ss)
    gi = np.arange(PROJ_N_TILE) // HEAD_DIM
    gmat = jnp.asarray(gi[:, None] == gi[None, :], BF16)
    tile_g = lambda g: jnp.tile(g, PROJ_N_TILE // HEAD_DIM).reshape(1, PROJ_N_TILE)
    ctx_gqa = (cache_gqa_k.reshape(Bs, -1, past, GQA_KV_W), cache_gqa_v.reshape(Bs, -1, past, GQA_KV_W))
    ctx_diff = (cache_diff_k.reshape(Bs, -1, past, DIFF_QK_W), cache_diff_v.reshape(Bs, -1, past, DIFF_BRANCH))

    gqa_k_t = gqa_v_t = diff_k_t = None
    diff_v_list, mC_list, mn_list, mm_list = [], [], [], []
    for i in range(DEPTH):
        kind, j = i % N_MIXERS, i // N_MIXERS
        common = dict(n_prompt=Tp, sample_seq=Ss)
        last = dict(final_g=final_norm_g) if i == DEPTH - 1 else {}
        if kind == 0:
            segs = ((GQA_Q_W, "q_norm", BF16), (GQA_KV_W, "k_norm", (BF16, F32, Sp, gqa_k_t)),
                    (GQA_KV_W, "plain", (BF16, F32, Sp, gqa_v_t)), (GQA_Q_W, "plain", BF16))
            q, kp, ks, gqa_k_t, vp, vs, gqa_v_t, z = proj_in(
                x, norm_g[i], mod4, i, gqa_w_in, j, segs, rope_tabs=rope_tabs,
                head_norm=(gmat, tile_g(gqa_q_norm_g[j]), tile_g(gqa_k_norm_g[j])), **common)
            kern = _gqa_attn_kernel
            o_p = attention(kern, [(GQA_KV_HEADS, HEAD_DIM + ONES_ROWS, "keys")], q, kp, vp, Bp, Sp, 0)
            o_s = attention(kern, [(GQA_KV_HEADS, "keys", 2 * LANES)] * 2, q, ks, vs, Bs, Ss, Tp, ctx=(*ctx_gqa, j))
            x = proj_out(o_p, o_s, z, x, mod4, i, gqa_w_out, j, tile=PROJ_OUT_TILE, **common, **last)
        elif kind == 1:
            lam_init = _diff_lambda_init(i)
            segs = ((DIFF_QK_W, "q", BF16), (DIFF_QK_W, "k", (BF16, F32, Sp, diff_k_t)),
                    (DIFF_BRANCH, "plain", (F32, None, Sp, None)), (DIFF_BRANCH, "plain", BF16))
            q, kp, ks, diff_k_t, vp, vs, z = proj_in(x, norm_g[i], mod4, i, diff_w_in, j, segs,
                                                    rope_tabs=rope_tabs, **common)
            lam_vecs = jnp.stack([diff_lambda_q1[j], diff_lambda_k1[j], diff_lambda_q2[j], diff_lambda_k2[j]])
            extra = (lam_vecs, diff_subln_g[j].reshape(1, DIFF_V_DIM))
            scr = [(DIFF_HEADS, "keys", 2 * LANES)]
            kern = functools.partial(_diff_attn_kernel, lam_init=lam_init)
            o_p = attention(kern, scr, q, kp, vp, Bp, Sp, 0, extra=extra)
            o_s = attention(kern, scr, q, ks, vs, Bs, Ss, Tp, ctx=(*ctx_diff, j), extra=extra)
            x = proj_out(o_p, o_s, z, x, mod4, i, diff_w_out, j, tile=PROJ_OUT_TILE, **common, **last)
            diff_v_list.append(vp.reshape(Bp, 1, Sp, DIFF_HEADS, DIFF_V_DIM))
        else:
            bq = _block_diag(mlstm_w_q[j]).astype(BF16)
            bk = _block_diag(mlstm_w_k[j])
            bkt = (bk * (DH ** -0.5)).transpose(0, 2, 1).astype(BF16)
            bv = _block_diag(mlstm_w_v[j]).astype(BF16)
            wg = jnp.concatenate([mlstm_w_gate_f[j][:, :H], mlstm_w_gate_f[j][:, H:],
                                  mlstm_w_gate_b[j][:, :H], mlstm_w_gate_b[j][:, H:]], axis=-1)
            wg = jnp.pad(wg, ((0, 0), (0, LANES - GATE_W))).astype(BF16).reshape(3, MLSTM_INNER, LANES)
            bg = jnp.pad(jnp.concatenate([mlstm_b_gate_f[j], mlstm_b_gate_b[j]]), (0, LANES - GATE_W)).reshape(1, LANES)
            assert not isinstance(x, tuple)
            xc, q, kt, v, gcol, grow, z = mlstm_in(x, norm_g[i], mod4, i, mlstm_w_in, j, mlstm_conv_w[j],
                                                   mlstm_conv_b[j], bq, bk.astype(BF16), bkt, bv, wg, bg, **common)
            hn_p, Cp, np_, mp_ = mlstm_prompt(q, kt, v, gcol, grow, mlstm_mh_norm_g[j], Bp)
            C0 = state_mlstm_C[:, j]
            n0 = state_mlstm_n[:, j].reshape(Bs, 2, H, 1, DH)
            m0 = jnp.broadcast_to(state_mlstm_m[:, j].transpose(0, 2, 1)[..., None, None], (Bs, H, 2, 1, LANES))
            hn_s = mlstm_sample(q, kt, v, gcol, grow, mlstm_mh_norm_g[j], C0, n0, m0, Tp, Bs, Ss)
            x = proj_out(hn_p, hn_s, z, x, mod4, i, mlstm_w_out, j, xc=xc, skip=mlstm_skip[j],
                         **common, **last)
            mC_list.append(Cp)
            mn_list.append(np_.reshape(Bp, 1, 2, H, DH))
            mm_list.append(mp_[:, :, 0, 0:2].transpose(0, 2, 1)[:, None])

    cat1 = lambda parts: parts[0] if len(parts) == 1 else jnp.concatenate(parts, axis=1)
    y_prompt, y_sample = x

    def from_transposed(t, trail):
        t = t.reshape(t.shape[:2] + trail + (Sp,))
        return jnp.moveaxis(t, -1, 2)

    return (y_prompt.reshape(Bp, Sp, D), y_sample.reshape(Bs, Ss, D),
            from_transposed(gqa_k_t, (GQA_KV_HEADS, HEAD_DIM)), from_transposed(gqa_v_t, (GQA_KV_HEADS, HEAD_DIM)),
            from_transposed(diff_k_t, (DIFF_HEADS, 2, HEAD_DIM)), cat1(diff_v_list),
            cat1(mC_list), cat1(mn_list), cat1(mm_list))
```

```python
import functools
import math

import jax
import jax.numpy as jnp
import numpy as np
from jax import lax
from jax.experimental import pallas as pl
from jax.experimental.pallas import tpu as pltpu

D_MODEL = 1024
DEPTH = 4
GRID_W = 64
N_MIXERS = 3
HEAD_DIM = 64
ROPE_THETA = 10000.0
NORM_EPS = 1e-6

GQA_HEADS = 16
GQA_KV_HEADS = 4
GQA_GROUP = GQA_HEADS // GQA_KV_HEADS
GQA_Q_W = GQA_HEADS * HEAD_DIM
GQA_KV_W = GQA_KV_HEADS * HEAD_DIM

DIFF_HEADS = 8
DIFF_QK_W = DIFF_HEADS * 2 * HEAD_DIM
DIFF_V_DIM = 2 * HEAD_DIM
DIFF_BRANCH = DIFF_HEADS * DIFF_V_DIM

MLSTM_INNER = 2 * D_MODEL
MLSTM_HEADS = 4
MLSTM_HEAD_DIM = MLSTM_INNER // MLSTM_HEADS
MLSTM_QKV_BLOCK = 4
MLSTM_CONV_K = 4

LANES = 128
SUBLANES = 8
VMEM_LIMIT_BYTES = 56 * 1024 * 1024

TOKEN_TILE = 512
PROJ_OUT_TILE = 1024
PROJ_N_TILE = 256
ATTN_Q_TILE = 256
ATTN_SHORT_SEQS = 2
MLSTM_CHUNK = 256
HEADWISE_CHUNK = LANES
GATE_W = 4 * MLSTM_HEADS
Q_SCALE = HEAD_DIM ** -0.5 * math.log2(math.e)

F32 = jnp.float32
BF16 = jnp.bfloat16


def _params(*semantics):
    return pltpu.CompilerParams(dimension_semantics=semantics, vmem_limit_bytes=VMEM_LIMIT_BYTES)


def _silu(x):
    return x * (1.0 / (1.0 + jnp.exp(-x)))


def _dot(a, b):
    return jnp.dot(a, b, preferred_element_type=F32)


def _dot_nt(a, b):
    return lax.dot_general(a, b, (((1,), (1,)), ((), ())), preferred_element_type=F32)


class TokenLayout:
    def __init__(self, n_prompt, seq, tile):
        self.prompt_tiles = n_prompt // tile
        self.tiles_per_seq = seq // tile

    def is_prompt(self, m):
        return m < self.prompt_tiles

    def cond_row(self, m):
        return jnp.where(m < self.prompt_tiles, 0, 1 + (m - self.prompt_tiles) // self.tiles_per_seq)

    def prompt_block(self, m):
        return jnp.minimum(m, self.prompt_tiles - 1)

    def sample_block(self, m):
        return jnp.maximum(m - self.prompt_tiles, 0)


def _mod_kernel(cond_ref, w_ref, b_ref, o_ref):
    a = _silu(cond_ref[...]).astype(BF16)
    o_ref[...] = _dot(a, w_ref[...].astype(BF16)) + b_ref[...]


def adaln_all(cond, w_mod, b_mod):
    tn = 3 * D_MODEL
    return pl.pallas_call(
        _mod_kernel,
        grid=(DEPTH, 3 * D_MODEL // tn),
        in_specs=[
            pl.BlockSpec((SUBLANES, D_MODEL), lambda i, n: (0, 0)),
            pl.BlockSpec((None, D_MODEL, tn), lambda i, n: (i, 0, n)),
            pl.BlockSpec((None, 1, tn), lambda i, n: (i, 0, n)),
        ],
        out_specs=pl.BlockSpec((None, SUBLANES, tn), lambda i, n: (i, 0, n)),
        out_shape=jax.ShapeDtypeStruct((DEPTH, SUBLANES, 3 * D_MODEL), F32),
        compiler_params=_params("parallel", "parallel"),
        name="adaln_mod",
    )(cond, w_mod, b_mod.reshape(DEPTH, 1, 3 * D_MODEL))


def _swap_half_heads(y):
    w = y.shape[-1]
    lane = lax.broadcasted_iota(jnp.int32, y.shape, 1)
    upper = (lane & (HEAD_DIM // 2)) != 0
    return jnp.where(upper, pltpu.roll(y, HEAD_DIM // 2, 1), pltpu.roll(y, w - HEAD_DIM // 2, 1))


def _x_specs(x, lay, tm):
    if isinstance(x, tuple):
        return [pl.BlockSpec((tm, D_MODEL), lambda m: (lay.prompt_block(m), 0)),
                pl.BlockSpec((tm, D_MODEL), lambda m: (lay.sample_block(m), 0))], list(x)
    return [pl.BlockSpec((tm, D_MODEL), lambda m: (m, 0))], [x]


def _resident_spec(stacked, layer):
    return pl.BlockSpec((None,) + stacked.shape[1:], lambda m: (layer, 0, 0), pipeline_mode=pl.Buffered(1))


def _two_paths(is_prompt, path):
    pl.when(is_prompt)(functools.partial(path, True))
    pl.when(jnp.logical_not(is_prompt))(functools.partial(path, False))


def _proj_in_kernel(*refs, segs, layout, has_norm, has_rope, x_pair):
    it = iter(refs)
    x_refs = (next(it), next(it)) if x_pair else (next(it),) * 2
    g_ref, mod_ref, w_ref = next(it), next(it), next(it)
    if has_rope:
        cos_ref, sin_ref = next(it), next(it)
    if has_norm:
        gmat_ref, qg_ref, kg_ref = next(it), next(it), next(it)
    prev_refs = [next(it) if kind != "full" and kind[2] else None for _, _, kind in segs]
    out_refs = []
    for _, _, kind in segs:
        if kind == "full":
            ref = next(it)
            out_refs.append((ref, ref, None))
        else:
            has_prompt, has_t, _ = kind
            out_refs.append((next(it) if has_prompt else None, next(it), next(it) if has_t else None))
    cw = PROJ_N_TILE

    def store_transposed(ref, acc, col0):
        seq_len = ref.shape[3]
        for s in range(ref.shape[0]):
            ref[s, ref.shape[1] - 1, col0:col0 + cw, :] = acc[s * seq_len:(s + 1) * seq_len, :].T.astype(ref.dtype)

    def path(prompt):
        x = x_refs[0 if prompt else 1][...]
        y = x * lax.rsqrt(jnp.mean(x * x, axis=-1, keepdims=True) + NORM_EPS) * g_ref[...]
        shift = mod_ref[:, 0:D_MODEL]
        scale = mod_ref[:, D_MODEL:2 * D_MODEL]
        h = (y * (1.0 + scale) + shift).astype(BF16)
        n_chunks = sum(s[0] for s in segs) // cw
        accs = [_dot(h, w_ref[:, i * cw:(i + 1) * cw].astype(BF16)) for i in range(n_chunks)]
        col = 0
        for (width, mode, _), (p_ref, s_ref, t_ref), prev_ref in zip(segs, out_refs, prev_refs):
            o_ref = p_ref if prompt else s_ref
            if prompt and prev_ref is not None:
                t_ref[:, 0:prev_ref.shape[1]] = prev_ref[...]
            for c in range(width // cw):
                acc = accs[col // cw]
                col += cw
                if mode in ("q_norm", "k_norm"):
                    ss = _dot((acc * acc).astype(BF16), gmat_ref[...])
                    gvec = qg_ref[...] if mode == "q_norm" else kg_ref[...]
                    acc = acc * lax.rsqrt(ss * (1.0 / HEAD_DIM) + NORM_EPS) * gvec
                if has_rope and mode != "plain" and not prompt:
                    acc = acc * cos_ref[...] + _swap_half_heads(acc) * sin_ref[...]
                if mode in ("q_norm", "q"):
                    acc = acc * Q_SCALE
                if o_ref is not None:
                    o_ref[:, c * cw:(c + 1) * cw] = acc.astype(o_ref.dtype)
                if prompt and t_ref is not None:
                    store_transposed(t_ref, acc, c * cw)

    _two_paths(layout.is_prompt(pl.program_id(0)), path)


def proj_in(x, norm_g, mod4, layer, w, w_layer, segs, n_prompt, sample_seq, rope_tabs=None, head_norm=None,
            tile=TOKEN_TILE):
    tm, cw = tile, PROJ_N_TILE
    lay = TokenLayout(n_prompt, sample_seq, tm)
    T = x[0].shape[0] + x[1].shape[0] if isinstance(x, tuple) else x.shape[0]
    N = w.shape[2]
    assert N == sum(s[0] for s in segs) and all(s[0] % cw == 0 for s in segs)

    in_specs, args = _x_specs(x, lay, tm)
    in_specs += [
        pl.BlockSpec((1, D_MODEL), lambda m: (0, 0)),
        pl.BlockSpec((None, None, 1, 3 * D_MODEL), lambda m: (layer, lay.cond_row(m), 0, 0)),
        _resident_spec(w, w_layer),
    ]
    args += [norm_g.reshape(1, D_MODEL), mod4, w]
    if rope_tabs is not None:
        def pos_block(m):
            return (lay.sample_block(m) % lay.tiles_per_seq, 0)
        in_specs += [pl.BlockSpec((tm, cw), pos_block), pl.BlockSpec((tm, cw), pos_block)]
        args += list(rope_tabs)
    if head_norm is not None:
        in_specs += [pl.BlockSpec((cw, cw), lambda m: (0, 0)),
                     pl.BlockSpec((1, cw), lambda m: (0, 0)),
                     pl.BlockSpec((1, cw), lambda m: (0, 0))]
        args += list(head_norm)

    out_specs, out_shape, kern_segs = [], [], []
    t_block = lambda m: (lay.prompt_block(m), 0, 0, 0)
    for width, mode, out in segs:
        if isinstance(out, tuple):
            prompt_dt, t_dt, seq_len, prev = out
            kern_segs.append((width, mode, (prompt_dt is not None, t_dt is not None, prev is not None)))
            if prev is not None:
                in_specs.append(pl.BlockSpec((tm // seq_len,) + prev.shape[1:], t_block))
                args.append(prev)
            if prompt_dt is not None:
                out_specs.append(pl.BlockSpec((tm, width), lambda m: (lay.prompt_block(m), 0)))
                out_shape.append(jax.ShapeDtypeStruct((n_prompt, width), prompt_dt))
            out_specs.append(pl.BlockSpec((tm, width), lambda m: (lay.sample_block(m), 0)))
            out_shape.append(jax.ShapeDtypeStruct((T - n_prompt, width), BF16))
            if t_dt is not None:
                layers = 1 if prev is None else prev.shape[1] + 1
                out_specs.append(pl.BlockSpec((tm // seq_len, layers, width, seq_len), t_block))
                out_shape.append(jax.ShapeDtypeStruct((n_prompt // seq_len, layers, width, seq_len), t_dt))
        else:
            kern_segs.append((width, mode, "full"))
            out_specs.append(pl.BlockSpec((tm, width), lambda m: (m, 0)))
            out_shape.append(jax.ShapeDtypeStruct((T, width), out))

    kern = functools.partial(_proj_in_kernel, segs=tuple(kern_segs), layout=lay, x_pair=isinstance(x, tuple),
                             has_norm=head_norm is not None, has_rope=rope_tabs is not None)
    return pl.pallas_call(
        kern,
        grid=(T // tm,),
        in_specs=in_specs,
        out_specs=out_specs,
        out_shape=out_shape,
        compiler_params=_params("arbitrary"),
        name="proj_in",
    )(*args)


def _proj_out_kernel(*refs, layout, with_skip, x_pair, final):
    it = iter(refs)
    o_refs = (next(it), next(it))
    if with_skip:
        xc_ref, skip_ref = next(it), next(it)
    z_ref = next(it)
    x_refs = (next(it), next(it)) if x_pair else (next(it),) * 2
    mod_ref, w_ref = next(it), next(it)
    if final:
        fg_ref = next(it)
    out_refs = (next(it), next(it)) if final else (next(it),) * 2

    def path(prompt):
        sel = 0 if prompt else 1
        a = o_refs[sel][...].astype(F32)
        if with_skip:
            a = a + skip_ref[...] * xc_ref[...].astype(F32)
        a = (a * _silu(z_ref[...].astype(F32))).astype(BF16)
        gate = mod_ref[:, 2 * D_MODEL:3 * D_MODEL]
        x = x_refs[sel][...] + gate * _dot(a, w_ref[...].astype(BF16))
        if final:
            x = x * lax.rsqrt(jnp.mean(x * x, axis=-1, keepdims=True) + NORM_EPS) * fg_ref[...]
        out_refs[sel][...] = x

    _two_paths(layout.is_prompt(pl.program_id(0)), path)


def proj_out(o_prompt, o_sample, z, x, mod4, layer, w_out, w_layer, n_prompt, sample_seq, xc=None, skip=None,
             final_g=None, tile=TOKEN_TILE):
    T = z.shape[0]
    K = o_prompt.shape[1]
    tm = tile
    lay = TokenLayout(n_prompt, sample_seq, tm)
    row = lambda m: (m, 0)
    prompt_row = lambda m: (lay.prompt_block(m), 0)
    sample_row = lambda m: (lay.sample_block(m), 0)
    in_specs = [pl.BlockSpec((tm, K), prompt_row), pl.BlockSpec((tm, K), sample_row)]
    args = [o_prompt, o_sample]
    if xc is not None:
        in_specs += [pl.BlockSpec((tm, K), row), pl.BlockSpec((1, K), lambda m: (0, 0))]
        args += [xc, skip.reshape(1, K)]
    in_specs.append(pl.BlockSpec((tm, K), row))
    args.append(z)
    x_specs, x_args = _x_specs(x, lay, tm)
    in_specs += x_specs + [
        pl.BlockSpec((None, None, 1, 3 * D_MODEL), lambda m: (layer, lay.cond_row(m), 0, 0)),
        _resident_spec(w_out, w_layer),
    ]
    args += x_args + [mod4, w_out]
    if final_g is not None:
        in_specs.append(pl.BlockSpec((1, D_MODEL), lambda m: (0, 0)))
        args.append(final_g.reshape(1, D_MODEL))
        out_specs = [pl.BlockSpec((tm, D_MODEL), prompt_row), pl.BlockSpec((tm, D_MODEL), sample_row)]
        out_shape = [jax.ShapeDtypeStruct((n_prompt, D_MODEL), F32),
                     jax.ShapeDtypeStruct((T - n_prompt, D_MODEL), F32)]
    else:
        out_specs = pl.BlockSpec((tm, D_MODEL), row)
        out_shape = jax.ShapeDtypeStruct((T, D_MODEL), F32)
    kern = functools.partial(_proj_out_kernel, layout=lay, with_skip=xc is not None,
                             x_pair=isinstance(x, tuple), final=final_g is not None)
    return pl.pallas_call(
        kern,
        grid=(T // tm,),
        in_specs=in_specs,
        out_specs=out_specs,
        out_shape=out_shape,
        compiler_params=_params("arbitrary"),
        name="proj_out",
    )(*args)


ONES_ROWS = 16


def _low_half(shape):
    return lax.broadcasted_iota(jnp.int32, shape, 1) < HEAD_DIM


def _attend_group(items, rows, values_transposed=False):
    scores = [[_dot_nt(q, k) for k in k_parts] for q, k_parts, _, _ in items]
    probs = []
    for s in scores:
        m = functools.reduce(jnp.maximum, [jnp.max(x, axis=-1, keepdims=True) for x in s])
        probs.append([jnp.exp2(x - m).astype(BF16) for x in s])
    out = []
    for p_parts, (_, _, v_ref, head) in zip(probs, items):
        acc = None
        for p, r in zip(p_parts, rows):
            t = _dot_nt(v_ref[head, :, r], p) if values_transposed else _dot(p, v_ref[head, r, :])
            acc = t if acc is None else acc + t
        out.append(acc)
    return out


def _group_size(n_keys):
    return 16 if n_keys <= 512 else 4


def _row_parts(k_ref, kc_ref):
    n_new = k_ref.shape[0]
    rows = [slice(0, n_new)]
    if kc_ref is not None:
        rows.append(slice(n_new, n_new + kc_ref.shape[0]))
    return rows


def _per_sequence(body, seqs, row_refs, shared, scratch):
    for s in range(seqs):
        def view(ref):
            n = ref.shape[0] // seqs
            return ref if seqs == 1 else ref.at[pl.ds(s * n, n)]
        body(*[view(r) for r in row_refs], *shared, *[r.at[s] for r in scratch])


def _gqa_attn_kernel(*refs, has_ctx, seqs):
    if has_ctx:
        q_ref, k_ref, v_ref, kc_ref, vc_ref, o_ref, *tables = refs
    else:
        q_ref, k_ref, v_ref, o_ref, *tables = refs
        kc_ref = vc_ref = None
    _per_sequence(_gqa_attn_sequence, seqs, (q_ref, k_ref, v_ref, o_ref), (kc_ref, vc_ref), tables)


def _gqa_attn_sequence(q_ref, k_ref, v_ref, o_ref, kc_ref, vc_ref, *tables):
    rows = _row_parts(k_ref, kc_ref)
    transposed = len(tables) == 1

    @pl.when(pl.program_id(1) == 0)
    def _():
        for src, r in zip([v_ref, vc_ref][:len(rows)], rows):
            if transposed:
                vt_ref, = tables
                first = lax.broadcasted_iota(jnp.int32, (ONES_ROWS, r.stop - r.start), 0) == 0
                for slab in range(GQA_KV_W // LANES):
                    t = src[:, slab * LANES:(slab + 1) * LANES].astype(F32).T
                    for half in range(LANES // HEAD_DIM):
                        c = slab * (LANES // HEAD_DIM) + half
                        vt_ref[c, 0:HEAD_DIM, r] = t[half * HEAD_DIM:(half + 1) * HEAD_DIM, :].astype(BF16)
                        vt_ref[c, HEAD_DIM:HEAD_DIM + ONES_ROWS, r] = jnp.where(first, 1.0, 0.0).astype(BF16)
            else:
                ve_ref, vo_ref = tables
                for c in range(GQA_KV_HEADS):
                    slab = src[:, (c // 2) * LANES:(c // 2 + 1) * LANES].astype(F32)
                    low = _low_half(slab.shape)
                    if c % 2 == 0:
                        ve = jnp.where(low, slab, 0.0)
                        vo = pltpu.roll(ve, HEAD_DIM, 1)
                    else:
                        vo = jnp.where(low, 0.0, slab)
                        ve = pltpu.roll(vo, HEAD_DIM, 1)
                    ones_e = jnp.where(low, 1.0, 0.0)
                    ve_ref[c, r, 0:LANES] = ve.astype(BF16)
                    ve_ref[c, r, LANES:2 * LANES] = ones_e.astype(BF16)
                    vo_ref[c, r, 0:LANES] = vo.astype(BF16)
                    vo_ref[c, r, LANES:2 * LANES] = (1.0 - ones_e).astype(BF16)

    k_srcs = [k_ref, kc_ref][:len(rows)]
    group = _group_size(rows[-1].stop)
    n_pairs = GQA_HEADS // 2
    for p0 in range(0, n_pairs, group // 2):
        items = []
        for pair in range(p0, p0 + group // 2):
            c = pair // (GQA_GROUP // 2)
            q = q_ref[:, pair * LANES:(pair + 1) * LANES].astype(F32)
            low = _low_half(q.shape)
            q_sw = pltpu.roll(q, HEAD_DIM, 1)
            if c % 2 == 0:
                qe, qo = jnp.where(low, q, 0.0), jnp.where(low, q_sw, 0.0)
            else:
                qe, qo = jnp.where(low, 0.0, q_sw), jnp.where(low, 0.0, q)
            k_parts = [s[:, (c // 2) * LANES:(c // 2 + 1) * LANES].astype(BF16) for s in k_srcs]
            items += [(qe.astype(BF16), k_parts, tables[0], c), (qo.astype(BF16), k_parts, tables[-1], c)]
        accs = _attend_group(items, rows, values_transposed=transposed)
        for i, pair in enumerate(range(p0, p0 + group // 2)):
            if transposed:
                o_t = [a[0:HEAD_DIM] / a[HEAD_DIM:HEAD_DIM + 1] for a in accs[2 * i:2 * i + 2]]
                o = jnp.concatenate(o_t, axis=0).T
            else:
                acc = accs[2 * i] + accs[2 * i + 1]
                o = acc[:, 0:LANES] / acc[:, LANES:2 * LANES]
            o_ref[:, pair * LANES:(pair + 1) * LANES] = o.astype(o_ref.dtype)


def _diff_attn_kernel(*refs, has_ctx, seqs, lam_init):
    if has_ctx:
        q_ref, k_ref, v_ref, kc_ref, vc_ref, lam_ref, g_ref, o_ref, vx_ref = refs
    else:
        q_ref, k_ref, v_ref, lam_ref, g_ref, o_ref, vx_ref = refs
        kc_ref = vc_ref = None
    _per_sequence(functools.partial(_diff_attn_sequence, lam_init=lam_init), seqs, (q_ref, k_ref, v_ref, o_ref),
                  (kc_ref, vc_ref, lam_ref, g_ref), (vx_ref,))


def _diff_attn_sequence(q_ref, k_ref, v_ref, o_ref, kc_ref, vc_ref, lam_ref, g_ref, vx_ref, *, lam_init):
    rows = _row_parts(k_ref, kc_ref)

    @pl.when(pl.program_id(1) == 0)
    def _():
        for src, r in zip([v_ref, vc_ref][:len(rows)], rows):
            for h in range(DIFF_HEADS):
                vx_ref[h, r, 0:LANES] = src[:, h * LANES:(h + 1) * LANES].astype(BF16)
                vx_ref[h, r, LANES:2 * LANES] = jnp.ones((r.stop - r.start, LANES), BF16)

    lv = lam_ref[...]
    s1 = jnp.sum(lv[0:1] * lv[1:2], axis=-1, keepdims=True)
    s2 = jnp.sum(lv[2:3] * lv[3:4], axis=-1, keepdims=True)
    lam = jnp.exp(s1) - jnp.exp(s2) + lam_init
    k_srcs = [k_ref, kc_ref][:len(rows)]
    group = _group_size(rows[-1].stop)
    for h0 in range(0, DIFF_HEADS, group // 2):
        items = []
        for h in range(h0, h0 + group // 2):
            q = q_ref[:, h * LANES:(h + 1) * LANES].astype(F32)
            low = _low_half(q.shape)
            k_parts = [s[:, h * LANES:(h + 1) * LANES].astype(BF16) for s in k_srcs]
            items += [(jnp.where(low, q, 0.0).astype(BF16), k_parts, vx_ref, h),
                      (jnp.where(low, 0.0, q).astype(BF16), k_parts, vx_ref, h)]
        accs = _attend_group(items, rows)
        for i, h in enumerate(range(h0, h0 + group // 2)):
            a0, a1 = accs[2 * i], accs[2 * i + 1]
            o = a0[:, 0:LANES] / a0[:, LANES:2 * LANES] - lam * (a1[:, 0:LANES] / a1[:, LANES:2 * LANES])
            o = o * lax.rsqrt(jnp.mean(o * o, axis=-1, keepdims=True) + NORM_EPS) * g_ref[...]
            o_ref[:, h * LANES:(h + 1) * LANES] = (o * (1.0 - lam_init)).astype(o_ref.dtype)


def attention(kernel, v_scratch, q, k, v, batch, sq, q_row0, ctx=None, extra=()):
    tq = ATTN_Q_TILE
    nq = sq // tq
    seqs = ATTN_SHORT_SEQS if nq == 1 and ctx is None and batch % ATTN_SHORT_SEQS == 0 else 1
    batch, sq, tq = batch // seqs, sq * seqs, tq * seqs
    off = q_row0 // tq
    in_specs = [
        pl.BlockSpec((tq, q.shape[1]), lambda b, i: (b * nq + i + off, 0)),
        pl.BlockSpec((sq, k.shape[1]), lambda b, i: (b, 0)),
        pl.BlockSpec((sq, v.shape[1]), lambda b, i: (b, 0)),
    ]
    args = [q, k, v]
    sk = sq
    if ctx is not None:
        kc, vc, layer = ctx
        sk += kc.shape[2]
        in_specs += [pl.BlockSpec((None, None) + kc.shape[2:], lambda b, i: (b, layer, 0, 0)),
                     pl.BlockSpec((None, None) + vc.shape[2:], lambda b, i: (b, layer, 0, 0))]
        args += [kc, vc]
    for a in extra:
        in_specs.append(pl.BlockSpec(a.shape, lambda b, i: (0, 0)))
    args += list(extra)
    return pl.pallas_call(
        functools.partial(kernel, has_ctx=ctx is not None, seqs=seqs),
        grid=(batch, nq),
        in_specs=in_specs,
        out_specs=pl.BlockSpec((tq, D_MODEL), lambda b, i: (b * nq + i, 0)),
        out_shape=jax.ShapeDtypeStruct((batch * sq, D_MODEL), BF16),
        scratch_shapes=[pltpu.VMEM((seqs,) + tuple(sk // seqs if d == "keys" else d for d in shape), BF16)
                        for shape in v_scratch],
        compiler_params=_params("parallel", "arbitrary"),
        name="attention",
    )(*args)


def _log_sigmoid(x):
    return jnp.minimum(x, 0.0) - jnp.log1p(jnp.exp(-jnp.abs(x)))


def _mlstm_in_kernel(cur_ref, prev_ref, next_ref, ng_ref, mod_ref, w_ref, cw_ref, cb_ref, bq_ref, bk_ref, bkt_ref,
                     bv_ref, wg_ref, bg_ref, xc_ref, q_ref, kt_ref, v_ref, gcol_ref, grow_ref, z_ref,
                     *, n_prompt_tiles, tiles_per_seq):
    L = MLSTM_CHUNK
    cw = PROJ_N_TILE
    t = pl.program_id(0)
    st = jnp.maximum(t - n_prompt_tiles, 0) % tiles_per_seq
    first = (t < n_prompt_tiles) | (st == 0)
    last = (t < n_prompt_tiles) | (st == tiles_per_seq - 1)
    n_ext = L + 2 * SUBLANES
    x = jnp.concatenate([prev_ref[...], cur_ref[...], next_ref[...]], axis=0)
    y = x * lax.rsqrt(jnp.mean(x * x, axis=-1, keepdims=True) + NORM_EPS) * ng_ref[...]
    h = (y * (1.0 + mod_ref[:, D_MODEL:2 * D_MODEL]) + mod_ref[:, 0:D_MODEL]).astype(BF16)
    n_xm = MLSTM_INNER // cw
    xm_chunks = [_dot(h, w_ref[:, i * cw:(i + 1) * cw].astype(BF16)) for i in range(n_xm)]
    z_chunks = [_dot(h[SUBLANES:SUBLANES + L], w_ref[:, (n_xm + i) * cw:(n_xm + i + 1) * cw].astype(BF16))
                for i in range(n_xm)]
    for i, zc in enumerate(z_chunks):
        z_ref[:, i * cw:(i + 1) * cw] = zc.astype(z_ref.dtype)
    row_e = lax.broadcasted_iota(jnp.int32, (n_ext, 1), 0)
    inside = ((row_e >= SUBLANES) | jnp.logical_not(first)) & ((row_e < SUBLANES + L) | jnp.logical_not(last))
    ext = jnp.where(inside, jnp.concatenate(xm_chunks, axis=1), 0.0)
    cur = ext[SUBLANES:SUBLANES + L]
    pad_l = MLSTM_CONV_K // 2
    acc = cb_ref[...] + cw_ref[pad_l:pad_l + 1, :] * cur
    for j in range(MLSTM_CONV_K):
        if j == pad_l:
            continue
        sh = (pad_l - j) % n_ext
        acc = acc + cw_ref[j:j + 1, :] * pltpu.roll(ext, sh, 0)[SUBLANES:SUBLANES + L]
    xc = _silu(acc)
    xc_ref[...] = xc.astype(xc_ref.dtype)

    g = jnp.zeros((L, LANES), F32)
    for c in range(MLSTM_INNER // HEADWISE_CHUNK):
        sl = slice(c * HEADWISE_CHUNK, (c + 1) * HEADWISE_CHUNK)
        xcb = xc[:, sl].astype(BF16)
        xmb = cur[:, sl].astype(BF16)
        qc = _dot(xcb, bq_ref[c]).astype(BF16)
        kc = _dot(xcb, bk_ref[c]).astype(BF16)
        vc = _dot(xmb, bv_ref[c]).astype(BF16)
        q_ref[:, sl] = qc
        v_ref[:, sl] = vc
        kt_ref[sl, :] = _dot_nt(bkt_ref[c], xcb).astype(BF16)
        g = g + _dot(qc, wg_ref[0, sl, :]) + _dot(kc, wg_ref[1, sl, :]) + _dot(vc, wg_ref[2, sl, :])
    g = g + bg_ref[...]

    ls = _log_sigmoid(g)
    row = lax.broadcasted_iota(jnp.int32, (L, LANES), 0)
    pre, suf = ls, ls
    s = 1
    while s < L:
        pre = pre + jnp.where(row >= s, pltpu.roll(pre, s, 0), 0.0)
        suf = suf + jnp.where(row < L - s, pltpu.roll(suf, L - s, 0), 0.0)
        s *= 2
    lane = lax.broadcasted_iota(jnp.int32, (L, LANES), 1)
    H = MLSTM_HEADS
    out = jnp.where((lane >= H) & (lane < 2 * H), pre, jnp.where((lane >= 3 * H) & (lane < 4 * H), suf, g))
    gcol_ref[...] = out[:, 0:GATE_W]
    grow_ref[...] = out.T[0:GATE_W, :]


def mlstm_in(x, norm_g, mod4, layer, w, w_layer, conv_w, conv_b, bq, bk, bkt, bv, wg, bg, n_prompt, sample_seq):
    T = x.shape[0]
    L = MLSTM_CHUNK
    n_tiles = T // L
    r8 = L // SUBLANES
    n_blk8 = T // SUBLANES
    C = MLSTM_INNER
    lay = TokenLayout(n_prompt, sample_seq, L)
    kern = functools.partial(_mlstm_in_kernel, n_prompt_tiles=n_prompt // L, tiles_per_seq=sample_seq // L)
    const3 = lambda t: (0, 0, 0)
    return pl.pallas_call(
        kern,
        grid=(n_tiles,),
        in_specs=[
            pl.BlockSpec((L, D_MODEL), lambda t: (t, 0)),
            pl.BlockSpec((SUBLANES, D_MODEL), lambda t: (jnp.maximum(t * r8 - 1, 0), 0)),
            pl.BlockSpec((SUBLANES, D_MODEL), lambda t: (jnp.minimum((t + 1) * r8, n_blk8 - 1), 0)),
            pl.BlockSpec((1, D_MODEL), lambda t: (0, 0)),
            pl.BlockSpec((None, None, 1, 3 * D_MODEL), lambda t: (layer, lay.cond_row(t), 0, 0)),
            _resident_spec(w, w_layer),
            pl.BlockSpec((MLSTM_CONV_K, C), lambda t: (0, 0)),
            pl.BlockSpec((1, C), lambda t: (0, 0)),
            pl.BlockSpec(bq.shape, const3),
            pl.BlockSpec(bk.shape, const3),
            pl.BlockSpec(bkt.shape, const3),
            pl.BlockSpec(bv.shape, const3),
            pl.BlockSpec(wg.shape, const3),
            pl.BlockSpec((1, LANES), lambda t: (0, 0)),
        ],
        out_specs=[
            pl.BlockSpec((L, C), lambda t: (t, 0)),
            pl.BlockSpec((L, C), lambda t: (t, 0)),
            pl.BlockSpec((None, C, L), lambda t: (t, 0, 0)),
            pl.BlockSpec((L, C), lambda t: (t, 0)),
            pl.BlockSpec((L, GATE_W), lambda t: (t, 0)),
            pl.BlockSpec((None, GATE_W, L), lambda t: (t, 0, 0)),
            pl.BlockSpec((L, C), lambda t: (t, 0)),
        ],
        out_shape=[
            jax.ShapeDtypeStruct((T, C), BF16),
            jax.ShapeDtypeStruct((T, C), BF16),
            jax.ShapeDtypeStruct((n_tiles, C, L), BF16),
            jax.ShapeDtypeStruct((T, C), BF16),
            jax.ShapeDtypeStruct((T, GATE_W), F32),
            jax.ShapeDtypeStruct((n_tiles, GATE_W, L), F32),
            jax.ShapeDtypeStruct((T, C), BF16),
        ],
        compiler_params=_params("parallel"),
        name="mlstm_in",
    )(x, x, x, norm_g.reshape(1, D_MODEL), mod4, w, conv_w, conv_b.reshape(1, C), bq, bk, bkt, bv, wg, bg)


def _gate_cols(gcol, h):
    lane = lax.broadcasted_iota(jnp.int32, gcol.shape, 1)
    pick = lambda idx: jnp.sum(jnp.where(lane == idx, gcol, 0.0), axis=-1, keepdims=True)
    return pick(MLSTM_HEADS + h), pick(3 * MLSTM_HEADS + h)


def _chunk_weights(qk, b_col, b_row, i_row, tot, causal_mask, m_prev):
    ar = i_row - b_row
    d = jnp.where(causal_mask, b_col + ar, -jnp.inf)
    inter = b_col + m_prev
    mt = jnp.maximum(jnp.max(d, axis=-1, keepdims=True), inter)
    s = qk * jnp.exp(d - mt)
    g = tot + ar
    m_new = jnp.maximum(tot + m_prev, jnp.max(g, axis=-1, keepdims=True))
    w = jnp.exp(g - m_new)
    w_hi = w.astype(BF16).astype(F32)
    rid = lax.broadcasted_iota(jnp.int32, (SUBLANES, w.shape[-1]), 0)
    w8 = jnp.where(rid == 0, w_hi, jnp.where(rid == 1, w - w_hi, 0.0)).astype(BF16)
    return dict(s=s.astype(BF16), den=jnp.sum(s, axis=-1, keepdims=True), mt=mt, inter=inter, w=w, w8=w8,
                m_new=m_new, decay_arg=tot + m_prev - m_new)


def _chunk_matmuls(wts, q, kt, v, C_prev):
    num = _dot(wts["s"], v)
    qC = None if C_prev is None else _dot(q, C_prev.astype(BF16))
    C_add = _dot((kt.astype(F32) * wts["w"]).astype(BF16), v)
    n8 = _dot_nt(wts["w8"], kt)
    return num, qC, C_add, n8[0:1] + n8[1:2]


def _chunk_finish(wts, mats, q, C_prev, n_prev):
    num, qC, C_new, n_new = mats
    den, mt = wts["den"], wts["mt"]
    if C_prev is not None:
        w_inter = jnp.exp(wts["inter"] - mt)
        num = num + w_inter * qC
        den = den + w_inter * jnp.sum(q.astype(F32) * n_prev, axis=-1, keepdims=True)
        decay = jnp.exp(wts["decay_arg"])
        C_new = decay * C_prev + C_new
        n_new = decay * n_prev + n_new
    h = num / jnp.maximum(jnp.abs(den), jnp.exp(-mt))
    return h, C_new, n_new, wts["m_new"]


def _chunk_dir(qk, q, kt, v, b_col, b_row, i_row, tot, causal_mask, m_prev, C_prev, n_prev):
    wts = _chunk_weights(qk, b_col, b_row, i_row, tot, causal_mask, m_prev)
    return _chunk_finish(wts, _chunk_matmuls(wts, q, kt, v, C_prev), q, C_prev, n_prev)


def _head_layernorm(h, g):
    mu = jnp.mean(h, axis=-1, keepdims=True)
    hc = h - mu
    var = jnp.mean(hc * hc, axis=-1, keepdims=True)
    return hc * lax.rsqrt(var + NORM_EPS) * g


def _chunk_masks(L):
    t_idx = lax.broadcasted_iota(jnp.int32, (L, L), 0)
    s_idx = lax.broadcasted_iota(jnp.int32, (L, L), 1)
    return s_idx <= t_idx, s_idx >= t_idx


def _mlstm_prompt_kernel(q_ref, kt_ref, v_ref, gcol_ref, grow_ref, mhg_ref,
                         hn_ref, C_ref, n_ref, m_ref):
    L = MLSTM_CHUNK
    H, DH = MLSTM_HEADS, MLSTM_HEAD_DIM
    fmask, bmask = _chunk_masks(L)
    zero = jnp.zeros((1, 1), F32)
    lane = lax.broadcasted_iota(jnp.int32, (1, LANES), 1)
    col = lambda idx: gcol_ref[:, idx:idx + 1]
    row = lambda idx: grow_ref[idx:idx + 1, :]
    for h in range(H):
        ch = slice(h * DH, (h + 1) * DH)
        q, kt, v = q_ref[:, ch], kt_ref[ch, :], v_ref[:, ch]
        qk = _dot(q, kt)
        bf_row, bb_row = row(H + h), row(3 * H + h)
        hf, Cf, nf, mf = _chunk_dir(qk, q, kt, v, col(H + h), bf_row, row(h), bf_row[:, L - 1:L], fmask,
                                    zero, None, None)
        hb, Cb, nb, mb = _chunk_dir(qk, q, kt, v, col(3 * H + h), bb_row, row(2 * H + h), bb_row[:, 0:1], bmask,
                                    zero, None, None)
        hn_ref[:, ch] = _head_layernorm(hf + hb, mhg_ref[:, ch]).astype(hn_ref.dtype)
        C_ref[0, h] = Cf
        C_ref[1, h] = Cb
        n_ref[0, h] = nf
        n_ref[1, h] = nb
        m_ref[h] = jnp.where(lane == 0, mf, jnp.where(lane == 1, mb, 0.0))


def mlstm_prompt(q, kt, v, gcol, grow, mh_g, batch):
    L, H, DH, C = MLSTM_CHUNK, MLSTM_HEADS, MLSTM_HEAD_DIM, MLSTM_INNER
    return pl.pallas_call(
        _mlstm_prompt_kernel,
        grid=(batch,),
        in_specs=[
            pl.BlockSpec((L, C), lambda b: (b, 0)),
            pl.BlockSpec((None, C, L), lambda b: (b, 0, 0)),
            pl.BlockSpec((L, C), lambda b: (b, 0)),
            pl.BlockSpec((L, GATE_W), lambda b: (b, 0)),
            pl.BlockSpec((None, GATE_W, L), lambda b: (b, 0, 0)),
            pl.BlockSpec((1, C), lambda b: (0, 0)),
        ],
        out_specs=[
            pl.BlockSpec((L, C), lambda b: (b, 0)),
            pl.BlockSpec((None, None, 2, H, DH, DH), lambda b: (b, 0, 0, 0, 0, 0)),
            pl.BlockSpec((None, 2, H, 1, DH), lambda b: (b, 0, 0, 0, 0)),
            pl.BlockSpec((None, H, 1, LANES), lambda b: (b, 0, 0, 0)),
        ],
        out_shape=[
            jax.ShapeDtypeStruct((batch * L, C), BF16),
            jax.ShapeDtypeStruct((batch, 1, 2, H, DH, DH), F32),
            jax.ShapeDtypeStruct((batch, 2, H, 1, DH), F32),
            jax.ShapeDtypeStruct((batch, H, 1, LANES), F32),
        ],
        compiler_params=_params("parallel"),
        name="mlstm_prompt",
    )(q, kt, v, gcol, grow, mh_g.reshape(1, C))


def _mlstm_sample_kernel(q_ref, kt_ref, v_ref, gcol_ref, grow_ref, mhg_ref, C0_ref, n0_ref, m0_ref,
                         hn_ref, hf_ref, hb_ref, C_ref, n_ref, m_ref, *, n_chunks):
    L = MLSTM_CHUNK
    H = MLSTM_HEADS
    h = pl.program_id(1)
    fmask, bmask = _chunk_masks(L)
    C_ref[...] = C0_ref[...]
    n_ref[...] = n0_ref[...]
    m_ref[...] = m0_ref[...]

    def one(c, direction):
        rows = pl.ds(pl.multiple_of(c * L, L), L)
        q, kt, v = q_ref[rows, :], kt_ref[c], v_ref[rows, :]
        qk = _dot(q, kt)
        gcol = gcol_ref[rows, :]
        b_col = _gate_cols(gcol, h)[direction]
        grow = lambda idx: grow_ref[c, pl.ds(idx, 1), :]
        b_row = grow((2 * direction + 1) * H + h)
        i_row = grow(2 * direction * H + h)
        tot = b_row[:, L - 1:L] if direction == 0 else b_row[:, 0:1]
        mask = fmask if direction == 0 else bmask
        m_prev = m_ref[direction][:, 0:1]
        hh, C_new, n_new, m_new = _chunk_dir(qk, q, kt, v, b_col, b_row, i_row, tot, mask, m_prev,
                                             C_ref[direction], n_ref[direction])
        C_ref[direction] = C_new
        n_ref[direction] = n_new
        m_ref[direction] = jnp.broadcast_to(m_new, (1, LANES))
        (hf_ref if direction == 0 else hb_ref)[rows, :] = hh

    def body(i, carry):
        one(i, 0)
        one(n_chunks - 1 - i, 1)
        return carry

    lax.fori_loop(0, n_chunks, body, 0)

    def norm(c, carry):
        rows = pl.ds(pl.multiple_of(c * L, L), L)
        hn_ref[rows, :] = _head_layernorm(hf_ref[rows, :] + hb_ref[rows, :], mhg_ref[...]).astype(hn_ref.dtype)
        return carry

    lax.fori_loop(0, n_chunks, norm, 0)


def mlstm_sample(q, kt, v, gcol, grow, mh_g, C0, n0, m0, n_prompt, batch, seq):
    L, H, DH = MLSTM_CHUNK, MLSTM_HEADS, MLSTM_HEAD_DIM
    nc = seq // L
    row_off = n_prompt // seq
    tile_off = n_prompt // L // nc
    return pl.pallas_call(
        functools.partial(_mlstm_sample_kernel, n_chunks=nc),
        grid=(batch, H),
        in_specs=[
            pl.BlockSpec((seq, DH), lambda b, h: (b + row_off, h)),
            pl.BlockSpec((nc, DH, L), lambda b, h: (b + tile_off, h, 0)),
            pl.BlockSpec((seq, DH), lambda b, h: (b + row_off, h)),
            pl.BlockSpec((seq, GATE_W), lambda b, h: (b + row_off, 0)),
            pl.BlockSpec((nc, GATE_W, L), lambda b, h: (b + tile_off, 0, 0)),
            pl.BlockSpec((1, DH), lambda b, h: (0, h)),
            pl.BlockSpec((None, 2, None, DH, DH), lambda b, h: (b, 0, h, 0, 0)),
            pl.BlockSpec((None, 2, None, 1, DH), lambda b, h: (b, 0, h, 0, 0)),
            pl.BlockSpec((None, None, 2, 1, LANES), lambda b, h: (b, h, 0, 0, 0)),
        ],
        out_specs=pl.BlockSpec((seq, DH), lambda b, h: (b, h)),
        out_shape=jax.ShapeDtypeStruct((batch * seq, MLSTM_INNER), BF16),
        scratch_shapes=[
            pltpu.VMEM((seq, DH), F32),
            pltpu.VMEM((seq, DH), F32),
            pltpu.VMEM((2, DH, DH), F32),
            pltpu.VMEM((2, 1, DH), F32),
            pltpu.VMEM((2, 1, LANES), F32),
        ],
        compiler_params=_params("parallel", "parallel"),
        name="mlstm_sample",
    )(q, kt, v, gcol, grow, mh_g.reshape(1, MLSTM_INNER), C0, n0, m0)


def _rope_tables(seq):
    t = np.arange(seq)
    row = (t // GRID_W).astype(np.float64)
    col = (t % GRID_W).astype(np.float64)
    n_freq = HEAD_DIM // 4
    inv = ROPE_THETA ** (-np.arange(n_freq, dtype=np.float64) / n_freq)
    ang = np.concatenate([row[:, None] * inv, col[:, None] * inv], axis=-1)
    cos, sin = np.cos(ang), np.sin(ang)
    reps = PROJ_N_TILE // HEAD_DIM
    return (jnp.asarray(np.tile(np.concatenate([cos, cos], axis=-1), (1, reps)), F32),
            jnp.asarray(np.tile(np.concatenate([-sin, sin], axis=-1), (1, reps)), F32))


def _block_diag(w):
    blk, ch = MLSTM_QKV_BLOCK, HEADWISE_CHUNK
    w4 = w.reshape(-1, ch // blk, blk, blk)
    rows = w4.transpose(0, 2, 1, 3).reshape(-1, 1, blk, ch)
    full = jnp.broadcast_to(rows, (rows.shape[0], ch // blk, blk, ch)).reshape(-1, ch, ch)
    block_of = jnp.arange(ch) // blk
    return jnp.where(block_of[:, None] == block_of[None, :], full, 0.0)


def _diff_lambda_init(layer_idx):
    return 0.8 - 0.6 * math.exp(-0.3 * layer_idx)


def kernel(x_prompt, x_sample, cache_gqa_k, cache_gqa_v, cache_diff_k, cache_diff_v, state_mlstm_C, state_mlstm_n, state_mlstm_m, c, c_ctx, norm_g, w_mod, b_mod, gqa_w_in, gqa_q_norm_g, gqa_k_norm_g, gqa_w_out, diff_w_in, diff_lambda_q1, diff_lambda_k1, diff_lambda_q2, diff_lambda_k2, diff_subln_g, diff_w_out, mlstm_w_in, mlstm_conv_w, mlstm_conv_b, mlstm_w_q, mlstm_w_k, mlstm_w_v, mlstm_w_gate_f, mlstm_b_gate_f, mlstm_w_gate_b, mlstm_b_gate_b, mlstm_mh_norm_g, mlstm_skip, mlstm_w_out, final_norm_g):
    Bp, Sp, D = x_prompt.shape
    Bs, Ss, _ = x_sample.shape
    Tp, Ts = Bp * Sp, Bs * Ss
    assert Sp == MLSTM_CHUNK and Ss % TOKEN_TILE == 0 and Tp % TOKEN_TILE == 0
    H, DH = MLSTM_HEADS, MLSTM_HEAD_DIM
    past = cache_gqa_k.shape[2]

    x = (x_prompt.reshape(Tp, D), x_sample.reshape(Ts, D))
    cond = jnp.zeros((SUBLANES, D), F32).at[0].set(c_ctx).at[1:1 + Bs].set(c)
    mod4 = adaln_all(cond, w_mod, b_mod).reshape(DEPTH, SUBLANES, 1, 3 * D)
    rope_tabs = _rope_tables(Ss)
    gi = np.arange(PROJ_N_TILE) // HEAD_DIM
    gmat = jnp.asarray(gi[:, None] == gi[None, :], BF16)
    tile_g = lambda g: jnp.tile(g, PROJ_N_TILE // HEAD_DIM).reshape(1, PROJ_N_TILE)
    ctx_gqa = (cache_gqa_k.reshape(Bs, -1, past, GQA_KV_W), cache_gqa_v.reshape(Bs, -1, past, GQA_KV_W))
    ctx_diff = (cache_diff_k.reshape(Bs, -1, past, DIFF_QK_W), cache_diff_v.reshape(Bs, -1, past, DIFF_BRANCH))

    gqa_k_t = gqa_v_t = diff_k_t = None
    diff_v_list, mC_list, mn_list, mm_list = [], [], [], []
    for i in range(DEPTH):
        kind, j = i % N_MIXERS, i // N_MIXERS
        common = dict(n_prompt=Tp, sample_seq=Ss)
        last = dict(final_g=final_norm_g) if i == DEPTH - 1 else {}
        if kind == 0:
            segs = ((GQA_Q_W, "q_norm", BF16), (GQA_KV_W, "k_norm", (BF16, F32, Sp, gqa_k_t)),
                    (GQA_KV_W, "plain", (BF16, F32, Sp, gqa_v_t)), (GQA_Q_W, "plain", BF16))
            q, kp, ks, gqa_k_t, vp, vs, gqa_v_t, z = proj_in(
                x, norm_g[i], mod4, i, gqa_w_in, j, segs, rope_tabs=rope_tabs,
                head_norm=(gmat, tile_g(gqa_q_norm_g[j]), tile_g(gqa_k_norm_g[j])), **common)
            kern = _gqa_attn_kernel
            o_p = attention(kern, [(GQA_KV_HEADS, HEAD_DIM + ONES_ROWS, "keys")], q, kp, vp, Bp, Sp, 0)
            o_s = attention(kern, [(GQA_KV_HEADS, "keys", 2 * LANES)] * 2, q, ks, vs, Bs, Ss, Tp, ctx=(*ctx_gqa, j))
            x = proj_out(o_p, o_s, z, x, mod4, i, gqa_w_out, j, tile=PROJ_OUT_TILE, **common, **last)
        elif kind == 1:
            lam_init = _diff_lambda_init(i)
            segs = ((DIFF_QK_W, "q", BF16), (DIFF_QK_W, "k", (BF16, F32, Sp, diff_k_t)),
                    (DIFF_BRANCH, "plain", (F32, None, Sp, None)), (DIFF_BRANCH, "plain", BF16))
            q, kp, ks, diff_k_t, vp, vs, z = proj_in(x, norm_g[i], mod4, i, diff_w_in, j, segs,
                                                    rope_tabs=rope_tabs, **common)
            lam_vecs = jnp.stack([diff_lambda_q1[j], diff_lambda_k1[j], diff_lambda_q2[j], diff_lambda_k2[j]])
            extra = (lam_vecs, diff_subln_g[j].reshape(1, DIFF_V_DIM))
            scr = [(DIFF_HEADS, "keys", 2 * LANES)]
            kern = functools.partial(_diff_attn_kernel, lam_init=lam_init)
            o_p = attention(kern, scr, q, kp, vp, Bp, Sp, 0, extra=extra)
            o_s = attention(kern, scr, q, ks, vs, Bs, Ss, Tp, ctx=(*ctx_diff, j), extra=extra)
            x = proj_out(o_p, o_s, z, x, mod4, i, diff_w_out, j, tile=PROJ_OUT_TILE, **common, **last)
            diff_v_list.append(vp.reshape(Bp, 1, Sp, DIFF_HEADS, DIFF_V_DIM))
        else:
            bq = _block_diag(mlstm_w_q[j]).astype(BF16)
            bk = _block_diag(mlstm_w_k[j])
            bkt = (bk * (DH ** -0.5)).transpose(0, 2, 1).astype(BF16)
            bv = _block_diag(mlstm_w_v[j]).astype(BF16)
            wg = jnp.concatenate([mlstm_w_gate_f[j][:, :H], mlstm_w_gate_f[j][:, H:],
                                  mlstm_w_gate_b[j][:, :H], mlstm_w_gate_b[j][:, H:]], axis=-1)
            wg = jnp.pad(wg, ((0, 0), (0, LANES - GATE_W))).astype(BF16).reshape(3, MLSTM_INNER, LANES)
            bg = jnp.pad(jnp.concatenate([mlstm_b_gate_f[j], mlstm_b_gate_b[j]]), (0, LANES - GATE_W)).reshape(1, LANES)
            assert not isinstance(x, tuple)
            xc, q, kt, v, gcol, grow, z = mlstm_in(x, norm_g[i], mod4, i, mlstm_w_in, j, mlstm_conv_w[j],
                                                   mlstm_conv_b[j], bq, bk.astype(BF16), bkt, bv, wg, bg, **common)
            hn_p, Cp, np_, mp_ = mlstm_prompt(q, kt, v, gcol, grow, mlstm_mh_norm_g[j], Bp)
            C0 = state_mlstm_C[:, j]
            n0 = state_mlstm_n[:, j].reshape(Bs, 2, H, 1, DH)
            m0 = jnp.broadcast_to(state_mlstm_m[:, j].transpose(0, 2, 1)[..., None, None], (Bs, H, 2, 1, LANES))
            hn_s = mlstm_sample(q, kt, v, gcol, grow, mlstm_mh_norm_g[j], C0, n0, m0, Tp, Bs, Ss)
            x = proj_out(hn_p, hn_s, z, x, mod4, i, mlstm_w_out, j, xc=xc, skip=mlstm_skip[j],
                         **common, **last)
            mC_list.append(Cp)
            mn_list.append(np_.reshape(Bp, 1, 2, H, DH))
            mm_list.append(mp_[:, :, 0, 0:2].transpose(0, 2, 1)[:, None])

    cat1 = lambda parts: parts[0] if len(parts) == 1 else jnp.concatenate(parts, axis=1)
    y_prompt, y_sample = x

    def from_transposed(t, trail):
        t = t.reshape(t.shape[:2] + trail + (Sp,))
        return jnp.moveaxis(t, -1, 2)

    return (y_prompt.reshape(Bp, Sp, D), y_sample.reshape(Bs, Ss, D),
            from_transposed(gqa_k_t, (GQA_KV_HEADS, HEAD_DIM)), from_transposed(gqa_v_t, (GQA_KV_HEADS, HEAD_DIM)),
            from_transposed(diff_k_t, (DIFF_HEADS, 2, HEAD_DIM)), cat1(diff_v_list),
            cat1(mC_list), cat1(mn_list), cat1(mm_list))
```
